```python
import math
import jax
import jax.numpy as jnp
from jax import lax
import numpy as np

D_MODEL = 1024
BATCH = 8
SEQ = 2048
DEPTH = 4
DEC_BATCH = 128
DEC_SEQ = 1
PAST_LEN = 2048
PAGE_SIZE = 128

N_MIXERS = 2
N_NSA = (DEPTH + 1) // 2
N_SSD = DEPTH // 2
D_FF = 4 * D_MODEL
EPS = 1e-6
NSA_HEADS = 16
NSA_HEAD_DIM = D_MODEL // NSA_HEADS
NSA_KV_HEADS = 4
NSA_REP = NSA_HEADS // NSA_KV_HEADS
NSA_KV_W = 2 * NSA_KV_HEADS * NSA_HEAD_DIM
NSA_IN = NSA_HEADS * NSA_HEAD_DIM + 3 * NSA_KV_W + 3 * NSA_HEADS
CMP_BLOCK = 32
CMP_STRIDE = 16
SEL_BLOCK = 64
SEL_TOPK = 8
WINDOW = 512
Q_BLOCK = 128
SSD_D_INNER = 2 * D_MODEL
SSD_HEAD_DIM = 64
SSD_HEADS = SSD_D_INNER // SSD_HEAD_DIM
SSD_GROUPS = 4
SSD_REP = SSD_HEADS // SSD_GROUPS
SSD_STATE = 128
SSD_CONV_W = 4
SSD_CHUNK = 256
SSD_CONV_DIM = SSD_D_INNER + 2 * SSD_GROUPS * SSD_STATE
SSD_IN = SSD_D_INNER + SSD_CONV_DIM + SSD_HEADS
BIG = 1e30
NEG = -1e30

kernel_name = 'nsa_ssd_hybrid_step'


def rmsnorm(x, w):
    xf = x.astype(jnp.float32)
    r = lax.rsqrt(jnp.mean(xf * xf, axis=-1, keepdims=True) + EPS)
    return (xf * r).astype(x.dtype) * w


def adaln(c, w, b):
    m = jax.nn.silu(c) @ w + b
    return jnp.split(m[:, None, :], 6, axis=-1)


def modulate(x, w, shift, scale):
    return rmsnorm(x, w) * (1.0 + scale) + shift


def sq_relu_mlp(h, w1, w2):
    return jnp.square(jax.nn.relu(h @ w1)) @ w2


def split_cols(a, sizes):
    return jnp.split(a, [int(s) for s in np.cumsum(sizes)[:-1]], axis=-1)


def masked_probs(s, mask):
    s = jnp.where(mask, s.astype(jnp.float32), NEG)
    return jax.nn.softmax(s, axis=-1) * mask


def nsa_project(h, w_in):
    B, T, _ = h.shape
    q, kv_c, kv_s, kv_w, gates = split_cols(
        h @ w_in, [NSA_HEADS * NSA_HEAD_DIM, NSA_KV_W, NSA_KV_W, NSA_KV_W, 3 * NSA_HEADS])
    q = q.reshape(B, T, NSA_KV_HEADS, NSA_REP, NSA_HEAD_DIM)
    kv = lambda a: a.reshape(B, T, 2, NSA_KV_HEADS, NSA_HEAD_DIM)
    gates = jax.nn.sigmoid(gates.astype(jnp.float32)).reshape(B, T, NSA_KV_HEADS, NSA_REP, 3)
    return q, kv(kv_c), kv(kv_s), kv(kv_w), gates


def nsa_merge(o_cmp, o_sel, o_win, gates):
    g = gates.astype(o_cmp.dtype)
    return g[..., 0:1] * o_cmp + g[..., 1:2] * o_sel + g[..., 2:3] * o_win


def compress_rows(rows, pe, w1, w2):
    B, T, G, HD = rows.shape
    nc = (T - CMP_BLOCK) // CMP_STRIDE + 1
    idx = np.arange(nc)[:, None] * CMP_STRIDE + np.arange(CMP_BLOCK)[None, :]
    blk = rows[:, idx] + pe[:, None, :]
    blk = jnp.swapaxes(blk, 2, 3).reshape(B, nc, G, CMP_BLOCK * HD)
    return jax.nn.silu(blk @ w1) @ w2


def nsa_core(q, t, kc, vc, gather_sel, n_sel, kw, vw, pos_w):
    B, Q, G, R, HD = q.shape
    scale = HD ** -0.5
    nc = kc.shape[1]
    c_start = np.arange(nc) * CMP_STRIDE
    cmp_end = jnp.asarray(c_start + CMP_BLOCK - 1)
    m_cmp = (cmp_end[None, :] <= t[:, None])[None, :, None, None, :]
    p_cmp = masked_probs(jnp.einsum('bqgrd,bcgd->bqgrc', q, kc) * scale, m_cmp)
    o_cmp = jnp.einsum('bqgrc,bcgd->bqgrd', p_cmp.astype(vc.dtype), vc)
    nj = max(n_sel, SEL_TOPK)
    s_start = np.arange(nj) * SEL_BLOCK
    overlap = jnp.asarray((c_start[:, None] < s_start[None, :] + SEL_BLOCK)
                          & (c_start[:, None] + CMP_BLOCK > s_start[None, :]), jnp.float32)
    imp = jnp.einsum('bqgrc,cj->bqgj', p_cmp, overlap)
    j = jnp.arange(nj)
    cur = t // SEL_BLOCK
    valid = (j[None, :] < n_sel) & (j[None, :] * SEL_BLOCK <= t[:, None])
    forced = (j[None, :] == 0) | (j[None, :] == cur[:, None]) | (j[None, :] == cur[:, None] - 1)
    score = jnp.where(forced[None, :, None, :], BIG, imp)
    score = jnp.where(valid[None, :, None, :], score, NEG)
    top_s, top_idx = lax.top_k(score, SEL_TOPK)
    k_sel, v_sel = gather_sel(top_idx)
    pos_sel = top_idx[..., None] * SEL_BLOCK + jnp.arange(SEL_BLOCK)
    m_sel = (top_s > NEG / 2)[..., None] & (pos_sel <= t[None, :, None, None, None])
    m_sel = m_sel.reshape(B, Q, G, 1, SEL_TOPK * SEL_BLOCK)
    s_sel = jnp.einsum('bqgrd,bqgksd->bqgrks', q, k_sel).reshape(B, Q, G, R, SEL_TOPK * SEL_BLOCK) * scale
    p_sel = masked_probs(s_sel, m_sel).reshape(B, Q, G, R, SEL_TOPK, SEL_BLOCK)
    o_sel = jnp.einsum('bqgrks,bqgksd->bqgrd', p_sel.astype(v_sel.dtype), v_sel)
    d = t[:, None] - pos_w[None, :]
    m_win = ((pos_w[None, :] >= 0) & (d >= 0) & (d <= WINDOW))[None, :, None, None, :]
    p_win = masked_probs(jnp.einsum('bqgrd,bsgd->bqgrs', q, kw) * scale, m_win)
    o_win = jnp.einsum('bqgrs,bsgd->bqgrd', p_win.astype(vw.dtype), vw)
    return o_cmp, o_sel, o_win


def nsa_prompt(h, w_in, w_out, pe, w1, w2):
    B, T, _ = h.shape
    q, kv_c, kv_s, kv_w, gates = nsa_project(h, w_in)
    kc = compress_rows(kv_c[:, :, 0], pe[0], w1[0], w2[0])
    vc = compress_rows(kv_c[:, :, 1], pe[1], w1[1], w2[1])
    n_sel = T // SEL_BLOCK
    sel_blocks = kv_s.reshape(B, n_sel, SEL_BLOCK, 2, NSA_KV_HEADS, NSA_HEAD_DIM)
    b_ix = jnp.arange(B)[:, None, None, None]
    g_ix = jnp.arange(NSA_KV_HEADS)[None, None, :, None]

    def gather_sel(idx):
        blk = sel_blocks[b_ix, jnp.minimum(idx, n_sel - 1), :, :, g_ix, :]
        return blk[..., 0, :], blk[..., 1, :]

    win_pad = jnp.pad(kv_w, ((0, 0), (WINDOW, 0), (0, 0), (0, 0), (0, 0)))
    n_qb = T // Q_BLOCK

    def query_block(args):
        q_blk, g_blk, qb = args
        start = qb * Q_BLOCK
        t = start + jnp.arange(Q_BLOCK)
        win = lax.dynamic_slice_in_dim(win_pad, start, WINDOW + Q_BLOCK, axis=1)
        pos_w = start - WINDOW + jnp.arange(WINDOW + Q_BLOCK)
        o = nsa_core(q_blk, t, kc, vc, gather_sel, n_sel, win[:, :, 0], win[:, :, 1], pos_w)
        return nsa_merge(*o, g_blk)

    to_blocks = lambda a: jnp.swapaxes(a.reshape((B, n_qb, Q_BLOCK) + a.shape[2:]), 0, 1)
    o = lax.map(query_block, (to_blocks(q), to_blocks(gates), jnp.arange(n_qb)))
    o = jnp.swapaxes(o, 0, 1).reshape(B, T, NSA_HEADS * NSA_HEAD_DIM)
    return o @ w_out, kv_c, kv_s, kv_w[:, -min(WINDOW, T):]


def nsa_sample(h, cache_c, cache_s, cache_w, page_table, w_in, w_out, pe, w1, w2):
    B, T, _ = h.shape
    q, kv_c, kv_s, kv_w, gates = nsa_project(h, w_in)
    past = page_table.shape[1] * PAGE_SIZE
    t = past + jnp.arange(T)
    past_c = cache_c[page_table].reshape((B, past) + cache_c.shape[2:])
    rows_c = jnp.concatenate([past_c, kv_c], axis=1)
    kc = compress_rows(rows_c[:, :, 0], pe[0], w1[0], w2[0])
    vc = compress_rows(rows_c[:, :, 1], pe[1], w1[1], w2[1])
    n_sel = -(-(past + T) // SEL_BLOCK)
    n_past_blk = past // SEL_BLOCK
    n_new_blk = n_sel - n_past_blk
    bpp = PAGE_SIZE // SEL_BLOCK
    pool_blocks = cache_s.reshape((-1, SEL_BLOCK) + cache_s.shape[2:])
    new_blocks = jnp.pad(kv_s, ((0, 0), (0, n_new_blk * SEL_BLOCK - T), (0, 0), (0, 0), (0, 0)))
    new_blocks = new_blocks.reshape(B, n_new_blk, SEL_BLOCK, 2, NSA_KV_HEADS, NSA_HEAD_DIM)
    b_ix = jnp.arange(B)[:, None, None, None]
    g_ix = jnp.arange(NSA_KV_HEADS)[None, None, :, None]

    def gather_sel(idx):
        logical = jnp.minimum(idx, n_past_blk - 1)
        phys = page_table[b_ix, logical // bpp] * bpp + logical % bpp
        old = pool_blocks[phys, :, :, g_ix, :]
        new = new_blocks[b_ix, jnp.clip(idx - n_past_blk, 0, n_new_blk - 1), :, :, g_ix, :]
        blk = jnp.where((idx >= n_past_blk)[..., None, None, None], new, old)
        return blk[..., 0, :], blk[..., 1, :]

    keep = cache_w.shape[1]
    win = jnp.concatenate([cache_w, kv_w], axis=1)
    pos_w = past - keep + jnp.arange(keep + T)
    o = nsa_core(q, t, kc, vc, gather_sel, n_sel, win[:, :, 0], win[:, :, 1], pos_w)
    o = nsa_merge(*o, gates).reshape(B, T, NSA_HEADS * NSA_HEAD_DIM)
    return o @ w_out, kv_c, kv_s, win[:, -keep:]


def ssd_scan(x, dt, a, b_in, c_in, init_state):
    B, T, G, R, P = x.shape
    L = math.gcd(SSD_CHUNK, T)
    nc = T // L
    f32 = jnp.float32
    chunk = lambda v: v.reshape((B, nc, L) + v.shape[2:])
    xdt = chunk(x.astype(f32) * dt[..., None])
    bc = chunk(b_in.astype(f32))
    cc = chunk(c_in.astype(f32))
    a_cum = jnp.cumsum(chunk(dt * a), axis=2)
    causal = np.tril(np.ones((L, L), dtype=bool))[:, :, None, None]
    seg = a_cum[:, :, :, None] - a_cum[:, :, None, :]
    decay = jnp.exp(jnp.where(causal, seg, NEG))
    cb = jnp.einsum('bclgn,bcsgn->bclsg', cc, bc)
    y_diag = jnp.einsum('bclsg,bclsgr,bcsgrp->bclgrp', cb, decay, xdt)
    decay_end = jnp.exp(a_cum[:, :, -1:] - a_cum)
    chunk_states = jnp.einsum('bcsgn,bcsgr,bcsgrp->bcgrpn', bc, decay_end, xdt)
    chunk_decay = jnp.exp(a_cum[:, :, -1])

    def step(state, inp):
        cs, cd = inp
        return state * cd[..., None, None] + cs, state

    final, prev = lax.scan(step, init_state.astype(f32),
                           (jnp.swapaxes(chunk_states, 0, 1), jnp.swapaxes(chunk_decay, 0, 1)))
    prev = jnp.swapaxes(prev, 0, 1)
    y_off = jnp.einsum('bclgn,bcgrpn,bclgr->bclgrp', cc, prev, jnp.exp(a_cum))
    return (y_diag + y_off).reshape(B, T, G, R, P), final


def ssd_mixer(h, conv_buf, ssm_state, w_in, conv_w, conv_b, dt_bias, a_log, d_skip, norm_w, w_out):
    B, T, _ = h.shape
    z, xbc, dt = split_cols(h @ w_in, [SSD_D_INNER, SSD_CONV_DIM, SSD_HEADS])
    xp = jnp.concatenate([conv_buf.astype(xbc.dtype), xbc], axis=1)
    conv = conv_b + xp[:, 0:T] * conv_w[0]
    for k in range(1, SSD_CONV_W):
        conv = conv + xp[:, k:k + T] * conv_w[k]
    new_conv = xp[:, T:]
    xbc = jax.nn.silu(conv)
    x, b_in, c_in = split_cols(xbc, [SSD_D_INNER, SSD_GROUPS * SSD_STATE, SSD_GROUPS * SSD_STATE])
    x = x.reshape(B, T, SSD_GROUPS, SSD_REP, SSD_HEAD_DIM)
    b_in = b_in.reshape(B, T, SSD_GROUPS, SSD_STATE)
    c_in = c_in.reshape(B, T, SSD_GROUPS, SSD_STATE)
    dt = jax.nn.softplus(dt.astype(jnp.float32) + dt_bias).reshape(B, T, SSD_GROUPS, SSD_REP)
    a = -jnp.exp(a_log.astype(jnp.float32)).reshape(SSD_GROUPS, SSD_REP)
    init = ssm_state.reshape(B, SSD_GROUPS, SSD_REP, SSD_HEAD_DIM, SSD_STATE)
    y, final = ssd_scan(x, dt, a, b_in, c_in, init)
    y = y + d_skip.reshape(SSD_GROUPS, SSD_REP)[:, :, None] * x
    y = y.astype(h.dtype).reshape(B, T, SSD_D_INNER)
    y = rmsnorm(y * jax.nn.silu(z), norm_w)
    new_state = final.reshape(B, SSD_HEADS, SSD_HEAD_DIM, SSD_STATE).astype(ssm_state.dtype)
    return y @ w_out, new_conv, new_state


def setup_inputs(seed: int = 0) -> dict:
    key = jax.random.key(seed)
    ks = jax.random.split(key, 32)

    def nrm(i, shape, scale):
        return jax.random.normal(ks[i], shape, jnp.float32) * scale

    n_pages = PAST_LEN // PAGE_SIZE
    used = DEC_BATCH * n_pages
    n_pool = (5 * used + 3) // 4
    page_table = jax.random.permutation(ks[0], n_pool)[:used].reshape(DEC_BATCH, n_pages).astype(jnp.int32)
    win_keep = min(WINDOW, PAST_LEN)
    kv_row = (2, NSA_KV_HEADS, NSA_HEAD_DIM)
    dt0 = jnp.exp(jax.random.uniform(ks[1], (N_SSD, SSD_HEADS), jnp.float32, math.log(1e-3), math.log(1e-1)))
    return {
        'x_prompt': nrm(2, (BATCH, SEQ, D_MODEL), 1.0),
        'x_sample': nrm(3, (DEC_BATCH, DEC_SEQ, D_MODEL), 1.0),
        'cache_kv_cmp': nrm(4, (N_NSA, n_pool, PAGE_SIZE) + kv_row, 1.0),
        'cache_kv_sel': nrm(5, (N_NSA, n_pool, PAGE_SIZE) + kv_row, 1.0),
        'cache_kv_win': nrm(6, (N_NSA, DEC_BATCH, win_keep) + kv_row, 1.0),
        'state_ssm': nrm(7, (N_SSD, DEC_BATCH, SSD_HEADS, SSD_HEAD_DIM, SSD_STATE), 0.3),
        'state_conv': nrm(8, (N_SSD, DEC_BATCH, SSD_CONV_W - 1, SSD_CONV_DIM), 1.0),
        'page_table': page_table,
        'c_prompt': nrm(9, (BATCH, D_MODEL), 1.0),
        'c_sample': nrm(10, (DEC_BATCH, D_MODEL), 1.0),
        'ada_w': nrm(11, (DEPTH, D_MODEL, 6 * D_MODEL), 0.5 * D_MODEL ** -0.5),
        'ada_b': nrm(12, (DEPTH, 6 * D_MODEL), 0.02),
        'norm_w': 1.0 + nrm(13, (DEPTH, 2, D_MODEL), 0.05),
        'mlp_w1': nrm(14, (DEPTH, D_MODEL, D_FF), D_MODEL ** -0.5),
        'mlp_w2': nrm(15, (DEPTH, D_FF, D_MODEL), D_FF ** -0.5),
        'nsa_w_in': nrm(16, (N_NSA, D_MODEL, NSA_IN), D_MODEL ** -0.5),
        'nsa_w_out': nrm(17, (N_NSA, NSA_HEADS * NSA_HEAD_DIM, D_MODEL), (NSA_HEADS * NSA_HEAD_DIM) ** -0.5),
        'nsa_cmp_pe': nrm(18, (N_NSA, 2, CMP_BLOCK, NSA_HEAD_DIM), 0.1),
        'nsa_cmp_w1': nrm(19, (N_NSA, 2, CMP_BLOCK * NSA_HEAD_DIM, NSA_HEAD_DIM), (CMP_BLOCK * NSA_HEAD_DIM) ** -0.5),
        'nsa_cmp_w2': nrm(20, (N_NSA, 2, NSA_HEAD_DIM, NSA_HEAD_DIM), NSA_HEAD_DIM ** -0.5),
        'ssd_w_in': nrm(21, (N_SSD, D_MODEL, SSD_IN), D_MODEL ** -0.5),
        'ssd_conv_w': nrm(22, (N_SSD, SSD_CONV_W, SSD_CONV_DIM), SSD_CONV_W ** -0.5),
        'ssd_conv_b': nrm(23, (N_SSD, SSD_CONV_DIM), 0.02),
        'ssd_dt_bias': dt0 + jnp.log(-jnp.expm1(-dt0)),
        'ssd_a_log': jnp.log(jax.random.uniform(ks[24], (N_SSD, SSD_HEADS), jnp.float32, 1.0, 16.0)),
        'ssd_d': 1.0 + nrm(25, (N_SSD, SSD_HEADS), 0.1),
        'ssd_norm_w': 1.0 + nrm(26, (N_SSD, SSD_D_INNER), 0.05),
        'ssd_w_out': nrm(27, (N_SSD, SSD_D_INNER, D_MODEL), SSD_D_INNER ** -0.5),
        'final_norm_w': 1.0 + nrm(28, (D_MODEL,), 0.05),
    }


def reference(x_prompt, x_sample, cache_kv_cmp, cache_kv_sel, cache_kv_win, state_ssm, state_conv,
              page_table, c_prompt, c_sample, ada_w, ada_b, norm_w, mlp_w1, mlp_w2,
              nsa_w_in, nsa_w_out, nsa_cmp_pe, nsa_cmp_w1, nsa_cmp_w2,
              ssd_w_in, ssd_conv_w, ssd_conv_b, ssd_dt_bias, ssd_a_log, ssd_d, ssd_norm_w, ssd_w_out,
              final_norm_w):
    xp, xs = x_prompt, x_sample
    bp = x_prompt.shape[0]
    cmp_p, cmp_s, sel_p, sel_s, win_p, win_s = [], [], [], [], [], []
    ssm_p, ssm_s, conv_p, conv_s = [], [], [], []
    for i in range(DEPTH):
        j = i // N_MIXERS
        mp = adaln(c_prompt, ada_w[i], ada_b[i])
        ms = adaln(c_sample, ada_w[i], ada_b[i])
        hp = modulate(xp, norm_w[i, 0], mp[0], mp[1])
        hs = modulate(xs, norm_w[i, 0], ms[0], ms[1])
        if i % N_MIXERS == 0:
            op, kc_p, ks_p, kw_p = nsa_prompt(hp, nsa_w_in[j], nsa_w_out[j], nsa_cmp_pe[j],
                                              nsa_cmp_w1[j], nsa_cmp_w2[j])
            os_, kc_s, ks_s, kw_s = nsa_sample(hs, cache_kv_cmp[j], cache_kv_sel[j], cache_kv_win[j],
                                               page_table, nsa_w_in[j], nsa_w_out[j], nsa_cmp_pe[j],
                                               nsa_cmp_w1[j], nsa_cmp_w2[j])
            cmp_p.append(kc_p); cmp_s.append(kc_s)
            sel_p.append(ks_p); sel_s.append(ks_s)
            win_p.append(kw_p); win_s.append(kw_s)
        else:
            ssd_w = (ssd_w_in[j], ssd_conv_w[j], ssd_conv_b[j], ssd_dt_bias[j], ssd_a_log[j],
                     ssd_d[j], ssd_norm_w[j], ssd_w_out[j])
            zero_conv = jnp.zeros((bp, SSD_CONV_W - 1, SSD_CONV_DIM), hp.dtype)
            zero_ssm = jnp.zeros((bp, SSD_HEADS, SSD_HEAD_DIM, SSD_STATE), jnp.float32)
            op, cv_p, st_p = ssd_mixer(hp, zero_conv, zero_ssm, *ssd_w)
            os_, cv_s, st_s = ssd_mixer(hs, state_conv[j], state_ssm[j], *ssd_w)
            conv_p.append(cv_p); conv_s.append(cv_s)
            ssm_p.append(st_p); ssm_s.append(st_s)
        xp = xp + mp[2] * op
        xs = xs + ms[2] * os_
        hp = modulate(xp, norm_w[i, 1], mp[3], mp[4])
        hs = modulate(xs, norm_w[i, 1], ms[3], ms[4])
        xp = xp + mp[5] * sq_relu_mlp(hp, mlp_w1[i], mlp_w2[i])
        xs = xs + ms[5] * sq_relu_mlp(hs, mlp_w1[i], mlp_w2[i])
    y_prompt = rmsnorm(xp, final_norm_w)
    y_sample = rmsnorm(xs, final_norm_w)
    return (y_prompt, y_sample, jnp.stack(cmp_p), jnp.stack(cmp_s), jnp.stack(sel_p), jnp.stack(sel_s),
            jnp.stack(win_p), jnp.stack(win_s), jnp.stack(ssm_p), jnp.stack(ssm_s),
            jnp.stack(conv_p), jnp.stack(conv_s))
```

```python
import functools
import math

import numpy as np
import jax
import jax.numpy as jnp
from jax import lax
from jax.experimental import pallas as pl
from jax.experimental.pallas import tpu as pltpu

F32 = jnp.float32
BF16 = jnp.bfloat16

D_MODEL = 1024
DEPTH = 4
D_FF = 4 * D_MODEL
EPS = 1e-6
PAGE_SIZE = 128
NSA_HEADS = 16
HD = 64
KVH = 4
REP = 4
KVW = 2 * KVH * HD
GW = KVH * HD
CMP_BLOCK = 32
CMP_STRIDE = 16
SEL_BLOCK = 64
SEL_TOPK = 8
WINDOW = 512
SSD_D_INNER = 2 * D_MODEL
SSD_P = 64
SSD_HEADS = SSD_D_INNER // SSD_P
SSD_GROUPS = 4
SSD_REP = SSD_HEADS // SSD_GROUPS
SSD_N = 128
SSD_CONV_W = 4
SSD_CHUNK = 256
SSD_BC = 2 * SSD_GROUPS * SSD_N
SSD_CONV_DIM = SSD_D_INNER + SSD_BC
BIG = 1e30
NEG = -1e30
LANES = 128
NSA_N = 1024 + 3 * KVW + LANES
SSD_NP = SSD_D_INNER + SSD_CONV_DIM + 2 * LANES
VMEM_LIMIT = 48 * 1024 * 1024


def _cparams(sem):
    return pltpu.CompilerParams(dimension_semantics=sem, vmem_limit_bytes=VMEM_LIMIT)


def _dot(a, b):
    return jnp.dot(a, b, preferred_element_type=F32)


def _dot_nt(a, b):
    return lax.dot_general(a, b, (((1,), (1,)), ((), ())), preferred_element_type=F32)


def _split(x, n):
    parts, r = [], x
    for i in range(n):
        p = r.astype(BF16)
        parts.append(p)
        if i + 1 < n:
            r = r - p.astype(F32)
    return parts


def _dot_x01(x, m01, n=3):
    acc = None
    for p in _split(x, n):
        t = _dot(p, m01)
        acc = t if acc is None else acc + t
    return acc


def _dot_01x(m01, x, n=3):
    acc = None
    for p in _split(x, n):
        t = _dot(m01, p)
        acc = t if acc is None else acc + t
    return acc


def _silu(x):
    return x * (1.0 / (1.0 + jnp.exp(-x)))


def _sigmoid(x):
    return 1.0 / (1.0 + jnp.exp(-x))


def _softplus(x):
    return jnp.maximum(x, 0.0) + jnp.log(1.0 + jnp.exp(-jnp.abs(x)))


def _modulated_norm(x, nw, shift, scale):
    r = lax.rsqrt(jnp.mean(x * x, axis=-1, keepdims=True) + EPS)
    return (x * r) * nw * (1.0 + scale) + shift


def _adaln_kernel(c_ref, w_ref, b_ref, o_ref):
    a = _silu(c_ref[...]).astype(BF16)
    o_ref[0] = _dot(a, w_ref[0].astype(BF16)) + b_ref[0]


def _adaln(c_all, ada_w, ada_b):
    m, d = c_all.shape
    n = ada_w.shape[-1]
    tn = 1536
    return pl.pallas_call(
        _adaln_kernel,
        out_shape=jax.ShapeDtypeStruct((DEPTH, m, n), F32),
        grid=(DEPTH, n // tn),
        in_specs=[pl.BlockSpec((m, d), lambda l, j: (0, 0)),
                  pl.BlockSpec((1, d, tn), lambda l, j: (l, 0, j)),
                  pl.BlockSpec((1, 1, tn), lambda l, j: (l, 0, j))],
        out_specs=pl.BlockSpec((1, m, tn), lambda l, j: (l, 0, j)),
        compiler_params=_cparams(("parallel", "parallel")),
        name="adaln",
    )(c_all, ada_w, ada_b.reshape(DEPTH, 1, n))


def _modmm_kernel(x_ref, nw_ref, sh_ref, sc_ref, w_ref, o_ref, h_ref):
    @pl.when(pl.program_id(1) == 0)
    def _():
        h_ref[...] = _modulated_norm(x_ref[...], nw_ref[...], sh_ref[0], sc_ref[0]).astype(BF16)

    o_ref[...] = _dot(h_ref[...], w_ref[...])


def _mod_rows(tm, rows_per_mod):
    if rows_per_mod is None:
        return lambda shape: pl.BlockSpec((1,) + shape[1:], lambda i, j: (0, 0, 0))
    bpb = rows_per_mod // tm
    return lambda shape: pl.BlockSpec((1,) + shape[1:], lambda i, j: (i // bpb, 0, 0))


def _mod_matmul(x, nw, shift, scale, w, *, tm, tn, rows_per_mod):
    m, d = x.shape
    n = w.shape[1]
    spec = _mod_rows(tm, rows_per_mod)
    return pl.pallas_call(
        _modmm_kernel,
        out_shape=jax.ShapeDtypeStruct((m, n), F32),
        grid=(m // tm, n // tn),
        in_specs=[pl.BlockSpec((tm, d), lambda i, j: (i, 0)),
                  pl.BlockSpec((1, d), lambda i, j: (0, 0)),
                  spec(shift.shape), spec(scale.shape),
                  pl.BlockSpec((d, tn), lambda i, j: (0, j))],
        out_specs=pl.BlockSpec((tm, tn), lambda i, j: (i, j)),
        scratch_shapes=[pltpu.VMEM((tm, d), BF16)],
        compiler_params=_cparams(("parallel", "arbitrary")),
        name="mod_matmul",
    )(x, nw, shift, scale, w)


def _post_mlp_kernel(x_ref, a_ref, wo_ref, g1_ref, nw_ref, sh_ref, sc_ref, g2_ref, w1_ref, w2_ref, fnw_ref,
                     o_ref, x1_ref, h_ref, acc_ref, *, final_norm):
    k = pl.program_id(1)

    @pl.when(k == 0)
    def _():
        x1 = x_ref[...] + g1_ref[0] * _dot(a_ref[...], wo_ref[...])
        x1_ref[...] = x1
        h_ref[...] = _modulated_norm(x1, nw_ref[...], sh_ref[0], sc_ref[0]).astype(BF16)
        acc_ref[...] = jnp.zeros_like(acc_ref)

    u = jnp.maximum(_dot(h_ref[...], w1_ref[...]), 0.0)
    acc_ref[...] += _dot((u * u).astype(BF16), w2_ref[...])

    @pl.when(k == pl.num_programs(1) - 1)
    def _():
        y = x1_ref[...] + g2_ref[0] * acc_ref[...]
        if final_norm:
            r = lax.rsqrt(jnp.mean(y * y, axis=-1, keepdims=True) + EPS)
            y = (y * r) * fnw_ref[...]
        o_ref[...] = y


def _post_mlp(x, a, wo, g1, nw, shift, scale, g2, w1, w2, fnw, *, tm, tf, rows_per_mod, final_norm):
    m, d = x.shape
    ka = a.shape[1]
    ff = w1.shape[1]
    spec = _mod_rows(tm, rows_per_mod)
    return pl.pallas_call(
        functools.partial(_post_mlp_kernel, final_norm=final_norm),
        out_shape=jax.ShapeDtypeStruct((m, d), F32),
        grid=(m // tm, ff // tf),
        in_specs=[pl.BlockSpec((tm, d), lambda i, k: (i, 0)),
                  pl.BlockSpec((tm, ka), lambda i, k: (i, 0)),
                  pl.BlockSpec((ka, d), lambda i, k: (0, 0)),
                  spec(g1.shape),
                  pl.BlockSpec((1, d), lambda i, k: (0, 0)),
                  spec(shift.shape), spec(scale.shape), spec(g2.shape),
                  pl.BlockSpec((d, tf), lambda i, k: (0, k)),
                  pl.BlockSpec((tf, d), lambda i, k: (k, 0)),
                  pl.BlockSpec((1, d), lambda i, k: (0, 0))],
        out_specs=pl.BlockSpec((tm, d), lambda i, k: (i, 0)),
        scratch_shapes=[pltpu.VMEM((tm, d), F32), pltpu.VMEM((tm, d), BF16), pltpu.VMEM((tm, d), F32)],
        compiler_params=_cparams(("parallel", "arbitrary")),
        name="post_mlp",
    )(x, a, wo, g1, nw, shift, scale, g2, w1, w2, fnw)


def _head_expand_matrix():
    e = np.zeros((LANES, SSD_D_INNER), np.float32)
    for h in range(SSD_HEADS):
        e[h, h * SSD_P:(h + 1) * SSD_P] = 1.0
    return jnp.asarray(e, BF16)


def _ssd_prompt_kernel(z_ref, x_ref, bc_ref, dt_ref, cwx_ref, cbx_ref, cwb_ref, cbb_ref, dtb_ref, alog_ref,
                       dskip_ref, nw_ref, exp_ref, tri_ref,
                       y_ref, conv_ref, ssm_ref, xs_ref, bs_ref, st_ref):
    c = pl.program_id(1)
    L = SSD_CHUNK
    tail = SSD_CONV_W - 1

    @pl.when(c == 0)
    def _():
        xs_ref[0:8, :] = jnp.zeros((8, SSD_D_INNER), F32)
        bs_ref[0:8, :] = jnp.zeros((8, SSD_BC), F32)
        st_ref[...] = jnp.zeros_like(st_ref)

    xs_ref[8:8 + L, :] = x_ref[...]
    bs_ref[8:8 + L, :] = bc_ref[...]

    def conv(buf, w_ref, b_ref):
        acc = b_ref[...] + buf[8 - tail:8 - tail + L, :] * w_ref[0:1, :]
        for k in range(1, SSD_CONV_W):
            acc = acc + buf[8 - tail + k:8 - tail + k + L, :] * w_ref[k:k + 1, :]
        return _silu(acc)

    x = conv(xs_ref, cwx_ref, cbx_ref)
    bcv = conv(bs_ref, cwb_ref, cbb_ref)
    xs_ref[8 - tail:8, :] = xs_ref[8 + L - tail:8 + L, :]
    bs_ref[8 - tail:8, :] = bs_ref[8 + L - tail:8 + L, :]

    @pl.when(c == pl.num_programs(1) - 1)
    def _():
        conv_ref[0, :, 0:SSD_D_INNER] = x_ref[L - tail:L, :]
        conv_ref[0, :, SSD_D_INNER:SSD_CONV_DIM] = bc_ref[L - tail:L, :]

    lane = lax.broadcasted_iota(jnp.int32, (1, LANES), 1)
    head_ok = lane < SSD_HEADS
    dt = jnp.where(head_ok, _softplus(dt_ref[...] + dtb_ref[...]), 0.0)
    a = jnp.where(head_ok, -jnp.exp(alog_ref[...]), 0.0)
    acum = _dot_01x(tri_ref[...], dt * a)
    acum_t = acum.T
    a_last = acum[L - 1:L, :]
    expand = exp_ref[...]
    dt_e = _dot_x01(dt, expand)
    eac_e = _dot_x01(jnp.exp(acum), expand)
    dend_e = _dot_x01(jnp.exp(a_last - acum), expand)
    cdec_e = eac_e[L - 1:L, :]

    xdt = x * dt_e
    xdt_b = xdt.astype(BF16)
    xdtw_b = (xdt * dend_e).astype(BF16)
    row = lax.broadcasted_iota(jnp.int32, (L, L), 0)
    col = lax.broadcasted_iota(jnp.int32, (L, L), 1)
    causal = row >= col
    lane2 = lax.broadcasted_iota(jnp.int32, (L, 2 * SSD_P), 1)
    first_head = lane2 < SSD_P

    y_parts = []
    for g in range(SSD_GROUPS):
        b_g = bcv[:, g * SSD_N:(g + 1) * SSD_N]
        c_g = bcv[:, (SSD_GROUPS + g) * SSD_N:(SSD_GROUPS + g + 1) * SSD_N]
        b_gb = b_g.astype(BF16)
        c_gb = c_g.astype(BF16)
        cb = _dot_nt(c_gb, b_gb)
        gl = slice(g * SSD_REP * SSD_P, (g + 1) * SSD_REP * SSD_P)
        st_g = st_ref[:, gl]
        y_off = _dot(c_gb, st_g.astype(BF16)) * eac_e[:, gl]
        pair_out = []
        for j in range(SSD_REP // 2):
            pl_ = slice(g * SSD_REP * SSD_P + j * 2 * SSD_P, g * SSD_REP * SSD_P + (j + 1) * 2 * SSD_P)
            xp = xdt_b[:, pl_]
            ys = []
            for hh in range(2):
                h = g * SSD_REP + 2 * j + hh
                seg = acum[:, h:h + 1] - acum_t[h:h + 1, :]
                dec = jnp.exp(jnp.where(causal, seg, NEG))
                ys.append(_dot((cb * dec).astype(BF16), xp))
            pair_out.append(jnp.where(first_head, ys[0], ys[1]))
        y_g = jnp.concatenate(pair_out, axis=1) + y_off
        y_parts.append(y_g)
        st_ref[:, gl] = st_g * cdec_e[:, gl] + _dot(b_g.T.astype(BF16), xdtw_b[:, gl])

    y = jnp.concatenate(y_parts, axis=1) + dskip_ref[...] * x
    zg = z_ref[...]
    y = y * _silu(zg)
    r = lax.rsqrt(jnp.mean(y * y, axis=-1, keepdims=True) + EPS)
    y_ref[...] = ((y * r) * nw_ref[...]).astype(BF16)

    @pl.when(c == pl.num_programs(1) - 1)
    def _():
        ssm_ref[0] = st_ref[...].T


def _ssd_prompt(proj, bsz, t, cw, cb, dtb, alog, dskip, nw):
    L = SSD_CHUNK
    nc = t // L
    di = SSD_D_INNER
    tri = jnp.asarray(np.tril(np.ones((L, L), np.float32)), BF16)
    full = lambda shape: pl.BlockSpec(shape, lambda b, c: (0,) * len(shape))
    return pl.pallas_call(
        _ssd_prompt_kernel,
        out_shape=(jax.ShapeDtypeStruct((bsz * t, di), BF16),
                   jax.ShapeDtypeStruct((bsz, SSD_CONV_W - 1, SSD_CONV_DIM), F32),
                   jax.ShapeDtypeStruct((bsz, di, SSD_N), F32)),
        grid=(bsz, nc),
        in_specs=[pl.BlockSpec((L, di), lambda b, c: (b * nc + c, 0)),
                  pl.BlockSpec((L, di), lambda b, c: (b * nc + c, 1)),
                  pl.BlockSpec((L, SSD_BC), lambda b, c: (b * nc + c, 2 * di // SSD_BC)),
                  pl.BlockSpec((L, LANES), lambda b, c: (b * nc + c, (2 * di + SSD_BC) // LANES)),
                  full((SSD_CONV_W, di)), full((1, di)), full((SSD_CONV_W, SSD_BC)), full((1, SSD_BC)),
                  full((1, LANES)), full((1, LANES)), full((1, di)), full((1, di)),
                  full((LANES, di)), full((L, L))],
        out_specs=(pl.BlockSpec((L, di), lambda b, c: (b * nc + c, 0)),
                   pl.BlockSpec((1, SSD_CONV_W - 1, SSD_CONV_DIM), lambda b, c: (b, 0, 0)),
                   pl.BlockSpec((1, di, SSD_N), lambda b, c: (b, 0, 0))),
        scratch_shapes=[pltpu.VMEM((8 + L, di), F32), pltpu.VMEM((8 + L, SSD_BC), F32),
                        pltpu.VMEM((SSD_N, di), F32)],
        compiler_params=_cparams(("parallel", "arbitrary")),
        name="ssd_prompt",
    )(proj, proj, proj, proj, cw[:, :di], cb[:, :di], cw[:, di:], cb[:, di:], dtb, alog, dskip, nw,
      _head_expand_matrix(), tri)


def _ssd_sample_kernel(p_ref, cs_ref, st_ref, cw_ref, cb_ref, dtb_ref, alog_ref, dskip_ref, nw_ref, exp_ref,
                       y_ref, conv_ref, ssm_ref):
    di = SSD_D_INNER
    z = p_ref[0, :, 0:di]
    xbc = p_ref[0, :, di:di + SSD_CONV_DIM]
    dtr = p_ref[0, :, di + SSD_CONV_DIM:di + SSD_CONV_DIM + LANES]
    cs = cs_ref[0]
    acc = cb_ref[...] + xbc * cw_ref[SSD_CONV_W - 1:SSD_CONV_W, :]
    for k in range(SSD_CONV_W - 1):
        acc = acc + cs[k:k + 1, :] * cw_ref[k:k + 1, :]
    conv_ref[0, 0:SSD_CONV_W - 2, :] = cs[1:SSD_CONV_W - 1, :]
    conv_ref[0, SSD_CONV_W - 2:SSD_CONV_W - 1, :] = xbc
    act = _silu(acc)
    x = act[:, 0:di]
    lane = lax.broadcasted_iota(jnp.int32, (1, LANES), 1)
    head_ok = lane < SSD_HEADS
    dt = jnp.where(head_ok, _softplus(dtr + dtb_ref[...]), 0.0)
    a = jnp.where(head_ok, -jnp.exp(alog_ref[...]), 0.0)
    da = jnp.exp(dt * a)
    lhs = jnp.concatenate([dt, da, jnp.zeros((6, LANES), F32)], axis=0)
    ex = _dot_x01(lhs, exp_ref[...])
    dt_e, da_e = ex[0:1, :], ex[1:2, :]
    xdt = x * dt_e

    eye = (lax.broadcasted_iota(jnp.int32, (LANES, LANES), 0)
           == lax.broadcasted_iota(jnp.int32, (LANES, LANES), 1))

    def to_col(rowvec):
        return jnp.sum(jnp.where(eye, jnp.broadcast_to(rowvec, (LANES, LANES)), 0.0), axis=1, keepdims=True)

    def to_row(colvec):
        return jnp.sum(jnp.where(eye, jnp.broadcast_to(colvec, (LANES, LANES)), 0.0), axis=0, keepdims=True)

    y_off = []
    cbs = []
    for g in range(SSD_GROUPS):
        b_g = act[:, di + g * SSD_N:di + (g + 1) * SSD_N]
        c_g = act[:, di + (SSD_GROUPS + g) * SSD_N:di + (SSD_GROUPS + g + 1) * SSD_N]
        cbs.append(jnp.broadcast_to(jnp.sum(b_g * c_g, axis=1, keepdims=True), (1, SSD_REP * SSD_P)))
        for i in range(SSD_REP * SSD_P // LANES):
            lo = g * SSD_REP * SSD_P + i * LANES
            st = st_ref[0, lo:lo + LANES, :]
            xcol = to_col(xdt[:, lo:lo + LANES])
            dcol = to_col(da_e[:, lo:lo + LANES])
            ssm_ref[0, lo:lo + LANES, :] = st * dcol + xcol * b_g
            y_off.append(to_row(jnp.sum(st * c_g, axis=1, keepdims=True)))
    y = xdt * jnp.concatenate(cbs, axis=1) + jnp.concatenate(y_off, axis=1) * da_e + dskip_ref[...] * x
    y = y * _silu(z)
    r = lax.rsqrt(jnp.mean(y * y, axis=-1, keepdims=True) + EPS)
    y_ref[0] = ((y * r) * nw_ref[...]).astype(BF16)


def _ssd_sample(proj, conv_state, ssm_state, cw, cb, dtb, alog, dskip, nw):
    bsz = proj.shape[0]
    di = SSD_D_INNER
    full = lambda shape: pl.BlockSpec(shape, lambda b: (0,) * len(shape))
    per_b = lambda shape: pl.BlockSpec((1,) + shape, lambda b: (b,) + (0,) * len(shape))
    y, conv, ssm = pl.pallas_call(
        _ssd_sample_kernel,
        out_shape=(jax.ShapeDtypeStruct((bsz, 1, di), BF16),
                   jax.ShapeDtypeStruct((bsz, SSD_CONV_W - 1, SSD_CONV_DIM), F32),
                   jax.ShapeDtypeStruct((bsz, di, SSD_N), F32)),
        grid=(bsz,),
        in_specs=[per_b((1, SSD_NP)), per_b((SSD_CONV_W - 1, SSD_CONV_DIM)), per_b((di, SSD_N)),
                  full((SSD_CONV_W, SSD_CONV_DIM)), full((1, SSD_CONV_DIM)), full((1, LANES)), full((1, LANES)),
                  full((1, di)), full((1, di)), full((LANES, di))],
        out_specs=(per_b((1, di)), per_b((SSD_CONV_W - 1, SSD_CONV_DIM)), per_b((di, SSD_N))),
        compiler_params=_cparams(("parallel",)),
        name="ssd_sample",
    )(proj.reshape(bsz, 1, SSD_NP), conv_state, ssm_state.reshape(bsz, di, SSD_N), cw, cb, dtb, alog, dskip, nw,
      _head_expand_matrix())
    return y.reshape(bsz, di), conv, ssm


CH = CMP_STRIDE
N_L = CMP_BLOCK // 2
PE_ROWS = 16


def _compress_accumulate(lhs_fn, pe_ref, wab_ref, kv):
    acc = None
    for l in range(N_L):
        lhs = jnp.concatenate([lhs_fn(l * KVW + kv * GW), pe_ref[kv, l]], axis=0).astype(BF16)
        t = _dot(lhs, wab_ref[kv, l])
        acc = t if acc is None else acc + t
    return acc


def _compress_finish(acc, w2, nch):
    p = acc[0:nch, 0:GW]
    q_next = pltpu.roll(acc[0:nch, GW:2 * GW], nch - 1, axis=0)
    bias = acc[nch:nch + 1, 0:GW] + acc[nch + 8:nch + 9, GW:2 * GW]
    out = _dot(_silu(p + q_next + bias).astype(BF16), w2)
    row = lax.broadcasted_iota(jnp.int32, (nch, 1), 0)
    return jnp.where(row < nch - 1, out, 0.0)


def _softmax_rows(s, ok):
    s = jnp.where(ok, s, NEG)
    m = jnp.max(s, axis=1, keepdims=True)
    p = jnp.where(ok, jnp.exp(s - m), 0.0)
    d = jnp.sum(p, axis=1, keepdims=True)
    return p * jnp.where(d > 0.0, 1.0 / d, 0.0)


def _topk_mask(score, j, width, shifts):
    rank = jnp.zeros(score.shape, F32)
    for k in shifts:
        other = pltpu.roll(score, k, axis=1)
        lower = j >= k
        if width != LANES:
            other = jnp.where(lower, other, pltpu.roll(score, LANES - width + k, axis=1))
        rank = rank + jnp.where(lower, jnp.where(other >= score, 1.0, 0.0), jnp.where(other > score, 1.0, 0.0))
    return rank < SEL_TOPK


def _overlap(n_cmp, n_sel):
    c_start = np.arange(n_cmp) * CMP_STRIDE
    s_start = np.arange(n_sel) * SEL_BLOCK
    return ((c_start[:, None] < s_start[None, :] + SEL_BLOCK)
            & (c_start[:, None] + CMP_BLOCK > s_start[None, :])).astype(np.float32)


def _cmp_prompt_kernel(rows_ref, pe_ref, wab_ref, w2_ref, kc_ref, vc_ref):
    nch = rows_ref.shape[1]
    for kv, out in ((0, kc_ref), (1, vc_ref)):
        acc = _compress_accumulate(lambda lo: rows_ref[0, :, lo:lo + GW], pe_ref, wab_ref, kv)
        out[0] = _compress_finish(acc, w2_ref[kv], nch)


def _cmp_prompt(proj3, pe_t, wab, w2bd):
    bsz, t, _ = proj3.shape
    nch = t // CH
    rows = proj3[:, :, 1024:1024 + KVW].reshape(bsz, nch, CH * KVW)
    full = lambda shape: pl.BlockSpec(shape, lambda b: (0,) * len(shape))
    return pl.pallas_call(
        _cmp_prompt_kernel,
        out_shape=(jax.ShapeDtypeStruct((bsz, nch, GW), F32),) * 2,
        grid=(bsz,),
        in_specs=[pl.BlockSpec((1, nch, CH * KVW), lambda b: (b, 0, 0)),
                  full(pe_t.shape), full(wab.shape), full(w2bd.shape)],
        out_specs=(pl.BlockSpec((1, nch, GW), lambda b: (b, 0, 0)),) * 2,
        compiler_params=_cparams(("parallel",)),
        name="nsa_compress_prompt",
    )(rows, pe_t, wab, w2bd)


def _nsa_prompt_kernel(q_ref, gt_ref, kc_ref, vc_ref, ks_ref, vs_ref, kw_ref, vw_ref, ov_ref, gexp_ref, o_ref,
                       *, tq, n_sel):
    i = pl.program_id(1)
    t0 = i * tq
    trow = t0 + lax.broadcasted_iota(jnp.int32, (tq, 1), 0)
    lane_g = lax.broadcasted_iota(jnp.int32, (1, GW), 1) // HD

    def tile_heads(a):
        return jnp.concatenate([a] * REP, axis=0)

    qs = [q_ref[0, :, r * GW:(r + 1) * GW] * (HD ** -0.5) for r in range(REP)]
    qg = [jnp.concatenate([jnp.where(lane_g == g, qr, 0.0) for qr in qs], axis=0).astype(BF16)
          for g in range(KVH)]

    kc = kc_ref[0].astype(BF16)
    vc = vc_ref[0].astype(BF16)
    ncp = kc.shape[0]
    cend = lax.broadcasted_iota(jnp.int32, (1, ncp), 1) * CMP_STRIDE + (CMP_BLOCK - 1)
    ok_cmp = tile_heads(jnp.where(cend <= trow, 1.0, 0.0)) > 0.5
    o_cmp = jnp.zeros((REP * tq, GW), F32)
    imp = jnp.zeros((tq, LANES), F32)
    for g in range(KVH):
        p = _softmax_rows(_dot_nt(qg[g], kc), ok_cmp)
        o_cmp = o_cmp + jnp.where(lane_g == g, _dot(p.astype(BF16), vc), 0.0)
        psum = p[0:tq] + p[tq:2 * tq] + p[2 * tq:3 * tq] + p[3 * tq:4 * tq]
        imp = imp + _dot_x01(psum, ov_ref[g])

    jj = lax.broadcasted_iota(jnp.int32, (tq, LANES), 1) % n_sel
    cur = trow // SEL_BLOCK
    valid = jj * SEL_BLOCK <= trow
    forced = (jj == 0) | (jj == cur) | (jj == cur - 1)
    score = jnp.where(valid, jnp.where(forced, BIG, imp), NEG)
    top = _topk_mask(score, jj, n_sel, range(1, n_sel))
    chosen = jnp.where(top & valid, 1.0, 0.0).astype(BF16)

    def flash(qgb, k_ref, v_ref, lo, hi, ck, mask_fn):
        def body(c, carry):
            m, l, acc = carry
            k0 = pl.multiple_of(c * ck, ck)
            kb = k_ref[0, pl.ds(k0, ck), :].astype(BF16)
            vb = v_ref[0, pl.ds(k0, ck), :].astype(BF16)
            ok = tile_heads(mask_fn(k0)) > 0.5
            s = jnp.where(ok, _dot_nt(qgb, kb), NEG)
            m_new = jnp.maximum(m, jnp.max(s, axis=1, keepdims=True))
            alpha = jnp.exp(m - m_new)
            p = jnp.where(ok, jnp.exp(s - m_new), 0.0)
            l = alpha * l + jnp.sum(p, axis=1, keepdims=True)
            acc = alpha * acc + _dot(p.astype(BF16), vb)
            return m_new, l, acc

        init = (jnp.full((REP * tq, 1), NEG, F32), jnp.zeros((REP * tq, 1), F32), jnp.zeros((REP * tq, GW), F32))
        _, l, acc = lax.fori_loop(lo, hi, body, init)
        return acc * jnp.where(l > 0.0, 1.0 / l, 0.0)

    ck_sel = 4 * SEL_BLOCK
    ck_win = LANES
    o_sel = jnp.zeros((REP * tq, GW), F32)
    o_win = jnp.zeros((REP * tq, GW), F32)
    for g in range(KVH):
        def sel_mask(k0, g=g):
            kpos = k0 + lax.broadcasted_iota(jnp.int32, (1, ck_sel), 1)
            blk = (k0 + lax.broadcasted_iota(jnp.int32, (LANES, ck_sel), 1)) // SEL_BLOCK + g * n_sel
            expand = jnp.where(blk == lax.broadcasted_iota(jnp.int32, (LANES, ck_sel), 0), 1.0, 0.0).astype(BF16)
            picked = _dot(chosen, expand) > 0.5
            return jnp.where(picked & (kpos <= trow), 1.0, 0.0)

        def win_mask(k0):
            d = trow - (k0 + lax.broadcasted_iota(jnp.int32, (1, ck_win), 1))
            return jnp.where((d >= 0) & (d <= WINDOW), 1.0, 0.0)

        o = flash(qg[g], ks_ref, vs_ref, 0, (t0 + tq + ck_sel - 1) // ck_sel, ck_sel, sel_mask)
        o_sel = o_sel + jnp.where(lane_g == g, o, 0.0)
        o = flash(qg[g], kw_ref, vw_ref, jnp.maximum(t0 - WINDOW, 0) // ck_win, (t0 + tq) // ck_win, ck_win, win_mask)
        o_win = o_win + jnp.where(lane_g == g, o, 0.0)

    gates = _sigmoid(gt_ref[0])
    out = None
    for br, o in enumerate((o_cmp, o_sel, o_win)):
        o_full = jnp.concatenate([o[r * tq:(r + 1) * tq] for r in range(REP)], axis=1)
        term = _dot_x01(gates, gexp_ref[br]) * o_full
        out = term if out is None else out + term
    o_ref[0] = out.astype(BF16)


def _gate_expand_matrix():
    e = np.zeros((3, LANES, NSA_HEADS * HD), np.float32)
    for g in range(KVH):
        for r in range(REP):
            for br in range(3):
                e[br, (g * REP + r) * 3 + br, r * GW + g * HD:r * GW + (g + 1) * HD] = 1.0
    return jnp.asarray(e, BF16)


def _nsa_prompt_attention(proj3, kc, vc):
    bsz, t, _ = proj3.shape
    tq = 128
    n_sel = t // SEL_BLOCK
    ncp = t // CH
    assert KVH * n_sel == LANES and ncp == LANES and tq == LANES
    ov = np.zeros((KVH, ncp, LANES), np.float32)
    for g in range(KVH):
        ov[g, :ncp - 1, g * n_sel:(g + 1) * n_sel] = _overlap(ncp - 1, n_sel)
    whole = lambda w, col: pl.BlockSpec((1, t, w), lambda b, i: (b, 0, col))
    full = lambda shape: pl.BlockSpec(shape, lambda b, i: (0,) * len(shape))
    out = pl.pallas_call(
        functools.partial(_nsa_prompt_kernel, tq=tq, n_sel=n_sel),
        out_shape=jax.ShapeDtypeStruct((bsz, t, NSA_HEADS * HD), BF16),
        grid=(bsz, t // tq),
        in_specs=[pl.BlockSpec((1, tq, NSA_HEADS * HD), lambda b, i: (b, i, 0)),
                  pl.BlockSpec((1, tq, LANES), lambda b, i: (b, i, (1024 + 3 * KVW) // LANES)),
                  pl.BlockSpec((1, ncp, GW), lambda b, i: (b, 0, 0)),
                  pl.BlockSpec((1, ncp, GW), lambda b, i: (b, 0, 0)),
                  whole(GW, (1024 + KVW) // GW), whole(GW, (1024 + KVW) // GW + 1),
                  whole(GW, (1024 + 2 * KVW) // GW), whole(GW, (1024 + 2 * KVW) // GW + 1),
                  full((KVH, ncp, LANES)), full((3, LANES, NSA_HEADS * HD))],
        out_specs=pl.BlockSpec((1, tq, NSA_HEADS * HD), lambda b, i: (b, i, 0)),
        compiler_params=_cparams(("parallel", "arbitrary")),
        name="nsa_prompt_attention",
    )(proj3, proj3, kc, vc, proj3, proj3, proj3, proj3, jnp.asarray(ov, BF16), _gate_expand_matrix())
    return out.reshape(bsz * t, NSA_HEADS * HD)


def _nsa_sample_kernel(pt_ref, p_ref, *rest, n_pages, past):
    del pt_ref
    cmp_pages = rest[:n_pages]
    sel_pages = rest[n_pages:2 * n_pages]
    win_ref, pe_ref, wab_ref, w2_ref, ov_ref, o_ref, wout_ref = rest[2 * n_pages:]
    nch = n_pages * (PAGE_SIZE // CH)
    t = past
    keep = win_ref.shape[1]
    rows_q = NSA_HEADS
    lane_g = lax.broadcasted_iota(jnp.int32, (1, GW), 1) // HD

    cmp_kv = []
    for kv in range(2):
        acc = _compress_accumulate(
            lambda lo: jnp.concatenate([pg[0, :, lo:lo + GW] for pg in cmp_pages], axis=0), pe_ref, wab_ref, kv)
        cmp_kv.append(_compress_finish(acc, w2_ref[kv], nch).astype(BF16))
    kc, vc = cmp_kv

    q16 = jnp.concatenate(
        [jnp.where(lane_g == g, p_ref[0, :, r * GW:(r + 1) * GW], 0.0) for g in range(KVH) for r in range(REP)],
        axis=0) * (HD ** -0.5)
    q16b = q16.astype(BF16)

    cend = lax.broadcasted_iota(jnp.int32, (1, nch), 1) * CMP_STRIDE + (CMP_BLOCK - 1)
    p = _softmax_rows(_dot_nt(q16b, kc), cend <= t)
    o_cmp = _dot(p.astype(BF16), vc)
    psum = jnp.concatenate(
        [jnp.broadcast_to(jnp.sum(p[g * REP:(g + 1) * REP], axis=0, keepdims=True), (REP, nch)) for g in range(KVH)],
        axis=0)
    imp = _dot_x01(psum, ov_ref[...])

    n_sel = -(-(t + 1) // SEL_BLOCK)
    nj = max(n_sel, SEL_TOPK)
    j = lax.broadcasted_iota(jnp.int32, (rows_q, LANES), 1)
    cur = t // SEL_BLOCK
    valid = (j < n_sel) & (j * SEL_BLOCK <= t)
    forced = (j == 0) | (j == cur) | (j == cur - 1)
    score = jnp.where(valid, jnp.where(forced, BIG, imp), NEG)
    score = jnp.where(j < nj, score, -3e38)
    shifts = list(range(1, nj)) + list(range(LANES - nj + 1, LANES))
    chosen = jnp.where(_topk_mask(score, j, LANES, shifts) & valid, 1.0, 0.0)
    chosen_b = chosen.astype(BF16)

    def attend(s_parts, ok_parts, v_parts, s_new, ok_new, v_new):
        s = jnp.where(jnp.concatenate(ok_parts, axis=1), jnp.concatenate(s_parts, axis=1), NEG)
        ok = jnp.concatenate(ok_parts, axis=1)
        s_new = jnp.where(ok_new, s_new, NEG)
        m = jnp.maximum(jnp.max(s, axis=1, keepdims=True), s_new)
        pr = jnp.where(ok, jnp.exp(s - m), 0.0)
        pr_new = jnp.where(ok_new, jnp.exp(s_new - m), 0.0)
        l = jnp.sum(pr, axis=1, keepdims=True) + pr_new
        acc = pr_new * v_new
        off = 0
        for v in v_parts:
            acc = acc + _dot(pr[:, off:off + v.shape[0]].astype(BF16), v)
            off += v.shape[0]
        return acc * jnp.where(l > 0.0, 1.0 / l, 0.0)

    s_parts, ok_parts, v_parts = [], [], []
    bpp = PAGE_SIZE // SEL_BLOCK
    for pg_i, pg in enumerate(sel_pages):
        s_parts.append(_dot_nt(q16b, pg[0, :, 0:GW].astype(BF16)))
        v_parts.append(pg[0, :, GW:2 * GW].astype(BF16))
        blk = pg_i * bpp + lax.broadcasted_iota(jnp.int32, (LANES, PAGE_SIZE), 1) // SEL_BLOCK
        expand = jnp.where(blk == lax.broadcasted_iota(jnp.int32, (LANES, PAGE_SIZE), 0), 1.0, 0.0).astype(BF16)
        ok_parts.append(_dot(chosen_b, expand) > 0.5)
    k_new = p_ref[0, :, 1024 + KVW:1024 + KVW + GW]
    v_new = p_ref[0, :, 1024 + KVW + GW:1024 + 2 * KVW]
    s_new = jnp.sum(q16 * k_new, axis=1, keepdims=True)
    o_sel = attend(s_parts, ok_parts, v_parts, s_new, chosen[:, cur:cur + 1] > 0.5, v_new)

    d = keep - lax.broadcasted_iota(jnp.int32, (1, keep), 1)
    k_new = p_ref[0, :, 1024 + 2 * KVW:1024 + 2 * KVW + GW]
    v_new = p_ref[0, :, 1024 + 2 * KVW + GW:1024 + 3 * KVW]
    s_new = jnp.sum(q16 * k_new, axis=1, keepdims=True)
    o_win = attend([_dot_nt(q16b, win_ref[0, :, 0:GW].astype(BF16))],
                   [jnp.broadcast_to((d <= WINDOW) & (t - d >= 0), (rows_q, keep))],
                   [win_ref[0, :, GW:2 * GW].astype(BF16)], s_new, jnp.full((rows_q, 1), True), v_new)
    wout_ref[0, 0:keep - 1, :] = win_ref[0, 1:keep, :]
    wout_ref[0, keep - 1:keep, :] = p_ref[0, :, 1024 + 2 * KVW:1024 + 3 * KVW]

    gates = jnp.broadcast_to(_sigmoid(p_ref[0, :, 1024 + 3 * KVW:NSA_N]), (rows_q, LANES))
    lane = lax.broadcasted_iota(jnp.int32, (rows_q, LANES), 1)
    rowi = lax.broadcasted_iota(jnp.int32, (rows_q, LANES), 0)
    o16 = None
    for br, o in enumerate((o_cmp, o_sel, o_win)):
        gcol = jnp.sum(jnp.where(lane == rowi * 3 + br, gates, 0.0), axis=1, keepdims=True)
        o16 = gcol * o if o16 is None else o16 + gcol * o
    outs = []
    for r in range(REP):
        acc = None
        for g in range(KVH):
            term = jnp.where(lane_g == g, o16[g * REP + r:g * REP + r + 1, :], 0.0)
            acc = term if acc is None else acc + term
        outs.append(acc)
    o_ref[0] = jnp.concatenate(outs, axis=1).astype(BF16)


def _nsa_sample(proj, cache_c, cache_s, cache_w, page_table, pe_t, wab, w2bd):
    bsz, n_pages = page_table.shape
    past = n_pages * PAGE_SIZE
    n_pool = cache_c.shape[0]
    keep = cache_w.shape[1]
    nch = past // CH
    assert nch == LANES and keep == WINDOW and past % SEL_BLOCK == 0
    n_sel = -(-(past + 1) // SEL_BLOCK)
    ov = np.zeros((nch, LANES), np.float32)
    ov[:nch - 1, :n_sel] = _overlap(nch - 1, n_sel)
    full = lambda shape: pl.BlockSpec(shape, lambda b, pt: (0,) * len(shape))
    per_b = lambda shape: pl.BlockSpec((1,) + shape, lambda b, pt: (b,) + (0,) * len(shape))
    page = lambda shape, p: pl.BlockSpec((1,) + shape, lambda b, pt: (pt[b * n_pages + p], 0, 0))
    grid_spec = pltpu.PrefetchScalarGridSpec(
        num_scalar_prefetch=1,
        grid=(bsz,),
        in_specs=([per_b((1, NSA_N))]
                  + [page((PAGE_SIZE // CH, CH * KVW), p) for p in range(n_pages)]
                  + [page((PAGE_SIZE, KVW), p) for p in range(n_pages)]
                  + [per_b((keep, KVW)), full(pe_t.shape), full(wab.shape), full(w2bd.shape), full((nch, LANES))]),
        out_specs=(per_b((1, NSA_HEADS * HD)), per_b((keep, KVW))),
    )
    cmp_view = cache_c.reshape(n_pool, PAGE_SIZE // CH, CH * KVW)
    sel_view = cache_s.reshape(n_pool, PAGE_SIZE, KVW)
    o, wout = pl.pallas_call(
        functools.partial(_nsa_sample_kernel, n_pages=n_pages, past=past),
        out_shape=(jax.ShapeDtypeStruct((bsz, 1, NSA_HEADS * HD), BF16),
                   jax.ShapeDtypeStruct((bsz, keep, KVW), F32)),
        grid_spec=grid_spec,
        compiler_params=_cparams(("arbitrary",)),
        name="nsa_sample",
    )(page_table.reshape(-1), proj.reshape(bsz, 1, NSA_N), *([cmp_view] * n_pages), *([sel_view] * n_pages),
      cache_w.reshape(bsz, keep, KVW), pe_t, wab, w2bd, jnp.asarray(ov, BF16))
    return o.reshape(bsz, NSA_HEADS * HD), wout


def _head_perm():
    idx = np.zeros(NSA_HEADS * HD, np.int32)
    for g in range(KVH):
        for r in range(REP):
            idx[r * GW + g * HD:r * GW + (g + 1) * HD] = (g * REP + r) * HD + np.arange(HD)
    return idx


def _prep_nsa(w_in, w_out, pe, w1, w2):
    perm = _head_perm()
    n_gate = 3 * NSA_HEADS
    w = jnp.concatenate([w_in[:, :1024][:, perm], w_in[:, 1024:],
                         jnp.zeros((D_MODEL, LANES - n_gate), F32)], axis=1).astype(BF16)
    wo = w_out[perm, :].astype(BF16)
    eye = jnp.eye(KVH, dtype=F32)
    w1r = w1.reshape(2, 2, N_L, HD, HD)
    wab = jnp.einsum('khlde,gf->klgdhfe', w1r, eye).reshape(2, N_L, GW, 2 * GW).astype(BF16)
    w2bd = jnp.einsum('kde,gf->kgdfe', w2, eye).reshape(2, GW, GW).astype(BF16)
    per = pe.reshape(2, 2, N_L, 1, 1, HD)
    pe_t = jnp.broadcast_to(per, (2, 2, N_L, 8, KVH, HD)).transpose(0, 2, 1, 3, 4, 5).reshape(2, N_L, PE_ROWS, GW)
    return w, wo, pe_t, wab, w2bd


def _prep_ssd(w_in, dt_bias, a_log, d_skip):
    pad = SSD_NP - w_in.shape[1]
    w = jnp.concatenate([w_in, jnp.zeros((D_MODEL, pad), F32)], axis=1).astype(BF16)
    pad_h = lambda v: jnp.concatenate([v, jnp.zeros((LANES - SSD_HEADS,), F32)]).reshape(1, LANES)
    return w, pad_h(dt_bias), pad_h(a_log), jnp.repeat(d_skip, SSD_P).reshape(1, SSD_D_INNER)


def kernel(x_prompt, x_sample, cache_kv_cmp, cache_kv_sel, cache_kv_win, state_ssm, state_conv, page_table, c_prompt, c_sample, ada_w, ada_b, norm_w, mlp_w1, mlp_w2, nsa_w_in, nsa_w_out, nsa_cmp_pe, nsa_cmp_w1, nsa_cmp_w2, ssd_w_in, ssd_conv_w, ssd_conv_b, ssd_dt_bias, ssd_a_log, ssd_d, ssd_norm_w, ssd_w_out, final_norm_w):
    bp, t, d = x_prompt.shape
    bs = x_sample.shape[0]
    xp = x_prompt.reshape(bp * t, d)
    xs = x_sample.reshape(bs, d)
    mods = _adaln(jnp.concatenate([c_prompt, c_sample], axis=0), ada_w, ada_b)
    w1b = mlp_w1.astype(BF16)
    w2b = mlp_w2.astype(BF16)
    fnw = final_norm_w.reshape(1, d)
    tm = 512
    kv_out = {k: [] for k in ("cmp_p", "cmp_s", "sel_p", "sel_s", "win_p", "win_s", "ssm_p", "ssm_s", "conv_p", "conv_s")}
    kv_rows = lambda a, n: a.reshape(n + (2, KVH, HD))

    for i in range(DEPTH):
        jl = i // 2
        mp = [mods[i, :bp, k * d:(k + 1) * d].reshape(bp, 1, d) for k in range(6)]
        ms = [mods[i, bp:, k * d:(k + 1) * d].reshape(1, bs, d) for k in range(6)]
        nw0 = norm_w[i, 0].reshape(1, d)
        nw1 = norm_w[i, 1].reshape(1, d)
        if i % 2 == 0:
            w, wo, pe_t, wab, w2bd = _prep_nsa(nsa_w_in[jl], nsa_w_out[jl], nsa_cmp_pe[jl], nsa_cmp_w1[jl], nsa_cmp_w2[jl])
            pp = _mod_matmul(xp, nw0, mp[0], mp[1], w, tm=tm, tn=896, rows_per_mod=t)
            ps = _mod_matmul(xs, nw0, ms[0], ms[1], w, tm=bs, tn=896, rows_per_mod=None)
            pp3 = pp.reshape(bp, t, NSA_N)
            kc, vc = _cmp_prompt(pp3, pe_t, wab, w2bd)
            ap = _nsa_prompt_attention(pp3, kc, vc)
            as_, wout = _nsa_sample(ps, cache_kv_cmp[jl], cache_kv_sel[jl], cache_kv_win[jl], page_table, pe_t, wab, w2bd)
            keep_p = min(WINDOW, t)
            kv_out["cmp_p"].append(kv_rows(pp3[:, :, 1024:1024 + KVW], (bp, t)))
            kv_out["sel_p"].append(kv_rows(pp3[:, :, 1024 + KVW:1024 + 2 * KVW], (bp, t)))
            kv_out["win_p"].append(kv_rows(pp3[:, t - keep_p:, 1024 + 2 * KVW:1024 + 3 * KVW], (bp, keep_p)))
            kv_out["cmp_s"].append(kv_rows(ps[:, 1024:1024 + KVW], (bs, 1)))
            kv_out["sel_s"].append(kv_rows(ps[:, 1024 + KVW:1024 + 2 * KVW], (bs, 1)))
            kv_out["win_s"].append(kv_rows(wout, (bs, wout.shape[1])))
        else:
            w, dtb, alog, dsk = _prep_ssd(ssd_w_in[jl], ssd_dt_bias[jl], ssd_a_log[jl], ssd_d[jl])
            wo = ssd_w_out[jl].astype(BF16)
            snw = ssd_norm_w[jl].reshape(1, SSD_D_INNER)
            cw = ssd_conv_w[jl]
            cb = ssd_conv_b[jl].reshape(1, SSD_CONV_DIM)
            pp = _mod_matmul(xp, nw0, mp[0], mp[1], w, tm=tm, tn=896, rows_per_mod=t)
            ps = _mod_matmul(xs, nw0, ms[0], ms[1], w, tm=bs, tn=896, rows_per_mod=None)
            ap, conv_p, ssm_p = _ssd_prompt(pp, bp, t, cw, cb, dtb, alog, dsk, snw)
            as_, conv_s, ssm_s = _ssd_sample(ps, state_conv[jl], state_ssm[jl], cw, cb, dtb, alog, dsk, snw)
            kv_out["conv_p"].append(conv_p)
            kv_out["conv_s"].append(conv_s)
            kv_out["ssm_p"].append(ssm_p.reshape(bp, SSD_HEADS, SSD_P, SSD_N))
            kv_out["ssm_s"].append(ssm_s.reshape(bs, SSD_HEADS, SSD_P, SSD_N))
        last = i == DEPTH - 1
        xp = _post_mlp(xp, ap, wo, mp[2], nw1, mp[3], mp[4], mp[5], w1b[i], w2b[i], fnw,
                       tm=tm, tf=512, rows_per_mod=t, final_norm=last)
        xs = _post_mlp(xs, as_, wo, ms[2], nw1, ms[3], ms[4], ms[5], w1b[i], w2b[i], fnw,
                       tm=bs, tf=512, rows_per_mod=None, final_norm=last)

    st = lambda k: jnp.stack(kv_out[k])
    return (xp.reshape(bp, t, d), xs.reshape(bs, 1, d), st("cmp_p"), st("cmp_s"), st("sel_p"), st("sel_s"),
            st("win_p"), st("win_s"), st("ssm_p"), st("ssm_s"), st("conv_p"), st("conv_s"))
```

```python
import functools
import math

import numpy as np
import jax
import jax.numpy as jnp
from jax import lax
from jax.experimental import pallas as pl
from jax.experimental.pallas import tpu as pltpu

F32 = jnp.float32
BF16 = jnp.bfloat16

D_MODEL = 1024
DEPTH = 4
D_FF = 4 * D_MODEL
EPS = 1e-6
PAGE_SIZE = 128
NSA_HEADS = 16
HD = 64
KVH = 4
REP = 4
KVW = 2 * KVH * HD
GW = KVH * HD
CMP_BLOCK = 32
CMP_STRIDE = 16
SEL_BLOCK = 64
SEL_TOPK = 8
WINDOW = 512
SSD_D_INNER = 2 * D_MODEL
SSD_P = 64
SSD_HEADS = SSD_D_INNER // SSD_P
SSD_GROUPS = 4
SSD_REP = SSD_HEADS // SSD_GROUPS
SSD_N = 128
SSD_CONV_W = 4
SSD_CHUNK = 256
SSD_BC = 2 * SSD_GROUPS * SSD_N
SSD_CONV_DIM = SSD_D_INNER + SSD_BC
BIG = 1e30
NEG = -1e30
NEG_MXU = -2.0 ** 100
LANES = 128
NSA_N = 1024 + 3 * KVW + LANES
KPAD = KVH * LANES
SSD_NP = SSD_D_INNER + SSD_CONV_DIM + 2 * LANES
VMEM_LIMIT = 48 * 1024 * 1024


def _cparams(sem):
    return pltpu.CompilerParams(dimension_semantics=sem, vmem_limit_bytes=VMEM_LIMIT)


def _dot(a, b):
    return jnp.dot(a, b, preferred_element_type=F32)


def _dot_nt(a, b):
    return lax.dot_general(a, b, (((1,), (1,)), ((), ())), preferred_element_type=F32)


def _split(x, n):
    parts, r = [], x
    for i in range(n):
        p = r.astype(BF16)
        parts.append(p)
        if i + 1 < n:
            r = r - p.astype(F32)
    return parts


def _dot_x01(x, m01, n=3):
    acc = None
    for p in _split(x, n):
        t = _dot(p, m01)
        acc = t if acc is None else acc + t
    return acc


def _dot_01x(m01, x, n=3):
    acc = None
    for p in _split(x, n):
        t = _dot(m01, p)
        acc = t if acc is None else acc + t
    return acc


def _silu(x):
    return x * (1.0 / (1.0 + jnp.exp(-x)))


def _sigmoid(x):
    return 1.0 / (1.0 + jnp.exp(-x))


def _softplus(x):
    return jnp.maximum(x, 0.0) + jnp.log(1.0 + jnp.exp(-jnp.abs(x)))


def _modulated_norm(x, nw, shift, scale):
    r = lax.rsqrt(jnp.mean(x * x, axis=-1, keepdims=True) + EPS)
    return (x * r) * nw * (1.0 + scale) + shift


def _adaln_kernel(c_ref, w_ref, b_ref, o_ref):
    a = _silu(c_ref[...]).astype(BF16)
    o_ref[0] = _dot(a, w_ref[0].astype(BF16)) + b_ref[0]


def _adaln(c_all, ada_w, ada_b):
    m, d = c_all.shape
    n = ada_w.shape[-1]
    tn = 1536
    return pl.pallas_call(
        _adaln_kernel,
        out_shape=jax.ShapeDtypeStruct((DEPTH, m, n), F32),
        grid=(DEPTH, n // tn),
        in_specs=[pl.BlockSpec((m, d), lambda l, j: (0, 0)),
                  pl.BlockSpec((1, d, tn), lambda l, j: (l, 0, j)),
                  pl.BlockSpec((1, 1, tn), lambda l, j: (l, 0, j))],
        out_specs=pl.BlockSpec((1, m, tn), lambda l, j: (l, 0, j)),
        compiler_params=_cparams(("parallel", "parallel")),
        name="adaln",
    )(c_all, ada_w, ada_b.reshape(DEPTH, 1, n))


def _modmm_kernel(x_ref, nw_ref, sh_ref, sc_ref, w_ref, o_ref, h_ref):
    @pl.when(pl.program_id(1) == 0)
    def _():
        h_ref[...] = _modulated_norm(x_ref[...], nw_ref[...], sh_ref[0], sc_ref[0]).astype(BF16)

    o_ref[...] = _dot(h_ref[...], w_ref[...])


def _mod_rows(tm, rows_per_mod):
    if rows_per_mod is None:
        return lambda shape: pl.BlockSpec((1,) + shape[1:], lambda i, j: (0, 0, 0))
    bpb = rows_per_mod // tm
    return lambda shape: pl.BlockSpec((1,) + shape[1:], lambda i, j: (i // bpb, 0, 0))


def _mod_matmul(x, nw, shift, scale, w, *, tm, tn, rows_per_mod):
    m, d = x.shape
    n = w.shape[1]
    spec = _mod_rows(tm, rows_per_mod)
    return pl.pallas_call(
        _modmm_kernel,
        out_shape=jax.ShapeDtypeStruct((m, n), F32),
        grid=(m // tm, n // tn),
        in_specs=[pl.BlockSpec((tm, d), lambda i, j: (i, 0)),
                  pl.BlockSpec((1, d), lambda i, j: (0, 0)),
                  spec(shift.shape), spec(scale.shape),
                  pl.BlockSpec((d, tn), lambda i, j: (0, j))],
        out_specs=pl.BlockSpec((tm, tn), lambda i, j: (i, j)),
        scratch_shapes=[pltpu.VMEM((tm, d), BF16)],
        compiler_params=_cparams(("parallel", "arbitrary")),
        name="mod_matmul",
    )(x, nw, shift, scale, w)


def _post_mlp_kernel(x_ref, a_ref, wo_ref, g1_ref, nw_ref, sh_ref, sc_ref, g2_ref, w1_ref, w2_ref, fnw_ref,
                     o_ref, x1_ref, h_ref, acc_ref, *, final_norm):
    k = pl.program_id(1)

    @pl.when(k == 0)
    def _():
        x1 = x_ref[...] + g1_ref[0] * _dot(a_ref[...], wo_ref[...])
        x1_ref[...] = x1
        h_ref[...] = _modulated_norm(x1, nw_ref[...], sh_ref[0], sc_ref[0]).astype(BF16)
        acc_ref[...] = jnp.zeros_like(acc_ref)

    u = jnp.maximum(_dot(h_ref[...], w1_ref[...]), 0.0)
    acc_ref[...] += _dot((u * u).astype(BF16), w2_ref[...])

    @pl.when(k == pl.num_programs(1) - 1)
    def _():
        y = x1_ref[...] + g2_ref[0] * acc_ref[...]
        if final_norm:
            r = lax.rsqrt(jnp.mean(y * y, axis=-1, keepdims=True) + EPS)
            y = (y * r) * fnw_ref[...]
        o_ref[...] = y


def _post_mlp(x, a, wo, g1, nw, shift, scale, g2, w1, w2, fnw, *, tm, tf, rows_per_mod, final_norm):
    m, d = x.shape
    ka = a.shape[1]
    ff = w1.shape[1]
    spec = _mod_rows(tm, rows_per_mod)
    return pl.pallas_call(
        functools.partial(_post_mlp_kernel, final_norm=final_norm),
        out_shape=jax.ShapeDtypeStruct((m, d), F32),
        grid=(m // tm, ff // tf),
        in_specs=[pl.BlockSpec((tm, d), lambda i, k: (i, 0)),
                  pl.BlockSpec((tm, ka), lambda i, k: (i, 0)),
                  pl.BlockSpec((ka, d), lambda i, k: (0, 0)),
                  spec(g1.shape),
                  pl.BlockSpec((1, d), lambda i, k: (0, 0)),
                  spec(shift.shape), spec(scale.shape), spec(g2.shape),
                  pl.BlockSpec((d, tf), lambda i, k: (0, k)),
                  pl.BlockSpec((tf, d), lambda i, k: (k, 0)),
                  pl.BlockSpec((1, d), lambda i, k: (0, 0))],
        out_specs=pl.BlockSpec((tm, d), lambda i, k: (i, 0)),
        scratch_shapes=[pltpu.VMEM((tm, d), F32), pltpu.VMEM((tm, d), BF16), pltpu.VMEM((tm, d), F32)],
        compiler_params=_cparams(("parallel", "arbitrary")),
        name="post_mlp",
    )(x, a, wo, g1, nw, shift, scale, g2, w1, w2, fnw)


def _head_expand_matrix():
    e = np.zeros((LANES, SSD_D_INNER), np.float32)
    for h in range(SSD_HEADS):
        e[h, h * SSD_P:(h + 1) * SSD_P] = 1.0
    return jnp.asarray(e, BF16)


def _ssd_prompt_kernel(z_ref, x_ref, bc_ref, dt_ref, cwx_ref, cbx_ref, cwb_ref, cbb_ref, dtb_ref, alog_ref,
                       dskip_ref, nw_ref, exp_ref, tri_ref,
                       y_ref, conv_ref, ssm_ref, xs_ref, bs_ref, st_ref):
    c = pl.program_id(1)
    L = SSD_CHUNK
    tail = SSD_CONV_W - 1

    @pl.when(c == 0)
    def _():
        xs_ref[0:8, :] = jnp.zeros((8, SSD_D_INNER), F32)
        bs_ref[0:8, :] = jnp.zeros((8, SSD_BC), F32)
        st_ref[...] = jnp.zeros_like(st_ref)

    xs_ref[8:8 + L, :] = x_ref[...]
    bs_ref[8:8 + L, :] = bc_ref[...]

    def conv(buf, w_ref, b_ref):
        acc = b_ref[...] + buf[8 - tail:8 - tail + L, :] * w_ref[0:1, :]
        for k in range(1, SSD_CONV_W):
            acc = acc + buf[8 - tail + k:8 - tail + k + L, :] * w_ref[k:k + 1, :]
        return _silu(acc)

    x = conv(xs_ref, cwx_ref, cbx_ref)
    bcv = conv(bs_ref, cwb_ref, cbb_ref)
    xs_ref[8 - tail:8, :] = xs_ref[8 + L - tail:8 + L, :]
    bs_ref[8 - tail:8, :] = bs_ref[8 + L - tail:8 + L, :]

    @pl.when(c == pl.num_programs(1) - 1)
    def _():
        conv_ref[0, :, 0:SSD_D_INNER] = x_ref[L - tail:L, :]
        conv_ref[0, :, SSD_D_INNER:SSD_CONV_DIM] = bc_ref[L - tail:L, :]

    lane = lax.broadcasted_iota(jnp.int32, (1, LANES), 1)
    head_ok = lane < SSD_HEADS
    dt = jnp.where(head_ok, _softplus(dt_ref[...] + dtb_ref[...]), 0.0)
    a = jnp.where(head_ok, -jnp.exp(alog_ref[...]), 0.0)
    acum = _dot_01x(tri_ref[...], dt * a)
    acum_t = acum.T
    a_last = acum[L - 1:L, :]
    expand = exp_ref[...]
    dt_e = _dot_x01(dt, expand)
    eac_e = _dot_x01(jnp.exp(acum), expand)
    dend_e = _dot_x01(jnp.exp(a_last - acum), expand)
    cdec_e = eac_e[L - 1:L, :]

    xdt = x * dt_e
    xdt_b = xdt.astype(BF16)
    xdtw_b = (xdt * dend_e).astype(BF16)
    row = lax.broadcasted_iota(jnp.int32, (L, L), 0)
    col = lax.broadcasted_iota(jnp.int32, (L, L), 1)
    causal = row >= col
    lane2 = lax.broadcasted_iota(jnp.int32, (L, 2 * SSD_P), 1)
    first_head = lane2 < SSD_P

    y_parts = []
    for g in range(SSD_GROUPS):
        b_g = bcv[:, g * SSD_N:(g + 1) * SSD_N]
        c_g = bcv[:, (SSD_GROUPS + g) * SSD_N:(SSD_GROUPS + g + 1) * SSD_N]
        b_gb = b_g.astype(BF16)
        c_gb = c_g.astype(BF16)
        cb = _dot_nt(c_gb, b_gb)
        gl = slice(g * SSD_REP * SSD_P, (g + 1) * SSD_REP * SSD_P)
        st_g = st_ref[:, gl]
        y_off = _dot(c_gb, st_g.astype(BF16)) * eac_e[:, gl]
        pair_out = []
        for j in range(SSD_REP // 2):
            pl_ = slice(g * SSD_REP * SSD_P + j * 2 * SSD_P, g * SSD_REP * SSD_P + (j + 1) * 2 * SSD_P)
            xp = xdt_b[:, pl_]
            ys = []
            for hh in range(2):
                h = g * SSD_REP + 2 * j + hh
                seg = acum[:, h:h + 1] - acum_t[h:h + 1, :]
                dec = jnp.exp(jnp.where(causal, seg, NEG))
                ys.append(_dot((cb * dec).astype(BF16), xp))
            pair_out.append(jnp.where(first_head, ys[0], ys[1]))
        y_g = jnp.concatenate(pair_out, axis=1) + y_off
        y_parts.append(y_g)
        st_ref[:, gl] = st_g * cdec_e[:, gl] + _dot(b_g.T.astype(BF16), xdtw_b[:, gl])

    y = jnp.concatenate(y_parts, axis=1) + dskip_ref[...] * x
    zg = z_ref[...]
    y = y * _silu(zg)
    r = lax.rsqrt(jnp.mean(y * y, axis=-1, keepdims=True) + EPS)
    y_ref[...] = ((y * r) * nw_ref[...]).astype(BF16)

    @pl.when(c == pl.num_programs(1) - 1)
    def _():
        ssm_ref[0] = st_ref[...].T


def _ssd_prompt(proj, bsz, t, cw, cb, dtb, alog, dskip, nw):
    L = SSD_CHUNK
    nc = t // L
    di = SSD_D_INNER
    tri = jnp.asarray(np.tril(np.ones((L, L), np.float32)), BF16)
    full = lambda shape: pl.BlockSpec(shape, lambda b, c: (0,) * len(shape))
    return pl.pallas_call(
        _ssd_prompt_kernel,
        out_shape=(jax.ShapeDtypeStruct((bsz * t, di), BF16),
                   jax.ShapeDtypeStruct((bsz, SSD_CONV_W - 1, SSD_CONV_DIM), F32),
                   jax.ShapeDtypeStruct((bsz, di, SSD_N), F32)),
        grid=(bsz, nc),
        in_specs=[pl.BlockSpec((L, di), lambda b, c: (b * nc + c, 0)),
                  pl.BlockSpec((L, di), lambda b, c: (b * nc + c, 1)),
                  pl.BlockSpec((L, SSD_BC), lambda b, c: (b * nc + c, 2 * di // SSD_BC)),
                  pl.BlockSpec((L, LANES), lambda b, c: (b * nc + c, (2 * di + SSD_BC) // LANES)),
                  full((SSD_CONV_W, di)), full((1, di)), full((SSD_CONV_W, SSD_BC)), full((1, SSD_BC)),
                  full((1, LANES)), full((1, LANES)), full((1, di)), full((1, di)),
                  full((LANES, di)), full((L, L))],
        out_specs=(pl.BlockSpec((L, di), lambda b, c: (b * nc + c, 0)),
                   pl.BlockSpec((1, SSD_CONV_W - 1, SSD_CONV_DIM), lambda b, c: (b, 0, 0)),
                   pl.BlockSpec((1, di, SSD_N), lambda b, c: (b, 0, 0))),
        scratch_shapes=[pltpu.VMEM((8 + L, di), F32), pltpu.VMEM((8 + L, SSD_BC), F32),
                        pltpu.VMEM((SSD_N, di), F32)],
        compiler_params=_cparams(("parallel", "arbitrary")),
        name="ssd_prompt",
    )(proj, proj, proj, proj, cw[:, :di], cb[:, :di], cw[:, di:], cb[:, di:], dtb, alog, dskip, nw,
      _head_expand_matrix(), tri)


def _ssd_sample_kernel(p_ref, cs_ref, st_ref, cw_ref, cb_ref, dtb_ref, alog_ref, dskip_ref, nw_ref, exp_ref,
                       y_ref, conv_ref, ssm_ref):
    di = SSD_D_INNER
    z = p_ref[0, :, 0:di]
    xbc = p_ref[0, :, di:di + SSD_CONV_DIM]
    dtr = p_ref[0, :, di + SSD_CONV_DIM:di + SSD_CONV_DIM + LANES]
    cs = cs_ref[0]
    acc = cb_ref[...] + xbc * cw_ref[SSD_CONV_W - 1:SSD_CONV_W, :]
    for k in range(SSD_CONV_W - 1):
        acc = acc + cs[k:k + 1, :] * cw_ref[k:k + 1, :]
    conv_ref[0, 0:SSD_CONV_W - 2, :] = cs[1:SSD_CONV_W - 1, :]
    conv_ref[0, SSD_CONV_W - 2:SSD_CONV_W - 1, :] = xbc
    act = _silu(acc)
    x = act[:, 0:di]
    lane = lax.broadcasted_iota(jnp.int32, (1, LANES), 1)
    head_ok = lane < SSD_HEADS
    dt = jnp.where(head_ok, _softplus(dtr + dtb_ref[...]), 0.0)
    a = jnp.where(head_ok, -jnp.exp(alog_ref[...]), 0.0)
    da = jnp.exp(dt * a)
    lhs = jnp.concatenate([dt, da, jnp.zeros((6, LANES), F32)], axis=0)
    ex = _dot_x01(lhs, exp_ref[...])
    dt_e, da_e = ex[0:1, :], ex[1:2, :]
    xdt = x * dt_e

    eye = (lax.broadcasted_iota(jnp.int32, (LANES, LANES), 0)
           == lax.broadcasted_iota(jnp.int32, (LANES, LANES), 1))

    def to_col(rowvec):
        return jnp.sum(jnp.where(eye, jnp.broadcast_to(rowvec, (LANES, LANES)), 0.0), axis=1, keepdims=True)

    def to_row(colvec):
        return jnp.sum(jnp.where(eye, jnp.broadcast_to(colvec, (LANES, LANES)), 0.0), axis=0, keepdims=True)

    y_off = []
    cbs = []
    for g in range(SSD_GROUPS):
        b_g = act[:, di + g * SSD_N:di + (g + 1) * SSD_N]
        c_g = act[:, di + (SSD_GROUPS + g) * SSD_N:di + (SSD_GROUPS + g + 1) * SSD_N]
        cbs.append(jnp.broadcast_to(jnp.sum(b_g * c_g, axis=1, keepdims=True), (1, SSD_REP * SSD_P)))
        for i in range(SSD_REP * SSD_P // LANES):
            lo = g * SSD_REP * SSD_P + i * LANES
            st = st_ref[0, lo:lo + LANES, :]
            xcol = to_col(xdt[:, lo:lo + LANES])
            dcol = to_col(da_e[:, lo:lo + LANES])
            ssm_ref[0, lo:lo + LANES, :] = st * dcol + xcol * b_g
            y_off.append(to_row(jnp.sum(st * c_g, axis=1, keepdims=True)))
    y = xdt * jnp.concatenate(cbs, axis=1) + jnp.concatenate(y_off, axis=1) * da_e + dskip_ref[...] * x
    y = y * _silu(z)
    r = lax.rsqrt(jnp.mean(y * y, axis=-1, keepdims=True) + EPS)
    y_ref[0] = ((y * r) * nw_ref[...]).astype(BF16)


def _ssd_sample(proj, conv_state, ssm_state, cw, cb, dtb, alog, dskip, nw):
    bsz = proj.shape[0]
    di = SSD_D_INNER
    full = lambda shape: pl.BlockSpec(shape, lambda b: (0,) * len(shape))
    per_b = lambda shape: pl.BlockSpec((1,) + shape, lambda b: (b,) + (0,) * len(shape))
    y, conv, ssm = pl.pallas_call(
        _ssd_sample_kernel,
        out_shape=(jax.ShapeDtypeStruct((bsz, 1, di), BF16),
                   jax.ShapeDtypeStruct((bsz, SSD_CONV_W - 1, SSD_CONV_DIM), F32),
                   jax.ShapeDtypeStruct((bsz, di, SSD_N), F32)),
        grid=(bsz,),
        in_specs=[per_b((1, SSD_NP)), per_b((SSD_CONV_W - 1, SSD_CONV_DIM)), per_b((di, SSD_N)),
                  full((SSD_CONV_W, SSD_CONV_DIM)), full((1, SSD_CONV_DIM)), full((1, LANES)), full((1, LANES)),
                  full((1, di)), full((1, di)), full((LANES, di))],
        out_specs=(per_b((1, di)), per_b((SSD_CONV_W - 1, SSD_CONV_DIM)), per_b((di, SSD_N))),
        compiler_params=_cparams(("parallel",)),
        name="ssd_sample",
    )(proj.reshape(bsz, 1, SSD_NP), conv_state, ssm_state.reshape(bsz, di, SSD_N), cw, cb, dtb, alog, dskip, nw,
      _head_expand_matrix())
    return y.reshape(bsz, di), conv, ssm


CH = CMP_STRIDE
N_L = CMP_BLOCK // 2
PE_ROWS = 16


def _compress_accumulate(lhs_fn, pe_ref, wab_ref, kv):
    acc = None
    for l in range(N_L):
        lhs = jnp.concatenate([lhs_fn(l * KVW + kv * GW).astype(BF16), pe_ref[kv, l]], axis=0)
        t = _dot(lhs, wab_ref[kv, l])
        acc = t if acc is None else acc + t
    return acc


def _compress_finish(acc, w2, nch):
    p = acc[0:nch, 0:GW]
    q_next = pltpu.roll(acc[0:nch, GW:2 * GW], nch - 1, axis=0)
    bias = acc[nch:nch + 1, 0:GW] + acc[nch + 8:nch + 9, GW:2 * GW]
    out = _dot(_silu(p + q_next + bias).astype(BF16), w2)
    row = lax.broadcasted_iota(jnp.int32, (nch, 1), 0)
    return jnp.where(row < nch - 1, out, 0.0)


def _softmax_masked(s, ok, axis):
    s = jnp.where(ok, s, NEG)
    m = jnp.max(s, axis=axis, keepdims=True)
    p = jnp.where(ok, jnp.exp(s - m), 0.0)
    d = jnp.sum(p, axis=axis, keepdims=True)
    return p * jnp.where(d > 0.0, 1.0 / d, 0.0)


def _interleave(n, scores, softmax, values, finish):
    out, pending = [], None
    s_next = scores(0)
    for k in range(n):
        s = s_next
        if k + 1 < n:
            s_next = scores(k + 1)
        p, aux = softmax(k, s)
        if pending is not None:
            out.append(finish(*pending))
        pending = (k, values(k, p), aux)
    out.append(finish(*pending))
    return out


def _topk_mask(score, j, width, shifts, axis):
    n = score.shape[axis]
    rank = jnp.zeros(score.shape, F32)
    for k in shifts:
        other = pltpu.roll(score, k, axis=axis)
        lower = j >= k
        if width != n:
            other = jnp.where(lower, other, pltpu.roll(score, n - width + k, axis=axis))
        rank = rank + jnp.where(lower, jnp.where(other >= score, 1.0, 0.0), jnp.where(other > score, 1.0, 0.0))
    return rank < SEL_TOPK


def _overlap(n_cmp, n_sel):
    c_start = np.arange(n_cmp) * CMP_STRIDE
    s_start = np.arange(n_sel) * SEL_BLOCK
    return ((c_start[:, None] < s_start[None, :] + SEL_BLOCK)
            & (c_start[:, None] + CMP_BLOCK > s_start[None, :])).astype(np.float32)


def _block_bias_matrix(width, n_keys):
    l = np.arange(LANES)[:, None] % width
    blk = np.arange(n_keys)[None, :] // SEL_BLOCK
    return np.where(l == blk, NEG_MXU, 0.0).astype(np.float32)


def _nsa_proj_kernel(x_ref, nw_ref, sh_ref, sc_ref, wr_ref, wt_ref,
                     rows_ref, kpad_ref, qt_ref, gt_ref, vt_ref, cmp_ref, sel_ref, win_ref):
    h = _modulated_norm(x_ref[...], nw_ref[...], sh_ref[0], sc_ref[0]).astype(BF16)
    y = _dot(h, wr_ref[...])
    rows_ref[...] = y[:, 0:KVW].astype(BF16)
    kpad_ref[...] = y[:, KVW:].astype(BF16)
    yt = _dot_nt(wt_ref[...], h)
    qt_ref[0] = (yt[0:1024] * (HD ** -0.5)).astype(BF16)
    cmp_ref[0] = yt[1024:1024 + KVW]
    sel_ref[0] = yt[1024 + KVW:1024 + 2 * KVW]
    vt_ref[0, 0:GW] = yt[1024 + KVW + GW:1024 + 2 * KVW].astype(BF16)
    vt_ref[0, GW:2 * GW] = yt[1024 + 2 * KVW + GW:1024 + 3 * KVW].astype(BF16)
    gt_ref[0] = _sigmoid(yt[1024 + 3 * KVW:NSA_N])

    @pl.when(pl.program_id(1) == pl.num_programs(1) - 1)
    def _():
        win_ref[0] = yt[1024 + 2 * KVW:1024 + 3 * KVW]


def _nsa_proj(x, nw, shift, scale, wr, wt, bsz, t):
    d = x.shape[1]
    tm = min(WINDOW, t)
    nb = t // tm
    row = lambda w: pl.BlockSpec((tm, w), lambda b, i: (b * nb + i, 0))
    col = lambda h: pl.BlockSpec((1, h, tm), lambda b, i: (b, 0, i))
    mod = lambda: pl.BlockSpec((1, 1, d), lambda b, i: (b, 0, 0))
    nq = NSA_HEADS * HD
    return pl.pallas_call(
        _nsa_proj_kernel,
        out_shape=(jax.ShapeDtypeStruct((bsz * t, KVW), BF16),
                   jax.ShapeDtypeStruct((bsz * t, 2 * KPAD), BF16),
                   jax.ShapeDtypeStruct((bsz, nq, t), BF16),
                   jax.ShapeDtypeStruct((bsz, LANES, t), F32),
                   jax.ShapeDtypeStruct((bsz, 2 * GW, t), BF16),
                   jax.ShapeDtypeStruct((bsz, KVW, t), F32),
                   jax.ShapeDtypeStruct((bsz, KVW, t), F32),
                   jax.ShapeDtypeStruct((bsz, KVW, tm), F32)),
        grid=(bsz, nb),
        in_specs=[row(d), pl.BlockSpec((1, d), lambda b, i: (0, 0)), mod(), mod(),
                  pl.BlockSpec((d, KVW + 2 * KPAD), lambda b, i: (0, 0)),
                  pl.BlockSpec((NSA_N, d), lambda b, i: (0, 0))],
        out_specs=(row(KVW), row(2 * KPAD), col(nq), col(LANES), col(2 * GW), col(KVW), col(KVW),
                   pl.BlockSpec((1, KVW, tm), lambda b, i: (b, 0, 0))),
        compiler_params=_cparams(("parallel", "arbitrary")),
        name="nsa_proj",
    )(x, nw, shift, scale, wr, wt)


def _cmp_prompt_kernel(rows_ref, pe_ref, wab_ref, w2k_ref, w2v_ref, kc_ref, vc_ref):
    nch = rows_ref.shape[1]
    lhs = lambda lo: rows_ref[0, :, lo:lo + GW]
    kc_ref[0] = _compress_finish(_compress_accumulate(lhs, pe_ref, wab_ref, 0), w2k_ref[...], nch).astype(BF16)
    vc_ref[0] = _compress_finish(_compress_accumulate(lhs, pe_ref, wab_ref, 1), w2v_ref[...], nch).T.astype(BF16)


def _cmp_prompt(rows, pe_t, wab, w2k_pad, w2v):
    bsz, nch, _ = rows.shape
    full = lambda shape: pl.BlockSpec(shape, lambda b: (0,) * len(shape))
    return pl.pallas_call(
        _cmp_prompt_kernel,
        out_shape=(jax.ShapeDtypeStruct((bsz, nch, KPAD), BF16), jax.ShapeDtypeStruct((bsz, GW, nch), BF16)),
        grid=(bsz,),
        in_specs=[pl.BlockSpec((1, nch, CH * KVW), lambda b: (b, 0, 0)),
                  full(pe_t.shape), full(wab.shape), full(w2k_pad.shape), full(w2v.shape)],
        out_specs=(pl.BlockSpec((1, nch, KPAD), lambda b: (b, 0, 0)), pl.BlockSpec((1, GW, nch), lambda b: (b, 0, 0))),
        compiler_params=_cparams(("parallel",)),
        name="nsa_compress_prompt",
    )(rows, pe_t, wab, w2k_pad, w2v)


def _nsa_prompt_kernel(qt_ref, gt_ref, kc_ref, vct_ref, kpad_ref, vt_ref, ovt_ref, nege_ref, o_ref,
                       *, tq, n_sel, ck):
    i = pl.program_id(1)
    t0 = i * tq
    nq = REP * tq
    tl = t0 + lax.broadcasted_iota(jnp.int32, (1, tq), 1)
    zq = jnp.zeros((HD, nq), BF16)

    def lanes4(a):
        return jnp.concatenate([a] * REP, axis=1)

    tl4 = lanes4(tl)
    qg = [jnp.concatenate([qt_ref[0, (g * REP + r) * HD:(g * REP + r + 1) * HD, :] for r in range(REP)], axis=1)
          for g in range(KVH)]
    q_rhs = [jnp.concatenate([q, zq], axis=0) for q in qg]

    ncp = kc_ref.shape[1]
    cend = lax.broadcasted_iota(jnp.int32, (ncp, 1), 0) * CMP_STRIDE + (CMP_BLOCK - 1)
    ok_cmp = cend <= tl4
    imp = jnp.zeros((LANES, tq), F32)
    o_cmp = []
    for g in range(KVH):
        p = _softmax_masked(_dot(kc_ref[0, :, g * LANES:(g + 1) * LANES], q_rhs[g]), ok_cmp, 0)
        o_cmp.append(_dot(vct_ref[0, g * HD:(g + 1) * HD, :], p.astype(BF16)))
        psum = p[:, 0:tq] + p[:, tq:2 * tq] + p[:, 2 * tq:3 * tq] + p[:, 3 * tq:4 * tq]
        imp = imp + _dot_01x(ovt_ref[g], psum)

    row = lax.broadcasted_iota(jnp.int32, (LANES, tq), 0)
    jj = row % n_sel
    cur = tl // SEL_BLOCK
    valid = jj * SEL_BLOCK <= tl
    forced = (jj == 0) | (jj == cur) | (jj == cur - 1)
    score = jnp.where(valid, jnp.where(forced, BIG, imp), NEG)
    top = _topk_mask(score, jj, n_sel, range(1, n_sel), 0)
    blocked = jnp.where(top & valid, 0.0, 1.0)

    n_full = t0 // ck
    sel_rhs = [jnp.concatenate([lanes4(jnp.where(row // n_sel == g, blocked, 0.0).astype(BF16)), q_rhs[g]], axis=0)
               for g in range(KVH)]

    def sel_chunk(c, carries, causal):
        k0 = pl.multiple_of(c * ck, ck)
        bias_rows = nege_ref[pl.ds(k0, ck), :]
        if causal:
            cbias = jnp.where(k0 + lax.broadcasted_iota(jnp.int32, (ck, 1), 0) <= tl4, 0.0, NEG)

        def scores(g):
            lhs = jnp.concatenate([bias_rows, kpad_ref[0, pl.ds(k0, ck), g * LANES:(g + 1) * LANES]], axis=1)
            s = _dot(lhs, sel_rhs[g])
            return s + cbias if causal else s

        def softmax(g, s):
            m, l, acc = carries[g]
            m_new = jnp.maximum(m, jnp.max(s, axis=0, keepdims=True))
            alpha = jnp.exp(m - m_new)
            p = jnp.exp(s - m_new)
            return p.astype(BF16), (m_new, alpha * l + jnp.sum(p, axis=0, keepdims=True), alpha, acc)

        def values(g, p):
            return _dot(vt_ref[0, g * HD:(g + 1) * HD, pl.ds(k0, ck)], p)

        def finish(g, pv, aux):
            m_new, l_new, alpha, acc = aux
            return m_new, l_new, alpha * acc + pv

        return tuple(_interleave(KVH, scores, softmax, values, finish))

    init = (jnp.full((1, nq), NEG, F32), jnp.zeros((1, nq), F32), jnp.zeros((HD, nq), F32))
    carries = lax.fori_loop(0, n_full, lambda c, cr: sel_chunk(c, cr, False), (init,) * KVH)
    o_sel = [acc * (1.0 / l) for _, l, acc in sel_chunk(n_full, carries, True)]

    kw = WINDOW + tq
    k_start = pl.multiple_of(jnp.maximum(t0 - WINDOW, 0), LANES)
    dpos = tl - (k_start + lax.broadcasted_iota(jnp.int32, (kw, 1), 0))
    wbias = lanes4(jnp.where((dpos >= 0) & (dpos <= WINDOW), 0.0, NEG))

    def win_scores(g):
        return _dot(kpad_ref[0, pl.ds(k_start, kw), (KVH + g) * LANES:(KVH + g + 1) * LANES], q_rhs[g]) + wbias

    def win_softmax(g, s):
        p = jnp.exp(s - jnp.max(s, axis=0, keepdims=True))
        return p.astype(BF16), jnp.sum(p, axis=0, keepdims=True)

    def win_values(g, p):
        return _dot(vt_ref[0, GW + g * HD:GW + (g + 1) * HD, pl.ds(k_start, kw)], p)

    o_win = _interleave(KVH, win_scores, win_softmax, win_values, lambda g, pv, l: pv * (1.0 / l))

    gt = gt_ref[0]
    outs = []
    for g in range(KVH):
        for r in range(REP):
            h = g * REP + r
            sl = slice(r * tq, (r + 1) * tq)
            outs.append(gt[3 * h:3 * h + 1, :] * o_cmp[g][:, sl] + gt[3 * h + 1:3 * h + 2, :] * o_sel[g][:, sl]
                        + gt[3 * h + 2:3 * h + 3, :] * o_win[g][:, sl])
    o_ref[0] = jnp.concatenate(outs, axis=0).T.astype(BF16)


def _nsa_prompt_attention(qt, gt, kc, vct, kpad, vt):
    bsz, nd, t = qt.shape
    tq = LANES
    ck = 4 * LANES
    n_sel = t // SEL_BLOCK
    ncp = t // CH
    assert KVH * n_sel == LANES and ncp == LANES and t % ck == 0
    ovt = np.zeros((KVH, LANES, ncp), np.float32)
    for g in range(KVH):
        ovt[g, g * n_sel:(g + 1) * n_sel, :ncp - 1] = _overlap(ncp - 1, n_sel).T
    nege_t = _block_bias_matrix(n_sel, t).T
    per_b = lambda shape: pl.BlockSpec((1,) + shape, lambda b, i: (b,) + (0,) * len(shape))
    full = lambda shape: pl.BlockSpec(shape, lambda b, i: (0,) * len(shape))
    out = pl.pallas_call(
        functools.partial(_nsa_prompt_kernel, tq=tq, n_sel=n_sel, ck=ck),
        out_shape=jax.ShapeDtypeStruct((bsz, t, nd), BF16),
        grid=(bsz, t // tq),
        in_specs=[pl.BlockSpec((1, nd, tq), lambda b, i: (b, 0, i)),
                  pl.BlockSpec((1, LANES, tq), lambda b, i: (b, 0, i)),
                  per_b((ncp, KPAD)), per_b((GW, ncp)), per_b((t, 2 * KPAD)), per_b((2 * GW, t)),
                  full((KVH, LANES, ncp)), full((t, LANES))],
        out_specs=pl.BlockSpec((1, tq, nd), lambda b, i: (b, i, 0)),
        compiler_params=_cparams(("parallel", "arbitrary")),
        name="nsa_prompt_attention",
    )(qt, gt, kc, vct, kpad.reshape(bsz, t, 2 * KPAD), vt, jnp.asarray(ovt, BF16), jnp.asarray(nege_t, BF16))
    return out.reshape(bsz * t, nd)


def _nsa_sample_kernel(pt_ref, p_ref, *rest, n_pages, past):
    del pt_ref
    cmp_pages = rest[:n_pages]
    sel_pages = rest[n_pages:2 * n_pages]
    win_ref, pe_ref, wab_ref, w2_ref, ov_ref, nege_ref, o_ref, wout_ref = rest[2 * n_pages:]
    nch = n_pages * (PAGE_SIZE // CH)
    t = past
    keep = win_ref.shape[-1]
    rows_q = 8

    cmp_t = []
    for kv in range(2):
        acc = _compress_accumulate(
            lambda lo: jnp.concatenate([pg[0, 0, :, lo:lo + GW] for pg in cmp_pages], axis=0), pe_ref, wab_ref, kv)
        cmp_t.append(_compress_finish(acc, w2_ref[kv], nch).T.astype(BF16))
    kct, vct = cmp_t

    lane = lax.broadcasted_iota(jnp.int32, (rows_q, LANES), 1)
    rowi = lax.broadcasted_iota(jnp.int32, (rows_q, LANES), 0)
    cend = lax.broadcasted_iota(jnp.int32, (1, nch), 1) * CMP_STRIDE + (CMP_BLOCK - 1)
    n_sel = -(-(t + 1) // SEL_BLOCK)
    nj = max(n_sel, SEL_TOPK)
    cur = t // SEL_BLOCK
    valid = (lane < n_sel) & (lane * SEL_BLOCK <= t)
    forced = (lane == 0) | (lane == cur) | (lane == cur - 1)
    shifts = list(range(1, nj)) + list(range(LANES - nj + 1, LANES))
    gates = jnp.broadcast_to(_sigmoid(p_ref[0, :, 1024 + 3 * KVW:NSA_N]), (rows_q, LANES))
    eye = (lax.broadcasted_iota(jnp.int32, (HD, HD), 0) == lax.broadcasted_iota(jnp.int32, (HD, HD), 1))
    last_lane = lax.broadcasted_iota(jnp.int32, (HD, keep), 1) == keep - 1

    def attend(s, s_new, ok_new, vt, v_new):
        s_new = jnp.where(ok_new, s_new, NEG)
        m = jnp.maximum(jnp.max(s, axis=1, keepdims=True), s_new)
        pr = jnp.exp(s - m)
        pr_new = jnp.where(ok_new, jnp.exp(s_new - m), 0.0)
        l = jnp.sum(pr, axis=1, keepdims=True) + pr_new
        return (_dot_nt(pr.astype(BF16), vt) + pr_new * v_new) * (1.0 / l)

    for g in range(KVH):
        q_g = jnp.concatenate([p_ref[0, :, (g * REP + r) * HD:(g * REP + r + 1) * HD] for r in range(REP)]
                              + [jnp.zeros((rows_q - REP, HD), F32)], axis=0) * (HD ** -0.5)
        q_gb = q_g.astype(BF16)

        p = _softmax_masked(_dot(q_gb, kct[g * HD:(g + 1) * HD, :]), cend <= t, 1)
        o_cmp = _dot_nt(p.astype(BF16), vct[g * HD:(g + 1) * HD, :])
        psum = jnp.broadcast_to(jnp.sum(p[0:REP], axis=0, keepdims=True), (rows_q, nch))
        imp = _dot_x01(psum, ov_ref[...])
        score = jnp.where(valid, jnp.where(forced, BIG, imp), NEG)
        score = jnp.where(lane < nj, score, -3e38)
        allowed = _topk_mask(score, lane, LANES, shifts, 1) & valid
        blocked = jnp.where(allowed, 0.0, 1.0).astype(BF16)

        kt = jnp.concatenate([pg[0, 0, 0, g] for pg in sel_pages], axis=1).astype(BF16)
        vt = jnp.concatenate([pg[0, 0, 1, g] for pg in sel_pages], axis=1).astype(BF16)
        s = _dot(jnp.concatenate([blocked, q_gb], axis=1), jnp.concatenate([nege_ref[...], kt], axis=0))
        k_new = p_ref[0, :, 1024 + KVW + g * HD:1024 + KVW + (g + 1) * HD]
        v_new = p_ref[0, :, 1024 + KVW + GW + g * HD:1024 + KVW + GW + (g + 1) * HD]
        new_ok = jnp.sum(jnp.where((lane == cur) & allowed, 1.0, 0.0), axis=1, keepdims=True) > 0.5
        o_sel = attend(s, jnp.sum(q_g * k_new, axis=1, keepdims=True), new_ok, vt, v_new)

        kt = win_ref[0, 0, 0, g]
        vt = win_ref[0, 0, 1, g]
        k_new = p_ref[0, :, 1024 + 2 * KVW + g * HD:1024 + 2 * KVW + (g + 1) * HD]
        v_new = p_ref[0, :, 1024 + 2 * KVW + GW + g * HD:1024 + 2 * KVW + GW + (g + 1) * HD]
        o_win = attend(_dot(q_gb, kt.astype(BF16)), jnp.sum(q_g * k_new, axis=1, keepdims=True),
                       jnp.full((rows_q, 1), True), vt.astype(BF16), v_new)
        for kv, old, new in ((0, kt, k_new), (1, vt, v_new)):
            col = jnp.sum(jnp.where(eye, jnp.broadcast_to(new, (HD, HD)), 0.0), axis=1, keepdims=True)
            wout_ref[0, kv, g] = jnp.where(last_lane, col, pltpu.roll(old, keep - 1, axis=1))

        o = None
        for br, ob in enumerate((o_cmp, o_sel, o_win)):
            gcol = jnp.sum(jnp.where(lane == (g * REP + rowi) * 3 + br, gates, 0.0), axis=1, keepdims=True)
            o = gcol * ob if o is None else o + gcol * ob
        for r in range(REP):
            h = g * REP + r
            o_ref[0, :, h * HD:(h + 1) * HD] = o[r:r + 1, :].astype(BF16)


def _nsa_sample(proj, cmp_view, sel_view, win_view, layer, page_table, pe_t, wab, w2bd):
    bsz, n_pages = page_table.shape
    past = n_pages * PAGE_SIZE
    keep = win_view.shape[-1]
    nch = past // CH
    assert nch == LANES and keep <= WINDOW and past % SEL_BLOCK == 0 and past - keep >= 0
    n_sel = -(-(past + 1) // SEL_BLOCK)
    ov = np.zeros((nch, LANES), np.float32)
    ov[:nch - 1, :n_sel] = _overlap(nch - 1, n_sel)
    full = lambda shape: pl.BlockSpec(shape, lambda b, pt: (0,) * len(shape))
    per_b = lambda shape: pl.BlockSpec((1,) + shape, lambda b, pt: (b,) + (0,) * len(shape))
    page = lambda shape, p: pl.BlockSpec((1, 1) + shape, lambda b, pt: (layer, pt[b * n_pages + p]) + (0,) * len(shape))
    win_shape = (2, KVH, HD, keep)
    grid_spec = pltpu.PrefetchScalarGridSpec(
        num_scalar_prefetch=1,
        grid=(bsz,),
        in_specs=([per_b((1, NSA_N))]
                  + [page((PAGE_SIZE // CH, CH * KVW), p) for p in range(n_pages)]
                  + [page((2, KVH, HD, PAGE_SIZE), p) for p in range(n_pages)]
                  + [pl.BlockSpec((1, 1) + win_shape, lambda b, pt: (layer, b, 0, 0, 0, 0)),
                     full(pe_t.shape), full(wab.shape), full(w2bd.shape), full((nch, LANES)), full((LANES, past))]),
        out_specs=(per_b((1, NSA_HEADS * HD)), per_b(win_shape)),
    )
    o, wout = pl.pallas_call(
        functools.partial(_nsa_sample_kernel, n_pages=n_pages, past=past),
        out_shape=(jax.ShapeDtypeStruct((bsz, 1, NSA_HEADS * HD), BF16),
                   jax.ShapeDtypeStruct((bsz,) + win_shape, F32)),
        grid_spec=grid_spec,
        compiler_params=_cparams(("arbitrary",)),
        name="nsa_sample",
    )(page_table.reshape(-1), proj.reshape(bsz, 1, NSA_N), *([cmp_view] * n_pages), *([sel_view] * n_pages),
      win_view, pe_t, wab, w2bd, jnp.asarray(ov, BF16), jnp.asarray(_block_bias_matrix(LANES, past), BF16))
    return o.reshape(bsz, NSA_HEADS * HD), wout


def _prep_nsa(w_in, w_out, pe, w1, w2):
    n_gate = 3 * NSA_HEADS
    w_full = jnp.concatenate([w_in, jnp.zeros((D_MODEL, LANES - n_gate), F32)], axis=1)

    def k_padded(lo):
        k = w_in[:, lo:lo + GW].reshape(D_MODEL, KVH, HD)
        return jnp.concatenate([k, jnp.zeros_like(k)], axis=2).reshape(D_MODEL, KPAD)

    w_rows = jnp.concatenate([w_in[:, 1024:1024 + KVW], k_padded(1024 + KVW), k_padded(1024 + 2 * KVW)], axis=1)
    eye = jnp.eye(KVH, dtype=F32)
    w1r = w1.reshape(2, 2, N_L, HD, HD)
    wab = jnp.einsum('khlde,gf->klgdhfe', w1r, eye).reshape(2, N_L, GW, 2 * GW)
    w2bd = jnp.einsum('kde,gf->kgdfe', w2, eye)
    w2k_pad = jnp.concatenate([w2bd[0], jnp.zeros_like(w2bd[0])], axis=3).reshape(GW, KPAD)
    per = pe.reshape(2, 2, N_L, 1, 1, HD)
    pe_t = jnp.broadcast_to(per, (2, 2, N_L, 8, KVH, HD)).transpose(0, 2, 1, 3, 4, 5).reshape(2, N_L, PE_ROWS, GW)
    b = lambda a: a.astype(BF16)
    return dict(w_sample=b(w_full), w_rows=b(w_rows), w_t=b(w_full.T), wo=b(w_out), pe_t=b(pe_t), wab=b(wab),
                w2bd=b(w2bd.reshape(2, GW, GW)), w2k_pad=b(w2k_pad))


def _nsa_prompt_layer(x, nw, shift, scale, prep, bsz, t):
    rows, kpad, qt, gt, vt, cmp_t, sel_t, win_t = _nsa_proj(x, nw, shift, scale, prep["w_rows"], prep["w_t"], bsz, t)
    kc, vct = _cmp_prompt(rows.reshape(bsz, t // CH, CH * KVW), prep["pe_t"], prep["wab"], prep["w2k_pad"],
                          prep["w2bd"][1])
    return _nsa_prompt_attention(qt, gt, kc, vct, kpad, vt), (cmp_t, sel_t, win_t)


def _prep_ssd(w_in, dt_bias, a_log, d_skip):
    pad = SSD_NP - w_in.shape[1]
    w = jnp.concatenate([w_in, jnp.zeros((D_MODEL, pad), F32)], axis=1).astype(BF16)
    pad_h = lambda v: jnp.concatenate([v, jnp.zeros((LANES - SSD_HEADS,), F32)]).reshape(1, LANES)
    return w, pad_h(dt_bias), pad_h(a_log), jnp.repeat(d_skip, SSD_P).reshape(1, SSD_D_INNER)


def kernel(x_prompt, x_sample, cache_kv_cmp, cache_kv_sel, cache_kv_win, state_ssm, state_conv, page_table, c_prompt, c_sample, ada_w, ada_b, norm_w, mlp_w1, mlp_w2, nsa_w_in, nsa_w_out, nsa_cmp_pe, nsa_cmp_w1, nsa_cmp_w2, ssd_w_in, ssd_conv_w, ssd_conv_b, ssd_dt_bias, ssd_a_log, ssd_d, ssd_norm_w, ssd_w_out, final_norm_w):
    bp, t, d = x_prompt.shape
    bs = x_sample.shape[0]
    xp = x_prompt.reshape(bp * t, d)
    xs = x_sample.reshape(bs, d)
    mods = _adaln(jnp.concatenate([c_prompt, c_sample], axis=0), ada_w, ada_b)
    w1b = mlp_w1.astype(BF16)
    w2b = mlp_w2.astype(BF16)
    fnw = final_norm_w.reshape(1, d)
    tm = 512
    n_nsa, n_pool = cache_kv_cmp.shape[:2]
    cmp_view = cache_kv_cmp.reshape(n_nsa, n_pool, PAGE_SIZE // CH, CH * KVW)
    sel_view = jnp.transpose(cache_kv_sel, (0, 1, 3, 4, 5, 2))
    win_view = jnp.transpose(cache_kv_win, (0, 1, 3, 4, 5, 2))
    outs = {k: [] for k in ("cmp_p", "cmp_s", "sel_p", "sel_s", "win_p", "win_s", "ssm_p", "ssm_s", "conv_p", "conv_s")}
    kv_rows = lambda a, n: a.reshape(n + (2, KVH, HD))
    kv_cols = lambda a: a.reshape(a.shape[0], 2, KVH, HD, a.shape[-1])

    for i in range(DEPTH):
        jl = i // 2
        mp = [mods[i, :bp, k * d:(k + 1) * d].reshape(bp, 1, d) for k in range(6)]
        ms = [mods[i, bp:, k * d:(k + 1) * d].reshape(1, bs, d) for k in range(6)]
        nw0 = norm_w[i, 0].reshape(1, d)
        nw1 = norm_w[i, 1].reshape(1, d)
        if i % 2 == 0:
            prep = _prep_nsa(nsa_w_in[jl], nsa_w_out[jl], nsa_cmp_pe[jl], nsa_cmp_w1[jl], nsa_cmp_w2[jl])
            wo = prep["wo"]
            ap, (cmp_t, sel_t, win_t) = _nsa_prompt_layer(xp, nw0, mp[0], mp[1], prep, bp, t)
            ps = _mod_matmul(xs, nw0, ms[0], ms[1], prep["w_sample"], tm=bs, tn=896, rows_per_mod=None)
            as_, wout = _nsa_sample(ps, cmp_view, sel_view, win_view, jl, page_table, prep["pe_t"], prep["wab"],
                                    prep["w2bd"])
            outs["cmp_p"].append(kv_cols(cmp_t))
            outs["sel_p"].append(kv_cols(sel_t))
            outs["win_p"].append(kv_cols(win_t))
            outs["win_s"].append(wout)
            outs["cmp_s"].append(kv_rows(ps[:, 1024:1024 + KVW], (bs, 1)))
            outs["sel_s"].append(kv_rows(ps[:, 1024 + KVW:1024 + 2 * KVW], (bs, 1)))
        else:
            w, dtb, alog, dsk = _prep_ssd(ssd_w_in[jl], ssd_dt_bias[jl], ssd_a_log[jl], ssd_d[jl])
            wo = ssd_w_out[jl].astype(BF16)
            snw = ssd_norm_w[jl].reshape(1, SSD_D_INNER)
            cw = ssd_conv_w[jl]
            cb = ssd_conv_b[jl].reshape(1, SSD_CONV_DIM)
            pp = _mod_matmul(xp, nw0, mp[0], mp[1], w, tm=tm, tn=896, rows_per_mod=t)
            ps = _mod_matmul(xs, nw0, ms[0], ms[1], w, tm=bs, tn=896, rows_per_mod=None)
            ap, conv_p, ssm_p = _ssd_prompt(pp, bp, t, cw, cb, dtb, alog, dsk, snw)
            as_, conv_s, ssm_s = _ssd_sample(ps, state_conv[jl], state_ssm[jl], cw, cb, dtb, alog, dsk, snw)
            outs["conv_p"].append(conv_p)
            outs["conv_s"].append(conv_s)
            outs["ssm_p"].append(ssm_p.reshape(bp, SSD_HEADS, SSD_P, SSD_N))
            outs["ssm_s"].append(ssm_s.reshape(bs, SSD_HEADS, SSD_P, SSD_N))
        last = i == DEPTH - 1
        xp = _post_mlp(xp, ap, wo, mp[2], nw1, mp[3], mp[4], mp[5], w1b[i], w2b[i], fnw,
                       tm=tm, tf=512, rows_per_mod=t, final_norm=last)
        xs = _post_mlp(xs, as_, wo, ms[2], nw1, ms[3], ms[4], ms[5], w1b[i], w2b[i], fnw,
                       tm=bs, tf=512, rows_per_mod=None, final_norm=last)

    st = lambda k: jnp.stack(outs[k])
    time_major = lambda k: jnp.transpose(st(k), (0, 1, 5, 2, 3, 4))
    return (xp.reshape(bp, t, d), xs.reshape(bs, 1, d), time_major("cmp_p"), st("cmp_s"), time_major("sel_p"),
            st("sel_s"), time_major("win_p"), time_major("win_s"), st("ssm_p"), st("ssm_s"),
            st("conv_p"), st("conv_s"))
```

```python
import functools
import math

import numpy as np
import jax
import jax.numpy as jnp
from jax import lax
from jax.experimental import pallas as pl
from jax.experimental.pallas import tpu as pltpu

F32 = jnp.float32
BF16 = jnp.bfloat16

D_MODEL = 1024
DEPTH = 4
D_FF = 4 * D_MODEL
EPS = 1e-6
PAGE_SIZE = 128
NSA_HEADS = 16
HD = 64
KVH = 4
REP = 4
KVW = 2 * KVH * HD
GW = KVH * HD
CMP_BLOCK = 32
CMP_STRIDE = 16
SEL_BLOCK = 64
SEL_TOPK = 8
WINDOW = 512
SSD_D_INNER = 2 * D_MODEL
SSD_P = 64
SSD_HEADS = SSD_D_INNER // SSD_P
SSD_GROUPS = 4
SSD_REP = SSD_HEADS // SSD_GROUPS
SSD_N = 128
SSD_CONV_W = 4
SSD_CHUNK = 256
SSD_BC = 2 * SSD_GROUPS * SSD_N
SSD_CONV_DIM = SSD_D_INNER + SSD_BC
BIG = 1e30
NEG = -1e30
NEG_MXU = -2.0 ** 100
LANES = 128
NSA_N = 1024 + 3 * KVW + LANES
KPAD = KVH * LANES
SSD_NP = SSD_D_INNER + SSD_CONV_DIM + 2 * LANES
VMEM_LIMIT = 48 * 1024 * 1024


def _cparams(sem):
    return pltpu.CompilerParams(dimension_semantics=sem, vmem_limit_bytes=VMEM_LIMIT)


def _dot(a, b):
    return jnp.dot(a, b, preferred_element_type=F32)


def _dot_nt(a, b):
    return lax.dot_general(a, b, (((1,), (1,)), ((), ())), preferred_element_type=F32)


def _split(x, n):
    parts, r = [], x
    for i in range(n):
        p = r.astype(BF16)
        parts.append(p)
        if i + 1 < n:
            r = r - p.astype(F32)
    return parts


def _dot_x01(x, m01, n=3):
    acc = None
    for p in _split(x, n):
        t = _dot(p, m01)
        acc = t if acc is None else acc + t
    return acc


def _dot_01x(m01, x, n=3):
    acc = None
    for p in _split(x, n):
        t = _dot(m01, p)
        acc = t if acc is None else acc + t
    return acc


def _silu(x):
    return x * (1.0 / (1.0 + jnp.exp(-x)))


def _sigmoid(x):
    return 1.0 / (1.0 + jnp.exp(-x))


def _softplus(x):
    return jnp.maximum(x, 0.0) + jnp.log(1.0 + jnp.exp(-jnp.abs(x)))


def _modulated_norm(x, nw, shift, scale):
    r = lax.rsqrt(jnp.mean(x * x, axis=-1, keepdims=True) + EPS)
    return (x * r) * nw * (1.0 + scale) + shift


def _adaln_kernel(c_ref, w_ref, b_ref, o_ref):
    a = _silu(c_ref[...]).astype(BF16)
    o_ref[0] = _dot(a, w_ref[0].astype(BF16)) + b_ref[0]


def _adaln(c_all, ada_w, ada_b):
    m, d = c_all.shape
    n = ada_w.shape[-1]
    tn = 1536
    return pl.pallas_call(
        _adaln_kernel,
        out_shape=jax.ShapeDtypeStruct((DEPTH, m, n), F32),
        grid=(DEPTH, n // tn),
        in_specs=[pl.BlockSpec((m, d), lambda l, j: (0, 0)),
                  pl.BlockSpec((1, d, tn), lambda l, j: (l, 0, j)),
                  pl.BlockSpec((1, 1, tn), lambda l, j: (l, 0, j))],
        out_specs=pl.BlockSpec((1, m, tn), lambda l, j: (l, 0, j)),
        compiler_params=_cparams(("parallel", "parallel")),
        name="adaln",
    )(c_all, ada_w, ada_b.reshape(DEPTH, 1, n))


def _modmm_kernel(x_ref, nw_ref, sh_ref, sc_ref, w_ref, o_ref, h_ref):
    @pl.when(pl.program_id(1) == 0)
    def _():
        h_ref[...] = _modulated_norm(x_ref[...], nw_ref[...], sh_ref[0], sc_ref[0]).astype(BF16)

    o_ref[...] = _dot(h_ref[...], w_ref[...])


def _mod_rows(tm, rows_per_mod):
    if rows_per_mod is None:
        return lambda shape: pl.BlockSpec((1,) + shape[1:], lambda i, j: (0, 0, 0))
    bpb = rows_per_mod // tm
    return lambda shape: pl.BlockSpec((1,) + shape[1:], lambda i, j: (i // bpb, 0, 0))


def _mod_matmul(x, nw, shift, scale, w, *, tm, tn, rows_per_mod):
    m, d = x.shape
    n = w.shape[1]
    spec = _mod_rows(tm, rows_per_mod)
    return pl.pallas_call(
        _modmm_kernel,
        out_shape=jax.ShapeDtypeStruct((m, n), F32),
        grid=(m // tm, n // tn),
        in_specs=[pl.BlockSpec((tm, d), lambda i, j: (i, 0)),
                  pl.BlockSpec((1, d), lambda i, j: (0, 0)),
                  spec(shift.shape), spec(scale.shape),
                  pl.BlockSpec((d, tn), lambda i, j: (0, j))],
        out_specs=pl.BlockSpec((tm, tn), lambda i, j: (i, j)),
        scratch_shapes=[pltpu.VMEM((tm, d), BF16)],
        compiler_params=_cparams(("parallel", "arbitrary")),
        name="mod_matmul",
    )(x, nw, shift, scale, w)


def _post_mlp_kernel(x_ref, a_ref, wo_ref, g1_ref, nw_ref, sh_ref, sc_ref, g2_ref, w1_ref, w2_ref, fnw_ref,
                     o_ref, x1_ref, h_ref, acc_ref, *, final_norm):
    k = pl.program_id(1)

    @pl.when(k == 0)
    def _():
        x1 = x_ref[...] + g1_ref[0] * _dot(a_ref[...], wo_ref[...])
        x1_ref[...] = x1
        h_ref[...] = _modulated_norm(x1, nw_ref[...], sh_ref[0], sc_ref[0]).astype(BF16)
        acc_ref[...] = jnp.zeros_like(acc_ref)

    u = jnp.maximum(_dot(h_ref[...], w1_ref[...]), 0.0)
    acc_ref[...] += _dot((u * u).astype(BF16), w2_ref[...])

    @pl.when(k == pl.num_programs(1) - 1)
    def _():
        y = x1_ref[...] + g2_ref[0] * acc_ref[...]
        if final_norm:
            r = lax.rsqrt(jnp.mean(y * y, axis=-1, keepdims=True) + EPS)
            y = (y * r) * fnw_ref[...]
        o_ref[...] = y


def _post_mlp(x, a, wo, g1, nw, shift, scale, g2, w1, w2, fnw, *, tm, tf, rows_per_mod, final_norm):
    m, d = x.shape
    ka = a.shape[1]
    ff = w1.shape[1]
    spec = _mod_rows(tm, rows_per_mod)
    return pl.pallas_call(
        functools.partial(_post_mlp_kernel, final_norm=final_norm),
        out_shape=jax.ShapeDtypeStruct((m, d), F32),
        grid=(m // tm, ff // tf),
        in_specs=[pl.BlockSpec((tm, d), lambda i, k: (i, 0)),
                  pl.BlockSpec((tm, ka), lambda i, k: (i, 0)),
                  pl.BlockSpec((ka, d), lambda i, k: (0, 0)),
                  spec(g1.shape),
                  pl.BlockSpec((1, d), lambda i, k: (0, 0)),
                  spec(shift.shape), spec(scale.shape), spec(g2.shape),
                  pl.BlockSpec((d, tf), lambda i, k: (0, k)),
                  pl.BlockSpec((tf, d), lambda i, k: (k, 0)),
                  pl.BlockSpec((1, d), lambda i, k: (0, 0))],
        out_specs=pl.BlockSpec((tm, d), lambda i, k: (i, 0)),
        scratch_shapes=[pltpu.VMEM((tm, d), F32), pltpu.VMEM((tm, d), BF16), pltpu.VMEM((tm, d), F32)],
        compiler_params=_cparams(("parallel", "arbitrary")),
        name="post_mlp",
    )(x, a, wo, g1, nw, shift, scale, g2, w1, w2, fnw)


def _head_expand_matrix():
    e = np.zeros((LANES, SSD_D_INNER), np.float32)
    for h in range(SSD_HEADS):
        e[h, h * SSD_P:(h + 1) * SSD_P] = 1.0
    return jnp.asarray(e, BF16)


def _ssd_prompt_kernel(z_ref, x_ref, bc_ref, dt_ref, cwx_ref, cbx_ref, cwb_ref, cbb_ref, dtb_ref, alog_ref,
                       dskip_ref, nw_ref, exp_ref, tri_ref,
                       y_ref, conv_ref, ssm_ref, xs_ref, bs_ref, st_ref):
    c = pl.program_id(1)
    L = SSD_CHUNK
    tail = SSD_CONV_W - 1

    @pl.when(c == 0)
    def _():
        xs_ref[0:8, :] = jnp.zeros((8, SSD_D_INNER), F32)
        bs_ref[0:8, :] = jnp.zeros((8, SSD_BC), F32)
        st_ref[...] = jnp.zeros_like(st_ref)

    xs_ref[8:8 + L, :] = x_ref[...]
    bs_ref[8:8 + L, :] = bc_ref[...]

    def conv(buf, w_ref, b_ref):
        acc = b_ref[...] + buf[8 - tail:8 - tail + L, :] * w_ref[0:1, :]
        for k in range(1, SSD_CONV_W):
            acc = acc + buf[8 - tail + k:8 - tail + k + L, :] * w_ref[k:k + 1, :]
        return _silu(acc)

    x = conv(xs_ref, cwx_ref, cbx_ref)
    bcv = conv(bs_ref, cwb_ref, cbb_ref)
    xs_ref[8 - tail:8, :] = xs_ref[8 + L - tail:8 + L, :]
    bs_ref[8 - tail:8, :] = bs_ref[8 + L - tail:8 + L, :]

    @pl.when(c == pl.num_programs(1) - 1)
    def _():
        conv_ref[0, :, 0:SSD_D_INNER] = x_ref[L - tail:L, :]
        conv_ref[0, :, SSD_D_INNER:SSD_CONV_DIM] = bc_ref[L - tail:L, :]

    lane = lax.broadcasted_iota(jnp.int32, (1, LANES), 1)
    head_ok = lane < SSD_HEADS
    dt = jnp.where(head_ok, _softplus(dt_ref[...] + dtb_ref[...]), 0.0)
    a = jnp.where(head_ok, -jnp.exp(alog_ref[...]), 0.0)
    acum = _dot_01x(tri_ref[...], dt * a)
    acum_t = acum.T
    a_last = acum[L - 1:L, :]
    expand = exp_ref[...]
    dt_e = _dot_x01(dt, expand)
    eac_e = _dot_x01(jnp.exp(acum), expand)
    dend_e = _dot_x01(jnp.exp(a_last - acum), expand)
    cdec_e = eac_e[L - 1:L, :]

    xdt = x * dt_e
    xdt_b = xdt.astype(BF16)
    xdtw_b = (xdt * dend_e).astype(BF16)
    row = lax.broadcasted_iota(jnp.int32, (L, L), 0)
    col = lax.broadcasted_iota(jnp.int32, (L, L), 1)
    causal = row >= col
    lane2 = lax.broadcasted_iota(jnp.int32, (L, 2 * SSD_P), 1)
    first_head = lane2 < SSD_P

    y_parts = []
    for g in range(SSD_GROUPS):
        b_g = bcv[:, g * SSD_N:(g + 1) * SSD_N]
        c_g = bcv[:, (SSD_GROUPS + g) * SSD_N:(SSD_GROUPS + g + 1) * SSD_N]
        b_gb = b_g.astype(BF16)
        c_gb = c_g.astype(BF16)
        cb = _dot_nt(c_gb, b_gb)
        gl = slice(g * SSD_REP * SSD_P, (g + 1) * SSD_REP * SSD_P)
        st_g = st_ref[:, gl]
        y_off = _dot(c_gb, st_g.astype(BF16)) * eac_e[:, gl]
        pair_out = []
        for j in range(SSD_REP // 2):
            pl_ = slice(g * SSD_REP * SSD_P + j * 2 * SSD_P, g * SSD_REP * SSD_P + (j + 1) * 2 * SSD_P)
            xp = xdt_b[:, pl_]
            ys = []
            for hh in range(2):
                h = g * SSD_REP + 2 * j + hh
                seg = acum[:, h:h + 1] - acum_t[h:h + 1, :]
                dec = jnp.exp(jnp.where(causal, seg, NEG))
                ys.append(_dot((cb * dec).astype(BF16), xp))
            pair_out.append(jnp.where(first_head, ys[0], ys[1]))
        y_g = jnp.concatenate(pair_out, axis=1) + y_off
        y_parts.append(y_g)
        st_ref[:, gl] = st_g * cdec_e[:, gl] + _dot(b_g.T.astype(BF16), xdtw_b[:, gl])

    y = jnp.concatenate(y_parts, axis=1) + dskip_ref[...] * x
    zg = z_ref[...]
    y = y * _silu(zg)
    r = lax.rsqrt(jnp.mean(y * y, axis=-1, keepdims=True) + EPS)
    y_ref[...] = ((y * r) * nw_ref[...]).astype(BF16)

    @pl.when(c == pl.num_programs(1) - 1)
    def _():
        ssm_ref[0] = st_ref[...].T


def _ssd_prompt(proj, bsz, t, cw, cb, dtb, alog, dskip, nw):
    L = SSD_CHUNK
    nc = t // L
    di = SSD_D_INNER
    tri = jnp.asarray(np.tril(np.ones((L, L), np.float32)), BF16)
    full = lambda shape: pl.BlockSpec(shape, lambda b, c: (0,) * len(shape))
    return pl.pallas_call(
        _ssd_prompt_kernel,
        out_shape=(jax.ShapeDtypeStruct((bsz * t, di), BF16),
                   jax.ShapeDtypeStruct((bsz, SSD_CONV_W - 1, SSD_CONV_DIM), F32),
                   jax.ShapeDtypeStruct((bsz, di, SSD_N), F32)),
        grid=(bsz, nc),
        in_specs=[pl.BlockSpec((L, di), lambda b, c: (b * nc + c, 0)),
                  pl.BlockSpec((L, di), lambda b, c: (b * nc + c, 1)),
                  pl.BlockSpec((L, SSD_BC), lambda b, c: (b * nc + c, 2 * di // SSD_BC)),
                  pl.BlockSpec((L, LANES), lambda b, c: (b * nc + c, (2 * di + SSD_BC) // LANES)),
                  full((SSD_CONV_W, di)), full((1, di)), full((SSD_CONV_W, SSD_BC)), full((1, SSD_BC)),
                  full((1, LANES)), full((1, LANES)), full((1, di)), full((1, di)),
                  full((LANES, di)), full((L, L))],
        out_specs=(pl.BlockSpec((L, di), lambda b, c: (b * nc + c, 0)),
                   pl.BlockSpec((1, SSD_CONV_W - 1, SSD_CONV_DIM), lambda b, c: (b, 0, 0)),
                   pl.BlockSpec((1, di, SSD_N), lambda b, c: (b, 0, 0))),
        scratch_shapes=[pltpu.VMEM((8 + L, di), F32), pltpu.VMEM((8 + L, SSD_BC), F32),
                        pltpu.VMEM((SSD_N, di), F32)],
        compiler_params=_cparams(("parallel", "arbitrary")),
        name="ssd_prompt",
    )(proj, proj, proj, proj, cw[:, :di], cb[:, :di], cw[:, di:], cb[:, di:], dtb, alog, dskip, nw,
      _head_expand_matrix(), tri)


def _ssd_sample_kernel(p_ref, cs_ref, st_ref, cw_ref, cb_ref, dtb_ref, alog_ref, dskip_ref, nw_ref, exp_ref, *rest):
    y_ref, conv_ref, ssm_ref = rest[-3:]
    di = SSD_D_INNER
    z = p_ref[0, :, 0:di]
    xbc = p_ref[0, :, di:di + SSD_CONV_DIM]
    dtr = p_ref[0, :, di + SSD_CONV_DIM:di + SSD_CONV_DIM + LANES]
    cs = cs_ref[0]
    acc = cb_ref[...] + xbc * cw_ref[SSD_CONV_W - 1:SSD_CONV_W, :]
    for k in range(SSD_CONV_W - 1):
        acc = acc + cs[k:k + 1, :] * cw_ref[k:k + 1, :]
    conv_ref[0, 0:SSD_CONV_W - 2, :] = cs[1:SSD_CONV_W - 1, :]
    conv_ref[0, SSD_CONV_W - 2:SSD_CONV_W - 1, :] = xbc
    act = _silu(acc)
    x = act[:, 0:di]
    lane = lax.broadcasted_iota(jnp.int32, (1, LANES), 1)
    head_ok = lane < SSD_HEADS
    dt = jnp.where(head_ok, _softplus(dtr + dtb_ref[...]), 0.0)
    a = jnp.where(head_ok, -jnp.exp(alog_ref[...]), 0.0)
    da = jnp.exp(dt * a)
    lhs = jnp.concatenate([dt, da, jnp.zeros((6, LANES), F32)], axis=0)
    ex = _dot_x01(lhs, exp_ref[...])
    dt_e, da_e = ex[0:1, :], ex[1:2, :]
    xdt = x * dt_e

    eye = (lax.broadcasted_iota(jnp.int32, (LANES, LANES), 0)
           == lax.broadcasted_iota(jnp.int32, (LANES, LANES), 1))

    def to_col(rowvec):
        return jnp.sum(jnp.where(eye, jnp.broadcast_to(rowvec, (LANES, LANES)), 0.0), axis=1, keepdims=True)

    def to_row(colvec):
        return jnp.sum(jnp.where(eye, jnp.broadcast_to(colvec, (LANES, LANES)), 0.0), axis=0, keepdims=True)

    y_off = []
    cbs = []
    for g in range(SSD_GROUPS):
        b_g = act[:, di + g * SSD_N:di + (g + 1) * SSD_N]
        c_g = act[:, di + (SSD_GROUPS + g) * SSD_N:di + (SSD_GROUPS + g + 1) * SSD_N]
        cbs.append(jnp.broadcast_to(jnp.sum(b_g * c_g, axis=1, keepdims=True), (1, SSD_REP * SSD_P)))
        for i in range(SSD_REP * SSD_P // LANES):
            lo = g * SSD_REP * SSD_P + i * LANES
            st = st_ref[0, 0, lo:lo + LANES, :]
            xcol = to_col(xdt[:, lo:lo + LANES])
            dcol = to_col(da_e[:, lo:lo + LANES])
            ssm_ref[0, 0, lo:lo + LANES, :] = st * dcol + xcol * b_g
            y_off.append(to_row(jnp.sum(st * c_g, axis=1, keepdims=True)))
    y = xdt * jnp.concatenate(cbs, axis=1) + jnp.concatenate(y_off, axis=1) * da_e + dskip_ref[...] * x
    y = y * _silu(z)
    r = lax.rsqrt(jnp.mean(y * y, axis=-1, keepdims=True) + EPS)
    y_ref[0] = ((y * r) * nw_ref[...]).astype(BF16)


def _ssd_sample(proj, conv_state, ssm_states, layer, ssm_out_prev, cw, cb, dtb, alog, dskip, nw):
    n_layers, bsz = ssm_states.shape[:2]
    di = SSD_D_INNER
    full = lambda shape: pl.BlockSpec(shape, lambda b: (0,) * len(shape))
    per_b = lambda shape: pl.BlockSpec((1,) + shape, lambda b: (b,) + (0,) * len(shape))
    state_spec = pl.BlockSpec((1, 1, di, SSD_N), lambda b: (layer, b, 0, 0))
    in_specs = [per_b((1, SSD_NP)), per_b((SSD_CONV_W - 1, SSD_CONV_DIM)), state_spec,
                full((SSD_CONV_W, SSD_CONV_DIM)), full((1, SSD_CONV_DIM)), full((1, LANES)), full((1, LANES)),
                full((1, di)), full((1, di)), full((LANES, di))]
    args = [proj.reshape(bsz, 1, SSD_NP), conv_state, ssm_states.reshape(n_layers, bsz, di, SSD_N), cw, cb, dtb, alog,
            dskip, nw, _head_expand_matrix()]
    aliases = {}
    if ssm_out_prev is not None:
        in_specs.append(pl.BlockSpec(memory_space=pl.ANY))
        aliases = {len(args): 2}
        args.append(ssm_out_prev)
    y, conv, ssm = pl.pallas_call(
        _ssd_sample_kernel,
        out_shape=(jax.ShapeDtypeStruct((bsz, 1, di), BF16),
                   jax.ShapeDtypeStruct((bsz, SSD_CONV_W - 1, SSD_CONV_DIM), F32),
                   jax.ShapeDtypeStruct((n_layers, bsz, di, SSD_N), F32)),
        grid=(bsz,),
        in_specs=in_specs,
        out_specs=(per_b((1, di)), per_b((SSD_CONV_W - 1, SSD_CONV_DIM)), state_spec),
        input_output_aliases=aliases,
        compiler_params=_cparams(("parallel",)),
        name="ssd_sample",
    )(*args)
    return y.reshape(bsz, di), conv, ssm


CH = CMP_STRIDE
N_L = CMP_BLOCK // 2
PE_ROWS = 16


def _compress_accumulate(lhs_fn, pe_ref, wab_ref, kv):
    acc = None
    for l in range(N_L):
        lhs = jnp.concatenate([lhs_fn(l, kv).astype(BF16), pe_ref[kv, l]], axis=0)
        t = _dot(lhs, wab_ref[kv, l])
        acc = t if acc is None else acc + t
    return acc


def _compress_finish(acc, w2, nch):
    p = acc[0:nch, 0:GW]
    q_next = pltpu.roll(acc[0:nch, GW:2 * GW], nch - 1, axis=0)
    bias = acc[nch:nch + 1, 0:GW] + acc[nch + 8:nch + 9, GW:2 * GW]
    out = _dot(_silu(p + q_next + bias).astype(BF16), w2)
    row = lax.broadcasted_iota(jnp.int32, (nch, 1), 0)
    return jnp.where(row < nch - 1, out, 0.0)


def _softmax_masked(s, ok, axis):
    s = jnp.where(ok, s, NEG)
    m = jnp.max(s, axis=axis, keepdims=True)
    p = jnp.where(ok, jnp.exp(s - m), 0.0)
    d = jnp.sum(p, axis=axis, keepdims=True)
    return p * jnp.where(d > 0.0, 1.0 / d, 0.0)


def _interleave(n, scores, softmax, values, finish):
    out, pending = [], None
    s_next = scores(0)
    for k in range(n):
        s = s_next
        if k + 1 < n:
            s_next = scores(k + 1)
        p, aux = softmax(k, s)
        if pending is not None:
            out.append(finish(*pending))
        pending = (k, values(k, p), aux)
    out.append(finish(*pending))
    return out


def _topk_mask(score, j, width, shifts, axis):
    n = score.shape[axis]
    rank = jnp.zeros(score.shape, F32)
    for k in shifts:
        other = pltpu.roll(score, k, axis=axis)
        lower = j >= k
        if width != n:
            other = jnp.where(lower, other, pltpu.roll(score, n - width + k, axis=axis))
        rank = rank + jnp.where(lower, jnp.where(other >= score, 1.0, 0.0), jnp.where(other > score, 1.0, 0.0))
    return rank < SEL_TOPK


def _overlap(n_cmp, n_sel):
    c_start = np.arange(n_cmp) * CMP_STRIDE
    s_start = np.arange(n_sel) * SEL_BLOCK
    return ((c_start[:, None] < s_start[None, :] + SEL_BLOCK)
            & (c_start[:, None] + CMP_BLOCK > s_start[None, :])).astype(np.float32)


def _block_bias_matrix(width, n_keys):
    l = np.arange(LANES)[:, None] % width
    blk = np.arange(n_keys)[None, :] // SEL_BLOCK
    return np.where(l == blk, NEG_MXU, 0.0).astype(np.float32)


def _nsa_proj_kernel(x_ref, nw_ref, sh_ref, sc_ref, wr_ref, wt_ref,
                     rows_ref, kpad_ref, qt_ref, gt_ref, vt_ref, cmp_ref, sel_ref, win_ref):
    h = _modulated_norm(x_ref[...], nw_ref[...], sh_ref[0], sc_ref[0]).astype(BF16)
    y = _dot(h, wr_ref[...])
    rows_ref[...] = y[:, 0:KVW].astype(BF16)
    kpad_ref[...] = y[:, KVW:].astype(BF16)
    yt = _dot_nt(wt_ref[...], h)
    qt_ref[0] = (yt[0:1024] * (HD ** -0.5)).astype(BF16)
    cmp_ref[0] = yt[1024:1024 + KVW]
    sel_ref[0] = yt[1024 + KVW:1024 + 2 * KVW]
    vt_ref[0, 0:GW] = yt[1024 + KVW + GW:1024 + 2 * KVW].astype(BF16)
    vt_ref[0, GW:2 * GW] = yt[1024 + 2 * KVW + GW:1024 + 3 * KVW].astype(BF16)
    gt_ref[0] = _sigmoid(yt[1024 + 3 * KVW:NSA_N])

    @pl.when(pl.program_id(1) == pl.num_programs(1) - 1)
    def _():
        win_ref[0] = yt[1024 + 2 * KVW:1024 + 3 * KVW]


def _nsa_proj(x, nw, shift, scale, wr, wt, bsz, t):
    d = x.shape[1]
    tm = min(WINDOW, t)
    nb = t // tm
    row = lambda w: pl.BlockSpec((tm, w), lambda b, i: (b * nb + i, 0))
    col = lambda h: pl.BlockSpec((1, h, tm), lambda b, i: (b, 0, i))
    mod = lambda: pl.BlockSpec((1, 1, d), lambda b, i: (b, 0, 0))
    nq = NSA_HEADS * HD
    return pl.pallas_call(
        _nsa_proj_kernel,
        out_shape=(jax.ShapeDtypeStruct((bsz * t, KVW), BF16),
                   jax.ShapeDtypeStruct((bsz * t, 2 * KPAD), BF16),
                   jax.ShapeDtypeStruct((bsz, nq, t), BF16),
                   jax.ShapeDtypeStruct((bsz, LANES, t), F32),
                   jax.ShapeDtypeStruct((bsz, 2 * GW, t), BF16),
                   jax.ShapeDtypeStruct((bsz, KVW, t), F32),
                   jax.ShapeDtypeStruct((bsz, KVW, t), F32),
                   jax.ShapeDtypeStruct((bsz, KVW, tm), F32)),
        grid=(bsz, nb),
        in_specs=[row(d), pl.BlockSpec((1, d), lambda b, i: (0, 0)), mod(), mod(),
                  pl.BlockSpec((d, KVW + 2 * KPAD), lambda b, i: (0, 0)),
                  pl.BlockSpec((NSA_N, d), lambda b, i: (0, 0))],
        out_specs=(row(KVW), row(2 * KPAD), col(nq), col(LANES), col(2 * GW), col(KVW), col(KVW),
                   pl.BlockSpec((1, KVW, tm), lambda b, i: (b, 0, 0))),
        compiler_params=_cparams(("parallel", "arbitrary")),
        name="nsa_proj",
    )(x, nw, shift, scale, wr, wt)


def _cmp_prompt_kernel(rows_ref, pe_ref, wab_ref, w2k_ref, w2v_ref, kc_ref, vc_ref):
    nch = rows_ref.shape[1]
    lhs = lambda l, kv: rows_ref[0, :, l * KVW + kv * GW:l * KVW + (kv + 1) * GW]
    kc_ref[0] = _compress_finish(_compress_accumulate(lhs, pe_ref, wab_ref, 0), w2k_ref[...], nch).astype(BF16)
    vc_ref[0] = _compress_finish(_compress_accumulate(lhs, pe_ref, wab_ref, 1), w2v_ref[...], nch).T.astype(BF16)


def _cmp_prompt(rows, pe_t, wab, w2k_pad, w2v):
    bsz, nch, _ = rows.shape
    full = lambda shape: pl.BlockSpec(shape, lambda b: (0,) * len(shape))
    return pl.pallas_call(
        _cmp_prompt_kernel,
        out_shape=(jax.ShapeDtypeStruct((bsz, nch, KPAD), BF16), jax.ShapeDtypeStruct((bsz, GW, nch), BF16)),
        grid=(bsz,),
        in_specs=[pl.BlockSpec((1, nch, CH * KVW), lambda b: (b, 0, 0)),
                  full(pe_t.shape), full(wab.shape), full(w2k_pad.shape), full(w2v.shape)],
        out_specs=(pl.BlockSpec((1, nch, KPAD), lambda b: (b, 0, 0)), pl.BlockSpec((1, GW, nch), lambda b: (b, 0, 0))),
        compiler_params=_cparams(("parallel",)),
        name="nsa_compress_prompt",
    )(rows, pe_t, wab, w2k_pad, w2v)


def _nsa_prompt_kernel(qt_ref, gt_ref, kc_ref, vct_ref, kpad_ref, vt_ref, ovt_ref, nege_ref, o_ref,
                       *, tq, n_sel, ck):
    i = pl.program_id(1)
    t0 = i * tq
    nq = REP * tq
    tl = t0 + lax.broadcasted_iota(jnp.int32, (1, tq), 1)
    zq = jnp.zeros((HD, nq), BF16)

    def lanes4(a):
        return jnp.concatenate([a] * REP, axis=1)

    tl4 = lanes4(tl)
    qg = [jnp.concatenate([qt_ref[0, (g * REP + r) * HD:(g * REP + r + 1) * HD, :] for r in range(REP)], axis=1)
          for g in range(KVH)]
    q_rhs = [jnp.concatenate([q, zq], axis=0) for q in qg]

    ncp = kc_ref.shape[1]
    cend = lax.broadcasted_iota(jnp.int32, (ncp, 1), 0) * CMP_STRIDE + (CMP_BLOCK - 1)
    ok_cmp = cend <= tl4
    s_cmp = [_dot(kc_ref[0, :, g * LANES:(g + 1) * LANES], q_rhs[g]) for g in range(KVH)]
    p_cmp = [_softmax_masked(s, ok_cmp, 0) for s in s_cmp]
    o_cmp = [_dot(vct_ref[0, g * HD:(g + 1) * HD, :], p_cmp[g].astype(BF16)) for g in range(KVH)]
    imp = jnp.zeros((LANES, tq), F32)
    for g, p in enumerate(p_cmp):
        psum = p[:, 0:tq] + p[:, tq:2 * tq] + p[:, 2 * tq:3 * tq] + p[:, 3 * tq:4 * tq]
        imp = imp + _dot_01x(ovt_ref[g], psum)

    row = lax.broadcasted_iota(jnp.int32, (LANES, tq), 0)
    jj = row % n_sel
    cur = tl // SEL_BLOCK
    valid = jj * SEL_BLOCK <= tl
    forced = (jj == 0) | (jj == cur) | (jj == cur - 1)
    score = jnp.where(valid, jnp.where(forced, BIG, imp), NEG)
    top = _topk_mask(score, jj, n_sel, range(1, n_sel), 0)
    blocked = jnp.where(top & valid, 0.0, 1.0)

    n_full = t0 // ck
    sel_rhs = [jnp.concatenate([lanes4(jnp.where(row // n_sel == g, blocked, 0.0).astype(BF16)), q_rhs[g]], axis=0)
               for g in range(KVH)]

    def sel_chunk(c, carries, causal):
        k0 = pl.multiple_of(c * ck, ck)
        bias_rows = nege_ref[pl.ds(k0, ck), :]
        if causal:
            cbias = jnp.where(k0 + lax.broadcasted_iota(jnp.int32, (ck, 1), 0) <= tl4, 0.0, NEG)

        def scores(g):
            lhs = jnp.concatenate([bias_rows, kpad_ref[0, pl.ds(k0, ck), g * LANES:(g + 1) * LANES]], axis=1)
            s = _dot(lhs, sel_rhs[g])
            return s + cbias if causal else s

        def softmax(g, s):
            m, l, acc = carries[g]
            m_new = jnp.maximum(m, jnp.max(s, axis=0, keepdims=True))
            alpha = jnp.exp(m - m_new)
            p = jnp.exp(s - m_new)
            return p.astype(BF16), (m_new, alpha * l + jnp.sum(p, axis=0, keepdims=True), alpha, acc)

        def values(g, p):
            return _dot(vt_ref[0, g * HD:(g + 1) * HD, pl.ds(k0, ck)], p)

        def finish(g, pv, aux):
            m_new, l_new, alpha, acc = aux
            return m_new, l_new, alpha * acc + pv

        return tuple(_interleave(KVH, scores, softmax, values, finish))

    init = (jnp.full((1, nq), NEG, F32), jnp.zeros((1, nq), F32), jnp.zeros((HD, nq), F32))
    carries = lax.fori_loop(0, n_full, lambda c, cr: sel_chunk(c, cr, False), (init,) * KVH)
    o_sel = [acc * (1.0 / l) for _, l, acc in sel_chunk(n_full, carries, True)]

    kw = WINDOW + tq
    k_start = pl.multiple_of(jnp.maximum(t0 - WINDOW, 0), LANES)
    dpos = tl - (k_start + lax.broadcasted_iota(jnp.int32, (kw, 1), 0))
    wbias = lanes4(jnp.where((dpos >= 0) & (dpos <= WINDOW), 0.0, NEG))

    def win_scores(g):
        return _dot(kpad_ref[0, pl.ds(k_start, kw), (KVH + g) * LANES:(KVH + g + 1) * LANES], q_rhs[g]) + wbias

    def win_softmax(g, s):
        p = jnp.exp(s - jnp.max(s, axis=0, keepdims=True))
        return p.astype(BF16), jnp.sum(p, axis=0, keepdims=True)

    def win_values(g, p):
        return _dot(vt_ref[0, GW + g * HD:GW + (g + 1) * HD, pl.ds(k_start, kw)], p)

    o_win = _interleave(KVH, win_scores, win_softmax, win_values, lambda g, pv, l: pv * (1.0 / l))

    gt = gt_ref[0]
    outs = []
    for g in range(KVH):
        for r in range(REP):
            h = g * REP + r
            sl = slice(r * tq, (r + 1) * tq)
            outs.append(gt[3 * h:3 * h + 1, :] * o_cmp[g][:, sl] + gt[3 * h + 1:3 * h + 2, :] * o_sel[g][:, sl]
                        + gt[3 * h + 2:3 * h + 3, :] * o_win[g][:, sl])
    o_ref[0] = jnp.concatenate(outs, axis=0).T.astype(BF16)


def _nsa_prompt_attention(qt, gt, kc, vct, kpad, vt):
    bsz, nd, t = qt.shape
    tq = LANES
    ck = 4 * LANES
    n_sel = t // SEL_BLOCK
    ncp = t // CH
    assert KVH * n_sel == LANES and ncp == LANES and t % ck == 0
    ovt = np.zeros((KVH, LANES, ncp), np.float32)
    for g in range(KVH):
        ovt[g, g * n_sel:(g + 1) * n_sel, :ncp - 1] = _overlap(ncp - 1, n_sel).T
    nege_t = _block_bias_matrix(n_sel, t).T
    per_b = lambda shape: pl.BlockSpec((1,) + shape, lambda b, i: (b,) + (0,) * len(shape))
    full = lambda shape: pl.BlockSpec(shape, lambda b, i: (0,) * len(shape))
    out = pl.pallas_call(
        functools.partial(_nsa_prompt_kernel, tq=tq, n_sel=n_sel, ck=ck),
        out_shape=jax.ShapeDtypeStruct((bsz, t, nd), BF16),
        grid=(bsz, t // tq),
        in_specs=[pl.BlockSpec((1, nd, tq), lambda b, i: (b, 0, i)),
                  pl.BlockSpec((1, LANES, tq), lambda b, i: (b, 0, i)),
                  per_b((ncp, KPAD)), per_b((GW, ncp)), per_b((t, 2 * KPAD)), per_b((2 * GW, t)),
                  full((KVH, LANES, ncp)), full((t, LANES))],
        out_specs=pl.BlockSpec((1, tq, nd), lambda b, i: (b, i, 0)),
        compiler_params=_cparams(("parallel", "arbitrary")),
        name="nsa_prompt_attention",
    )(qt, gt, kc, vct, kpad.reshape(bsz, t, 2 * KPAD), vt, jnp.asarray(ovt, BF16), jnp.asarray(nege_t, BF16))
    return out.reshape(bsz * t, nd)


def _nsa_sample_kernel(pt_ref, p_ref, *rest, n_pages, past):
    del pt_ref
    cmp_pages = rest[:n_pages]
    sel_pages = rest[n_pages:2 * n_pages]
    win_ref, pe_ref, wab_ref, w2_ref, ov_ref, nege_ref, perm_ref = rest[2 * n_pages:2 * n_pages + 7]
    o_ref, wout_ref, xr_ref = rest[-3:]
    nch = n_pages * (PAGE_SIZE // CH)
    cpp = PAGE_SIZE // CH
    t = past
    keep = win_ref.shape[-1]
    rows_q = 8

    perm = perm_ref[...]
    for p, pg in enumerate(cmp_pages):
        for kv in range(2):
            for j in range(KVH // 2):
                tile = pg[0, 0, kv, 2 * j:2 * j + 2].reshape(2 * HD, PAGE_SIZE).astype(BF16)
                rows = _dot_nt(perm, tile)
                for l in range(N_L):
                    xr_ref[kv, j, l, p * cpp:(p + 1) * cpp, :] = rows[l * cpp:(l + 1) * cpp, :]

    cmp_t = []
    for kv in range(2):
        acc = _compress_accumulate(
            lambda l, kv: jnp.concatenate([xr_ref[kv, j, l] for j in range(KVH // 2)], axis=1), pe_ref, wab_ref, kv)
        cmp_t.append(_compress_finish(acc, w2_ref[kv], nch).T.astype(BF16))
    kct, vct = cmp_t

    lane = lax.broadcasted_iota(jnp.int32, (rows_q, LANES), 1)
    rowi = lax.broadcasted_iota(jnp.int32, (rows_q, LANES), 0)
    cend = lax.broadcasted_iota(jnp.int32, (1, nch), 1) * CMP_STRIDE + (CMP_BLOCK - 1)
    n_sel = -(-(t + 1) // SEL_BLOCK)
    nj = max(n_sel, SEL_TOPK)
    cur = t // SEL_BLOCK
    valid = (lane < n_sel) & (lane * SEL_BLOCK <= t)
    forced = (lane == 0) | (lane == cur) | (lane == cur - 1)
    shifts = list(range(1, nj)) + list(range(LANES - nj + 1, LANES))
    gates = jnp.broadcast_to(_sigmoid(p_ref[0, :, 1024 + 3 * KVW:NSA_N]), (rows_q, LANES))
    eye = (lax.broadcasted_iota(jnp.int32, (HD, HD), 0) == lax.broadcasted_iota(jnp.int32, (HD, HD), 1))
    last_lane = lax.broadcasted_iota(jnp.int32, (HD, keep), 1) == keep - 1

    G = range(KVH)
    new_row = lambda off, g: p_ref[0, :, off + g * HD:off + (g + 1) * HD]
    q = [jnp.concatenate([p_ref[0, :, (g * REP + r) * HD:(g * REP + r + 1) * HD] for r in range(REP)]
                         + [jnp.zeros((rows_q - REP, HD), F32)], axis=0) * (HD ** -0.5) for g in G]
    qb = [x.astype(BF16) for x in q]

    def softmax_with_new(s, s_new, ok_new):
        s_new = jnp.where(ok_new, s_new, NEG)
        m = jnp.maximum(jnp.max(s, axis=1, keepdims=True), s_new)
        pr = jnp.exp(s - m)
        pr_new = jnp.where(ok_new, jnp.exp(s_new - m), 0.0)
        return pr.astype(BF16), pr_new, 1.0 / (jnp.sum(pr, axis=1, keepdims=True) + pr_new)

    s_cmp = [_dot(qb[g], kct[g * HD:(g + 1) * HD, :]) for g in G]
    p_cmp = [_softmax_masked(s, cend <= t, 1) for s in s_cmp]
    o_cmp = [_dot_nt(p_cmp[g].astype(BF16), vct[g * HD:(g + 1) * HD, :]) for g in G]
    allowed = []
    for g in G:
        psum = jnp.broadcast_to(jnp.sum(p_cmp[g][0:REP], axis=0, keepdims=True), (rows_q, nch))
        imp = _dot_x01(psum, ov_ref[...])
        score = jnp.where(valid, jnp.where(forced, BIG, imp), NEG)
        score = jnp.where(lane < nj, score, -3e38)
        allowed.append(_topk_mask(score, lane, LANES, shifts, 1) & valid)

    nege = nege_ref[...]
    kt = [jnp.concatenate([pg[0, 0, 0, g] for pg in sel_pages], axis=1).astype(BF16) for g in G]
    vt = [jnp.concatenate([pg[0, 0, 1, g] for pg in sel_pages], axis=1).astype(BF16) for g in G]
    s_sel = [_dot(jnp.concatenate([jnp.where(allowed[g], 0.0, 1.0).astype(BF16), qb[g]], axis=1),
                  jnp.concatenate([nege, kt[g]], axis=0)) for g in G]
    v_new = [new_row(1024 + KVW + GW, g) for g in G]
    sm = [softmax_with_new(s_sel[g], jnp.sum(q[g] * new_row(1024 + KVW, g), axis=1, keepdims=True),
                           jnp.sum(jnp.where((lane == cur) & allowed[g], 1.0, 0.0), axis=1, keepdims=True) > 0.5)
          for g in G]
    pv = [_dot_nt(sm[g][0], vt[g]) for g in G]
    o_sel = [(pv[g] + sm[g][1] * v_new[g]) * sm[g][2] for g in G]

    kw = [win_ref[0, 0, 0, g] for g in G]
    vw = [win_ref[0, 0, 1, g] for g in G]
    kw_new = [new_row(1024 + 2 * KVW, g) for g in G]
    vw_new = [new_row(1024 + 2 * KVW + GW, g) for g in G]
    s_win = [_dot(qb[g], kw[g].astype(BF16)) for g in G]
    sm = [softmax_with_new(s_win[g], jnp.sum(q[g] * kw_new[g], axis=1, keepdims=True), jnp.full((rows_q, 1), True))
          for g in G]
    pv = [_dot_nt(sm[g][0], vw[g].astype(BF16)) for g in G]
    o_win = [(pv[g] + sm[g][1] * vw_new[g]) * sm[g][2] for g in G]
    for g in G:
        for kv, old, new in ((0, kw[g], kw_new[g]), (1, vw[g], vw_new[g])):
            col = jnp.sum(jnp.where(eye, jnp.broadcast_to(new, (HD, HD)), 0.0), axis=1, keepdims=True)
            wout_ref[0, 0, kv, g] = jnp.where(last_lane, col, pltpu.roll(old, keep - 1, axis=1))

    for g in G:
        o = None
        for br, ob in enumerate((o_cmp[g], o_sel[g], o_win[g])):
            gcol = jnp.sum(jnp.where(lane == (g * REP + rowi) * 3 + br, gates, 0.0), axis=1, keepdims=True)
            o = gcol * ob if o is None else o + gcol * ob
        for r in range(REP):
            h = g * REP + r
            o_ref[0, :, h * HD:(h + 1) * HD] = o[r:r + 1, :].astype(BF16)


def _nsa_sample(proj, cmp_view, sel_view, win_view, layer, win_out_prev, page_table, pe_t, wab, w2bd):
    n_layers = win_view.shape[0]
    bsz, n_pages = page_table.shape
    past = n_pages * PAGE_SIZE
    keep = win_view.shape[-1]
    nch = past // CH
    assert nch == LANES and keep <= WINDOW and past % SEL_BLOCK == 0 and past - keep >= 0
    n_sel = -(-(past + 1) // SEL_BLOCK)
    ov = np.zeros((nch, LANES), np.float32)
    ov[:nch - 1, :n_sel] = _overlap(nch - 1, n_sel)
    full = lambda shape: pl.BlockSpec(shape, lambda b, pt: (0,) * len(shape))
    once = lambda shape: pl.BlockSpec(shape, lambda b, pt: (0,) * len(shape), pipeline_mode=pl.Buffered(1))
    per_b = lambda shape: pl.BlockSpec((1,) + shape, lambda b, pt: (b,) + (0,) * len(shape))
    page_shape = (2, KVH, HD, PAGE_SIZE)
    page = lambda p: pl.BlockSpec((1, 1) + page_shape, lambda b, pt: (layer, pt[b * n_pages + p], 0, 0, 0, 0))
    win_shape = (2, KVH, HD, keep)
    win_spec = pl.BlockSpec((1, 1) + win_shape, lambda b, pt: (layer, b, 0, 0, 0, 0))
    in_specs = ([per_b((1, NSA_N))] + [page(p) for p in range(n_pages)] * 2
                + [win_spec, full(pe_t.shape), once(wab.shape), full(w2bd.shape), full((nch, LANES)),
                   full((LANES, past)), full((PAGE_SIZE, PAGE_SIZE))])
    cpp = PAGE_SIZE // CH
    perm = np.zeros((PAGE_SIZE, PAGE_SIZE), np.float32)
    for l in range(N_L):
        for c in range(cpp):
            perm[l * cpp + c, c * CH + l] = 1.0
    args = [page_table.reshape(-1), proj.reshape(bsz, 1, NSA_N)] + [cmp_view] * n_pages + [sel_view] * n_pages + [
        win_view, pe_t, wab, w2bd, jnp.asarray(ov, BF16), jnp.asarray(_block_bias_matrix(LANES, past), BF16),
        jnp.asarray(perm, BF16)]
    aliases = {}
    if win_out_prev is not None:
        in_specs.append(pl.BlockSpec(memory_space=pl.ANY))
        aliases = {len(args): 1}
        args.append(win_out_prev)
    o, wout = pl.pallas_call(
        functools.partial(_nsa_sample_kernel, n_pages=n_pages, past=past),
        out_shape=(jax.ShapeDtypeStruct((bsz, 1, NSA_HEADS * HD), BF16),
                   jax.ShapeDtypeStruct((n_layers, bsz) + win_shape, F32)),
        grid_spec=pltpu.PrefetchScalarGridSpec(
            num_scalar_prefetch=1, grid=(bsz,), in_specs=in_specs,
            out_specs=(per_b((1, NSA_HEADS * HD)), win_spec),
            scratch_shapes=[pltpu.VMEM((2, KVH // 2, N_L, nch, 2 * HD), F32)]),
        input_output_aliases=aliases,
        compiler_params=_cparams(("arbitrary",)),
        name="nsa_sample",
    )(*args)
    return o.reshape(bsz, NSA_HEADS * HD), wout


def _prep_nsa(w_in, w_out, pe, w1, w2):
    n_gate = 3 * NSA_HEADS
    w_full = jnp.concatenate([w_in, jnp.zeros((D_MODEL, LANES - n_gate), F32)], axis=1)

    def k_padded(lo):
        k = w_in[:, lo:lo + GW].reshape(D_MODEL, KVH, HD)
        return jnp.concatenate([k, jnp.zeros_like(k)], axis=2).reshape(D_MODEL, KPAD)

    w_rows = jnp.concatenate([w_in[:, 1024:1024 + KVW], k_padded(1024 + KVW), k_padded(1024 + 2 * KVW)], axis=1)
    eye = jnp.eye(KVH, dtype=F32)
    w1r = w1.reshape(2, 2, N_L, HD, HD)
    wab = jnp.einsum('khlde,gf->klgdhfe', w1r, eye).reshape(2, N_L, GW, 2 * GW)
    w2bd = jnp.einsum('kde,gf->kgdfe', w2, eye)
    w2k_pad = jnp.concatenate([w2bd[0], jnp.zeros_like(w2bd[0])], axis=3).reshape(GW, KPAD)
    per = pe.reshape(2, 2, N_L, 1, 1, HD)
    pe_t = jnp.broadcast_to(per, (2, 2, N_L, 8, KVH, HD)).transpose(0, 2, 1, 3, 4, 5).reshape(2, N_L, PE_ROWS, GW)
    b = lambda a: a.astype(BF16)
    return dict(w_sample=b(w_full), w_rows=b(w_rows), w_t=b(w_full.T), wo=b(w_out), pe_t=b(pe_t), wab=b(wab),
                w2bd=b(w2bd.reshape(2, GW, GW)), w2k_pad=b(w2k_pad))


def _nsa_prompt_layer(x, nw, shift, scale, prep, bsz, t):
    rows, kpad, qt, gt, vt, cmp_t, sel_t, win_t = _nsa_proj(x, nw, shift, scale, prep["w_rows"], prep["w_t"], bsz, t)
    kc, vct = _cmp_prompt(rows.reshape(bsz, t // CH, CH * KVW), prep["pe_t"], prep["wab"], prep["w2k_pad"],
                          prep["w2bd"][1])
    return _nsa_prompt_attention(qt, gt, kc, vct, kpad, vt), (cmp_t, sel_t, win_t)


def _prep_ssd(w_in, dt_bias, a_log, d_skip):
    pad = SSD_NP - w_in.shape[1]
    w = jnp.concatenate([w_in, jnp.zeros((D_MODEL, pad), F32)], axis=1).astype(BF16)
    pad_h = lambda v: jnp.concatenate([v, jnp.zeros((LANES - SSD_HEADS,), F32)]).reshape(1, LANES)
    return w, pad_h(dt_bias), pad_h(a_log), jnp.repeat(d_skip, SSD_P).reshape(1, SSD_D_INNER)


def kernel(x_prompt, x_sample, cache_kv_cmp, cache_kv_sel, cache_kv_win, state_ssm, state_conv, page_table, c_prompt, c_sample, ada_w, ada_b, norm_w, mlp_w1, mlp_w2, nsa_w_in, nsa_w_out, nsa_cmp_pe, nsa_cmp_w1, nsa_cmp_w2, ssd_w_in, ssd_conv_w, ssd_conv_b, ssd_dt_bias, ssd_a_log, ssd_d, ssd_norm_w, ssd_w_out, final_norm_w):
    bp, t, d = x_prompt.shape
    bs = x_sample.shape[0]
    xp = x_prompt.reshape(bp * t, d)
    xs = x_sample.reshape(bs, d)
    mods = _adaln(jnp.concatenate([c_prompt, c_sample], axis=0), ada_w, ada_b)
    w1b = mlp_w1.astype(BF16)
    w2b = mlp_w2.astype(BF16)
    fnw = final_norm_w.reshape(1, d)
    tm = 512
    n_nsa, n_pool = cache_kv_cmp.shape[:2]
    cmp_view = jnp.transpose(cache_kv_cmp, (0, 1, 3, 4, 5, 2))
    sel_view = jnp.transpose(cache_kv_sel, (0, 1, 3, 4, 5, 2))
    win_view = jnp.transpose(cache_kv_win, (0, 1, 3, 4, 5, 2))
    win_s = ssm_s = None
    outs = {k: [] for k in ("cmp_p", "cmp_s", "sel_p", "sel_s", "win_p", "win_s", "ssm_p", "ssm_s", "conv_p", "conv_s")}
    kv_rows = lambda a, n: a.reshape(n + (2, KVH, HD))
    kv_cols = lambda a: a.reshape(a.shape[0], 2, KVH, HD, a.shape[-1])

    for i in range(DEPTH):
        jl = i // 2
        mp = [mods[i, :bp, k * d:(k + 1) * d].reshape(bp, 1, d) for k in range(6)]
        ms = [mods[i, bp:, k * d:(k + 1) * d].reshape(1, bs, d) for k in range(6)]
        nw0 = norm_w[i, 0].reshape(1, d)
        nw1 = norm_w[i, 1].reshape(1, d)
        if i % 2 == 0:
            prep = _prep_nsa(nsa_w_in[jl], nsa_w_out[jl], nsa_cmp_pe[jl], nsa_cmp_w1[jl], nsa_cmp_w2[jl])
            wo = prep["wo"]
            ap, (cmp_t, sel_t, win_t) = _nsa_prompt_layer(xp, nw0, mp[0], mp[1], prep, bp, t)
            ps = _mod_matmul(xs, nw0, ms[0], ms[1], prep["w_sample"], tm=bs, tn=896, rows_per_mod=None)
            as_, win_s = _nsa_sample(ps, cmp_view, sel_view, win_view, jl, win_s, page_table, prep["pe_t"],
                                     prep["wab"], prep["w2bd"])
            outs["cmp_p"].append(kv_cols(cmp_t))
            outs["sel_p"].append(kv_cols(sel_t))
            outs["win_p"].append(kv_cols(win_t))
            outs["cmp_s"].append(kv_rows(ps[:, 1024:1024 + KVW], (bs, 1)))
            outs["sel_s"].append(kv_rows(ps[:, 1024 + KVW:1024 + 2 * KVW], (bs, 1)))
        else:
            w, dtb, alog, dsk = _prep_ssd(ssd_w_in[jl], ssd_dt_bias[jl], ssd_a_log[jl], ssd_d[jl])
            wo = ssd_w_out[jl].astype(BF16)
            snw = ssd_norm_w[jl].reshape(1, SSD_D_INNER)
            cw = ssd_conv_w[jl]
            cb = ssd_conv_b[jl].reshape(1, SSD_CONV_DIM)
            pp = _mod_matmul(xp, nw0, mp[0], mp[1], w, tm=tm, tn=896, rows_per_mod=t)
            ps = _mod_matmul(xs, nw0, ms[0], ms[1], w, tm=bs, tn=896, rows_per_mod=None)
            ap, conv_p, ssm_p = _ssd_prompt(pp, bp, t, cw, cb, dtb, alog, dsk, snw)
            as_, conv_s, ssm_s = _ssd_sample(ps, state_conv[jl], state_ssm, jl, ssm_s, cw, cb, dtb, alog, dsk, snw)
            outs["conv_p"].append(conv_p)
            outs["conv_s"].append(conv_s)
            outs["ssm_p"].append(ssm_p.reshape(bp, SSD_HEADS, SSD_P, SSD_N))
        last = i == DEPTH - 1
        xp = _post_mlp(xp, ap, wo, mp[2], nw1, mp[3], mp[4], mp[5], w1b[i], w2b[i], fnw,
                       tm=tm, tf=512, rows_per_mod=t, final_norm=last)
        xs = _post_mlp(xs, as_, wo, ms[2], nw1, ms[3], ms[4], ms[5], w1b[i], w2b[i], fnw,
                       tm=bs, tf=512, rows_per_mod=None, final_norm=last)

    st = lambda k: jnp.stack(outs[k])
    time_major = lambda a: jnp.transpose(a, (0, 1, 5, 2, 3, 4))
    return (xp.reshape(bp, t, d), xs.reshape(bs, 1, d), time_major(st("cmp_p")), st("cmp_s"),
            time_major(st("sel_p")), st("sel_s"), time_major(st("win_p")), time_major(win_s), st("ssm_p"),
            ssm_s.reshape(state_ssm.shape), st("conv_p"), st("conv_s"))
```

```python
import functools
import math

import numpy as np
import jax
import jax.numpy as jnp
from jax import lax
from jax.experimental import pallas as pl
from jax.experimental.pallas import tpu as pltpu

F32 = jnp.float32
BF16 = jnp.bfloat16

D_MODEL = 1024
DEPTH = 4
D_FF = 4 * D_MODEL
EPS = 1e-6
PAGE_SIZE = 128
NSA_HEADS = 16
HD = 64
KVH = 4
REP = 4
KVW = 2 * KVH * HD
GW = KVH * HD
CMP_BLOCK = 32
CMP_STRIDE = 16
SEL_BLOCK = 64
SEL_TOPK = 8
WINDOW = 512
SSD_D_INNER = 2 * D_MODEL
SSD_P = 64
SSD_HEADS = SSD_D_INNER // SSD_P
SSD_GROUPS = 4
SSD_REP = SSD_HEADS // SSD_GROUPS
SSD_N = 128
SSD_CONV_W = 4
SSD_CHUNK = 256
SSD_BC = 2 * SSD_GROUPS * SSD_N
SSD_CONV_DIM = SSD_D_INNER + SSD_BC
BIG = 1e30
NEG = -1e30
NEG_MXU = -2.0 ** 100
LOG2E = math.log2(math.e)
ONES_ROWS = 16
LANES = 128
NSA_N = 1024 + 3 * KVW + LANES
KPAD = KVH * LANES
SSD_NP = SSD_D_INNER + SSD_CONV_DIM + 2 * LANES
VMEM_LIMIT = 48 * 1024 * 1024


def _cparams(sem):
    return pltpu.CompilerParams(dimension_semantics=sem, vmem_limit_bytes=VMEM_LIMIT)


def _dot(a, b):
    return jnp.dot(a, b, preferred_element_type=F32)


def _dot_nt(a, b):
    return lax.dot_general(a, b, (((1,), (1,)), ((), ())), preferred_element_type=F32)


def _split(x, n):
    parts, r = [], x
    for i in range(n):
        p = r.astype(BF16)
        parts.append(p)
        if i + 1 < n:
            r = r - p.astype(F32)
    return parts


def _dot_x01(x, m01, n=3):
    acc = None
    for p in _split(x, n):
        t = _dot(p, m01)
        acc = t if acc is None else acc + t
    return acc


def _dot_01x(m01, x, n=3):
    acc = None
    for p in _split(x, n):
        t = _dot(m01, p)
        acc = t if acc is None else acc + t
    return acc


def _sigmoid(x):
    return 0.5 + 0.5 * jnp.tanh(0.5 * x)


def _silu(x):
    return x * _sigmoid(x)


def _softplus(x):
    return jnp.maximum(x, 0.0) + jnp.log(1.0 + jnp.exp(-jnp.abs(x)))


def _modulated_norm(x, nw, shift, scale):
    r = lax.rsqrt(jnp.mean(x * x, axis=-1, keepdims=True) + EPS)
    return (x * r) * nw * (1.0 + scale) + shift


def _adaln_kernel(c_ref, w_ref, b_ref, o_ref):
    a = _silu(c_ref[...]).astype(BF16)
    o_ref[0] = _dot(a, w_ref[0].astype(BF16)) + b_ref[0]


def _adaln(c_all, ada_w, ada_b):
    m, d = c_all.shape
    n = ada_w.shape[-1]
    tn = 1536
    return pl.pallas_call(
        _adaln_kernel,
        out_shape=jax.ShapeDtypeStruct((DEPTH, m, n), F32),
        grid=(DEPTH, n // tn),
        in_specs=[pl.BlockSpec((m, d), lambda l, j: (0, 0)),
                  pl.BlockSpec((1, d, tn), lambda l, j: (l, 0, j)),
                  pl.BlockSpec((1, 1, tn), lambda l, j: (l, 0, j))],
        out_specs=pl.BlockSpec((1, m, tn), lambda l, j: (l, 0, j)),
        compiler_params=_cparams(("parallel", "parallel")),
        name="adaln",
    )(c_all, ada_w, ada_b.reshape(DEPTH, 1, n))


def _modmm_kernel(x_ref, nw_ref, sh_ref, sc_ref, w_ref, o_ref, h_ref):
    @pl.when(pl.program_id(1) == 0)
    def _():
        h_ref[...] = _modulated_norm(x_ref[...], nw_ref[...], sh_ref[0], sc_ref[0]).astype(BF16)

    o_ref[...] = _dot(h_ref[...], w_ref[...])


def _mod_rows(tm, rows_per_mod):
    if rows_per_mod is None:
        return lambda shape: pl.BlockSpec((1,) + shape[1:], lambda i, j: (0, 0, 0))
    bpb = rows_per_mod // tm
    return lambda shape: pl.BlockSpec((1,) + shape[1:], lambda i, j: (i // bpb, 0, 0))


def _mod_matmul(x, nw, shift, scale, w, *, tm, tn, rows_per_mod):
    m, d = x.shape
    n = w.shape[1]
    spec = _mod_rows(tm, rows_per_mod)
    return pl.pallas_call(
        _modmm_kernel,
        out_shape=jax.ShapeDtypeStruct((m, n), F32),
        grid=(m // tm, n // tn),
        in_specs=[pl.BlockSpec((tm, d), lambda i, j: (i, 0)),
                  pl.BlockSpec((1, d), lambda i, j: (0, 0)),
                  spec(shift.shape), spec(scale.shape),
                  pl.BlockSpec((d, tn), lambda i, j: (0, j))],
        out_specs=pl.BlockSpec((tm, tn), lambda i, j: (i, j)),
        scratch_shapes=[pltpu.VMEM((tm, d), BF16)],
        compiler_params=_cparams(("parallel", "arbitrary")),
        name="mod_matmul",
    )(x, nw, shift, scale, w)


def _post_mlp_kernel(x_ref, a_ref, wo_ref, g1_ref, nw_ref, sh_ref, sc_ref, g2_ref, w1_ref, w2_ref, fnw_ref,
                     o_ref, x1_ref, h_ref, acc_ref, *, final_norm):
    k = pl.program_id(1)

    @pl.when(k == 0)
    def _():
        x1 = x_ref[...] + g1_ref[0] * _dot(a_ref[...], wo_ref[...])
        x1_ref[...] = x1
        h_ref[...] = _modulated_norm(x1, nw_ref[...], sh_ref[0], sc_ref[0]).astype(BF16)
        acc_ref[...] = jnp.zeros_like(acc_ref)

    u = jnp.maximum(_dot(h_ref[...], w1_ref[...]), 0.0)
    acc_ref[...] += _dot((u * u).astype(BF16), w2_ref[...])

    @pl.when(k == pl.num_programs(1) - 1)
    def _():
        y = x1_ref[...] + g2_ref[0] * acc_ref[...]
        if final_norm:
            r = lax.rsqrt(jnp.mean(y * y, axis=-1, keepdims=True) + EPS)
            y = (y * r) * fnw_ref[...]
        o_ref[...] = y


def _post_mlp(x, a, wo, g1, nw, shift, scale, g2, w1, w2, fnw, *, tm, tf, rows_per_mod, final_norm):
    m, d = x.shape
    ka = a.shape[1]
    ff = w1.shape[1]
    spec = _mod_rows(tm, rows_per_mod)
    return pl.pallas_call(
        functools.partial(_post_mlp_kernel, final_norm=final_norm),
        out_shape=jax.ShapeDtypeStruct((m, d), F32),
        grid=(m // tm, ff // tf),
        in_specs=[pl.BlockSpec((tm, d), lambda i, k: (i, 0)),
                  pl.BlockSpec((tm, ka), lambda i, k: (i, 0)),
                  pl.BlockSpec((ka, d), lambda i, k: (0, 0)),
                  spec(g1.shape),
                  pl.BlockSpec((1, d), lambda i, k: (0, 0)),
                  spec(shift.shape), spec(scale.shape), spec(g2.shape),
                  pl.BlockSpec((d, tf), lambda i, k: (0, k)),
                  pl.BlockSpec((tf, d), lambda i, k: (k, 0)),
                  pl.BlockSpec((1, d), lambda i, k: (0, 0))],
        out_specs=pl.BlockSpec((tm, d), lambda i, k: (i, 0)),
        scratch_shapes=[pltpu.VMEM((tm, d), F32), pltpu.VMEM((tm, d), BF16), pltpu.VMEM((tm, d), F32)],
        compiler_params=_cparams(("parallel", "arbitrary")),
        name="post_mlp",
    )(x, a, wo, g1, nw, shift, scale, g2, w1, w2, fnw)


def _head_expand_matrix():
    e = np.zeros((LANES, SSD_D_INNER), np.float32)
    for h in range(SSD_HEADS):
        e[h, h * SSD_P:(h + 1) * SSD_P] = 1.0
    return jnp.asarray(e, BF16)


def _ssd_prompt_kernel(z_ref, x_ref, bc_ref, dt_ref, cwx_ref, cbx_ref, cwb_ref, cbb_ref, dtb_ref, alog_ref,
                       dskip_ref, nw_ref, exp_ref, tri_ref,
                       y_ref, conv_ref, ssm_ref, xs_ref, bs_ref, st_ref):
    c = pl.program_id(1)
    L = SSD_CHUNK
    tail = SSD_CONV_W - 1

    @pl.when(c == 0)
    def _():
        xs_ref[0:8, :] = jnp.zeros((8, SSD_D_INNER), F32)
        bs_ref[0:8, :] = jnp.zeros((8, SSD_BC), F32)
        st_ref[...] = jnp.zeros_like(st_ref)

    xs_ref[8:8 + L, :] = x_ref[...]
    bs_ref[8:8 + L, :] = bc_ref[...]

    def conv(buf, w_ref, b_ref):
        acc = b_ref[...] + buf[8 - tail:8 - tail + L, :] * w_ref[0:1, :]
        for k in range(1, SSD_CONV_W):
            acc = acc + buf[8 - tail + k:8 - tail + k + L, :] * w_ref[k:k + 1, :]
        return _silu(acc)

    x = conv(xs_ref, cwx_ref, cbx_ref)
    bcv = conv(bs_ref, cwb_ref, cbb_ref)
    xs_ref[8 - tail:8, :] = xs_ref[8 + L - tail:8 + L, :]
    bs_ref[8 - tail:8, :] = bs_ref[8 + L - tail:8 + L, :]

    @pl.when(c == pl.num_programs(1) - 1)
    def _():
        conv_ref[0, :, 0:SSD_D_INNER] = x_ref[L - tail:L, :]
        conv_ref[0, :, SSD_D_INNER:SSD_CONV_DIM] = bc_ref[L - tail:L, :]

    lane = lax.broadcasted_iota(jnp.int32, (1, LANES), 1)
    head_ok = lane < SSD_HEADS
    dt = jnp.where(head_ok, _softplus(dt_ref[...] + dtb_ref[...]), 0.0)
    a = jnp.where(head_ok, -jnp.exp(alog_ref[...]) * LOG2E, 0.0)
    acum = _dot_01x(tri_ref[...], dt * a)
    acum_t = acum.T
    a_last = acum[L - 1:L, :]
    expand = exp_ref[...]
    dt_e = _dot_x01(dt, expand)
    eac_e = _dot_x01(jnp.exp2(acum), expand)
    dend_e = _dot_x01(jnp.exp2(a_last - acum), expand)
    cdec_e = eac_e[L - 1:L, :]

    xdt = x * dt_e
    xdt_b = xdt.astype(BF16)
    xdtw_b = (xdt * dend_e).astype(BF16)
    row = lax.broadcasted_iota(jnp.int32, (L, L), 0)
    col = lax.broadcasted_iota(jnp.int32, (L, L), 1)
    causal = row >= col
    lane2 = lax.broadcasted_iota(jnp.int32, (L, 2 * SSD_P), 1)
    first_head = lane2 < SSD_P

    y_parts = []
    for g in range(SSD_GROUPS):
        b_g = bcv[:, g * SSD_N:(g + 1) * SSD_N]
        c_g = bcv[:, (SSD_GROUPS + g) * SSD_N:(SSD_GROUPS + g + 1) * SSD_N]
        b_gb = b_g.astype(BF16)
        c_gb = c_g.astype(BF16)
        cb = _dot_nt(c_gb, b_gb)
        gl = slice(g * SSD_REP * SSD_P, (g + 1) * SSD_REP * SSD_P)
        st_g = st_ref[:, gl]
        y_off = _dot(c_gb, st_g.astype(BF16)) * eac_e[:, gl]
        pair_out = []
        for j in range(SSD_REP // 2):
            pl_ = slice(g * SSD_REP * SSD_P + j * 2 * SSD_P, g * SSD_REP * SSD_P + (j + 1) * 2 * SSD_P)
            xp = xdt_b[:, pl_]
            ys = []
            for hh in range(2):
                h = g * SSD_REP + 2 * j + hh
                seg = acum[:, h:h + 1] - acum_t[h:h + 1, :]
                dec = jnp.exp2(jnp.where(causal, seg, NEG))
                ys.append(_dot((cb * dec).astype(BF16), xp))
            pair_out.append(jnp.where(first_head, ys[0], ys[1]))
        y_g = jnp.concatenate(pair_out, axis=1) + y_off
        y_parts.append(y_g)
        st_ref[:, gl] = st_g * cdec_e[:, gl] + _dot(b_g.T.astype(BF16), xdtw_b[:, gl])

    y = jnp.concatenate(y_parts, axis=1) + dskip_ref[...] * x
    zg = z_ref[...]
    y = y * _silu(zg)
    r = lax.rsqrt(jnp.mean(y * y, axis=-1, keepdims=True) + EPS)
    y_ref[...] = ((y * r) * nw_ref[...]).astype(BF16)

    @pl.when(c == pl.num_programs(1) - 1)
    def _():
        ssm_ref[0] = st_ref[...].T


def _ssd_prompt(proj, bsz, t, cw, cb, dtb, alog, dskip, nw):
    L = SSD_CHUNK
    nc = t // L
    di = SSD_D_INNER
    tri = jnp.asarray(np.tril(np.ones((L, L), np.float32)), BF16)
    full = lambda shape: pl.BlockSpec(shape, lambda b, c: (0,) * len(shape))
    return pl.pallas_call(
        _ssd_prompt_kernel,
        out_shape=(jax.ShapeDtypeStruct((bsz * t, di), BF16),
                   jax.ShapeDtypeStruct((bsz, SSD_CONV_W - 1, SSD_CONV_DIM), F32),
                   jax.ShapeDtypeStruct((bsz, di, SSD_N), F32)),
        grid=(bsz, nc),
        in_specs=[pl.BlockSpec((L, di), lambda b, c: (b * nc + c, 0)),
                  pl.BlockSpec((L, di), lambda b, c: (b * nc + c, 1)),
                  pl.BlockSpec((L, SSD_BC), lambda b, c: (b * nc + c, 2 * di // SSD_BC)),
                  pl.BlockSpec((L, LANES), lambda b, c: (b * nc + c, (2 * di + SSD_BC) // LANES)),
                  full((SSD_CONV_W, di)), full((1, di)), full((SSD_CONV_W, SSD_BC)), full((1, SSD_BC)),
                  full((1, LANES)), full((1, LANES)), full((1, di)), full((1, di)),
                  full((LANES, di)), full((L, L))],
        out_specs=(pl.BlockSpec((L, di), lambda b, c: (b * nc + c, 0)),
                   pl.BlockSpec((1, SSD_CONV_W - 1, SSD_CONV_DIM), lambda b, c: (b, 0, 0)),
                   pl.BlockSpec((1, di, SSD_N), lambda b, c: (b, 0, 0))),
        scratch_shapes=[pltpu.VMEM((8 + L, di), F32), pltpu.VMEM((8 + L, SSD_BC), F32),
                        pltpu.VMEM((SSD_N, di), F32)],
        compiler_params=_cparams(("parallel", "arbitrary")),
        name="ssd_prompt",
    )(proj, proj, proj, proj, cw[:, :di], cb[:, :di], cw[:, di:], cb[:, di:], dtb, alog, dskip, nw,
      _head_expand_matrix(), tri)


def _ssd_sample_kernel(p_ref, cs_ref, st_ref, cw_ref, cb_ref, dtb_ref, alog_ref, dskip_ref, nw_ref, exp_ref, *rest):
    y_ref, conv_ref, ssm_ref = rest[-3:]
    di = SSD_D_INNER
    z = p_ref[0, :, 0:di]
    xbc = p_ref[0, :, di:di + SSD_CONV_DIM]
    dtr = p_ref[0, :, di + SSD_CONV_DIM:di + SSD_CONV_DIM + LANES]
    cs = cs_ref[0]
    acc = cb_ref[...] + xbc * cw_ref[SSD_CONV_W - 1:SSD_CONV_W, :]
    for k in range(SSD_CONV_W - 1):
        acc = acc + cs[k:k + 1, :] * cw_ref[k:k + 1, :]
    conv_ref[0, 0:SSD_CONV_W - 2, :] = cs[1:SSD_CONV_W - 1, :]
    conv_ref[0, SSD_CONV_W - 2:SSD_CONV_W - 1, :] = xbc
    act = _silu(acc)
    x = act[:, 0:di]
    lane = lax.broadcasted_iota(jnp.int32, (1, LANES), 1)
    head_ok = lane < SSD_HEADS
    dt = jnp.where(head_ok, _softplus(dtr + dtb_ref[...]), 0.0)
    a = jnp.where(head_ok, -jnp.exp(alog_ref[...]), 0.0)
    da = jnp.exp(dt * a)
    lhs = jnp.concatenate([dt, da, jnp.zeros((6, LANES), F32)], axis=0)
    ex = _dot_x01(lhs, exp_ref[...])
    dt_e, da_e = ex[0:1, :], ex[1:2, :]
    xdt = x * dt_e

    eye = (lax.broadcasted_iota(jnp.int32, (LANES, LANES), 0)
           == lax.broadcasted_iota(jnp.int32, (LANES, LANES), 1))

    def to_col(rowvec):
        return jnp.sum(jnp.where(eye, jnp.broadcast_to(rowvec, (LANES, LANES)), 0.0), axis=1, keepdims=True)

    def to_row(colvec):
        return jnp.sum(jnp.where(eye, jnp.broadcast_to(colvec, (LANES, LANES)), 0.0), axis=0, keepdims=True)

    y_off = []
    cbs = []
    for g in range(SSD_GROUPS):
        b_g = act[:, di + g * SSD_N:di + (g + 1) * SSD_N]
        c_g = act[:, di + (SSD_GROUPS + g) * SSD_N:di + (SSD_GROUPS + g + 1) * SSD_N]
        cbs.append(jnp.broadcast_to(jnp.sum(b_g * c_g, axis=1, keepdims=True), (1, SSD_REP * SSD_P)))
        for i in range(SSD_REP * SSD_P // LANES):
            lo = g * SSD_REP * SSD_P + i * LANES
            st = st_ref[0, 0, lo:lo + LANES, :]
            xcol = to_col(xdt[:, lo:lo + LANES])
            dcol = to_col(da_e[:, lo:lo + LANES])
            ssm_ref[0, 0, lo:lo + LANES, :] = st * dcol + xcol * b_g
            y_off.append(to_row(jnp.sum(st * c_g, axis=1, keepdims=True)))
    y = xdt * jnp.concatenate(cbs, axis=1) + jnp.concatenate(y_off, axis=1) * da_e + dskip_ref[...] * x
    y = y * _silu(z)
    r = lax.rsqrt(jnp.mean(y * y, axis=-1, keepdims=True) + EPS)
    y_ref[0] = ((y * r) * nw_ref[...]).astype(BF16)


def _ssd_sample(proj, conv_state, ssm_states, layer, ssm_out_prev, cw, cb, dtb, alog, dskip, nw):
    n_layers, bsz = ssm_states.shape[:2]
    di = SSD_D_INNER
    full = lambda shape: pl.BlockSpec(shape, lambda b: (0,) * len(shape))
    per_b = lambda shape: pl.BlockSpec((1,) + shape, lambda b: (b,) + (0,) * len(shape))
    state_spec = pl.BlockSpec((1, 1, di, SSD_N), lambda b: (layer, b, 0, 0))
    in_specs = [per_b((1, SSD_NP)), per_b((SSD_CONV_W - 1, SSD_CONV_DIM)), state_spec,
                full((SSD_CONV_W, SSD_CONV_DIM)), full((1, SSD_CONV_DIM)), full((1, LANES)), full((1, LANES)),
                full((1, di)), full((1, di)), full((LANES, di))]
    args = [proj.reshape(bsz, 1, SSD_NP), conv_state, ssm_states.reshape(n_layers, bsz, di, SSD_N), cw, cb, dtb, alog,
            dskip, nw, _head_expand_matrix()]
    aliases = {}
    if ssm_out_prev is not None:
        in_specs.append(pl.BlockSpec(memory_space=pl.ANY))
        aliases = {len(args): 2}
        args.append(ssm_out_prev)
    y, conv, ssm = pl.pallas_call(
        _ssd_sample_kernel,
        out_shape=(jax.ShapeDtypeStruct((bsz, 1, di), BF16),
                   jax.ShapeDtypeStruct((bsz, SSD_CONV_W - 1, SSD_CONV_DIM), F32),
                   jax.ShapeDtypeStruct((n_layers, bsz, di, SSD_N), F32)),
        grid=(bsz,),
        in_specs=in_specs,
        out_specs=(per_b((1, di)), per_b((SSD_CONV_W - 1, SSD_CONV_DIM)), state_spec),
        input_output_aliases=aliases,
        compiler_params=_cparams(("parallel",)),
        name="ssd_sample",
    )(*args)
    return y.reshape(bsz, di), conv, ssm


CH = CMP_STRIDE
N_L = CMP_BLOCK // 2
PE_ROWS = 16


def _compress_accumulate(lhs_fn, pe_ref, wab_ref, kv):
    acc = None
    for l in range(N_L):
        lhs = jnp.concatenate([lhs_fn(l, kv).astype(BF16), pe_ref[kv, l]], axis=0)
        t = _dot(lhs, wab_ref[kv, l])
        acc = t if acc is None else acc + t
    return acc


def _compress_finish(acc, w2, nch):
    p = acc[0:nch, 0:GW]
    q_next = pltpu.roll(acc[0:nch, GW:2 * GW], nch - 1, axis=0)
    bias = acc[nch:nch + 1, 0:GW] + acc[nch + 8:nch + 9, GW:2 * GW]
    out = _dot(_silu(p + q_next + bias).astype(BF16), w2)
    row = lax.broadcasted_iota(jnp.int32, (nch, 1), 0)
    return jnp.where(row < nch - 1, out, 0.0)


def _softmax_masked(s, ok, axis, exp=jnp.exp):
    s = jnp.where(ok, s, NEG)
    m = jnp.max(s, axis=axis, keepdims=True)
    p = jnp.where(ok, exp(s - m), 0.0)
    d = jnp.sum(p, axis=axis, keepdims=True)
    return p * jnp.where(d > 0.0, 1.0 / d, 0.0)


def _interleave(n, scores, softmax, values, finish):
    out, pending = [], None
    s_next = scores(0)
    for k in range(n):
        s = s_next
        if k + 1 < n:
            s_next = scores(k + 1)
        p, aux = softmax(k, s)
        if pending is not None:
            out.append(finish(*pending))
        pending = (k, values(k, p), aux)
    out.append(finish(*pending))
    return out


def _topk_mask(score, j, width, shifts, axis):
    n = score.shape[axis]
    rank = jnp.zeros(score.shape, F32)
    for k in shifts:
        other = pltpu.roll(score, k, axis=axis)
        lower = j >= k
        if width != n:
            other = jnp.where(lower, other, pltpu.roll(score, n - width + k, axis=axis))
        rank = rank + jnp.where(lower, jnp.where(other >= score, 1.0, 0.0), jnp.where(other > score, 1.0, 0.0))
    return rank < SEL_TOPK


def _overlap(n_cmp, n_sel):
    c_start = np.arange(n_cmp) * CMP_STRIDE
    s_start = np.arange(n_sel) * SEL_BLOCK
    return ((c_start[:, None] < s_start[None, :] + SEL_BLOCK)
            & (c_start[:, None] + CMP_BLOCK > s_start[None, :])).astype(np.float32)


def _block_bias_matrix(width, n_keys):
    l = np.arange(LANES)[:, None] % width
    blk = np.arange(n_keys)[None, :] // SEL_BLOCK
    return np.where(l == blk, NEG_MXU, 0.0).astype(np.float32)


def _nsa_proj_kernel(x_ref, nw_ref, sh_ref, sc_ref, wr_ref, wt_ref,
                     rows_ref, kpad_ref, qt_ref, gt_ref, vt_ref, cmp_ref, sel_ref, win_ref):
    h = _modulated_norm(x_ref[...], nw_ref[...], sh_ref[0], sc_ref[0]).astype(BF16)
    y = _dot(h, wr_ref[...])
    rows_ref[...] = y[:, 0:KVW].astype(BF16)
    kpad_ref[...] = y[:, KVW:].astype(BF16)
    yt = _dot_nt(wt_ref[...], h)
    qt_ref[0] = (yt[0:1024] * (HD ** -0.5 * LOG2E)).astype(BF16)
    cmp_ref[0] = yt[1024:1024 + KVW]
    sel_ref[0] = yt[1024 + KVW:1024 + 2 * KVW]
    vt_ref[0, 0:GW] = yt[1024 + KVW + GW:1024 + 2 * KVW].astype(BF16)
    vt_ref[0, GW:2 * GW] = yt[1024 + 2 * KVW + GW:1024 + 3 * KVW].astype(BF16)
    gt_ref[0] = _sigmoid(yt[1024 + 3 * KVW:NSA_N])

    @pl.when(pl.program_id(1) == pl.num_programs(1) - 1)
    def _():
        win_ref[0] = yt[1024 + 2 * KVW:1024 + 3 * KVW]


def _nsa_proj(x, nw, shift, scale, wr, wt, bsz, t):
    d = x.shape[1]
    tm = min(WINDOW, t)
    nb = t // tm
    row = lambda w: pl.BlockSpec((tm, w), lambda b, i: (b * nb + i, 0))
    col = lambda h: pl.BlockSpec((1, h, tm), lambda b, i: (b, 0, i))
    mod = lambda: pl.BlockSpec((1, 1, d), lambda b, i: (b, 0, 0))
    nq = NSA_HEADS * HD
    return pl.pallas_call(
        _nsa_proj_kernel,
        out_shape=(jax.ShapeDtypeStruct((bsz * t, KVW), BF16),
                   jax.ShapeDtypeStruct((bsz * t, 2 * KPAD), BF16),
                   jax.ShapeDtypeStruct((bsz, nq, t), BF16),
                   jax.ShapeDtypeStruct((bsz, LANES, t), F32),
                   jax.ShapeDtypeStruct((bsz, 2 * GW, t), BF16),
                   jax.ShapeDtypeStruct((bsz, KVW, t), F32),
                   jax.ShapeDtypeStruct((bsz, KVW, t), F32),
                   jax.ShapeDtypeStruct((bsz, KVW, tm), F32)),
        grid=(bsz, nb),
        in_specs=[row(d), pl.BlockSpec((1, d), lambda b, i: (0, 0)), mod(), mod(),
                  pl.BlockSpec((d, KVW + 2 * KPAD), lambda b, i: (0, 0)),
                  pl.BlockSpec((NSA_N, d), lambda b, i: (0, 0))],
        out_specs=(row(KVW), row(2 * KPAD), col(nq), col(LANES), col(2 * GW), col(KVW), col(KVW),
                   pl.BlockSpec((1, KVW, tm), lambda b, i: (b, 0, 0))),
        compiler_params=_cparams(("parallel", "arbitrary")),
        name="nsa_proj",
    )(x, nw, shift, scale, wr, wt)


def _cmp_prompt_kernel(rows_ref, pe_ref, wab_ref, w2k_ref, w2v_ref, kc_ref, vc_ref):
    nch = rows_ref.shape[1]
    lhs = lambda l, kv: rows_ref[0, :, l * KVW + kv * GW:l * KVW + (kv + 1) * GW]
    kc_ref[0] = _compress_finish(_compress_accumulate(lhs, pe_ref, wab_ref, 0), w2k_ref[...], nch).astype(BF16)
    vc_ref[0] = _compress_finish(_compress_accumulate(lhs, pe_ref, wab_ref, 1), w2v_ref[...], nch).T.astype(BF16)


def _cmp_prompt(rows, pe_t, wab, w2k_pad, w2v):
    bsz, nch, _ = rows.shape
    full = lambda shape: pl.BlockSpec(shape, lambda b: (0,) * len(shape))
    return pl.pallas_call(
        _cmp_prompt_kernel,
        out_shape=(jax.ShapeDtypeStruct((bsz, nch, KPAD), BF16), jax.ShapeDtypeStruct((bsz, GW, nch), BF16)),
        grid=(bsz,),
        in_specs=[pl.BlockSpec((1, nch, CH * KVW), lambda b: (b, 0, 0)),
                  full(pe_t.shape), full(wab.shape), full(w2k_pad.shape), full(w2v.shape)],
        out_specs=(pl.BlockSpec((1, nch, KPAD), lambda b: (b, 0, 0)), pl.BlockSpec((1, GW, nch), lambda b: (b, 0, 0))),
        compiler_params=_cparams(("parallel",)),
        name="nsa_compress_prompt",
    )(rows, pe_t, wab, w2k_pad, w2v)


def _nsa_prompt_kernel(qt_ref, gt_ref, kc_ref, vct_ref, kpad_ref, vt_ref, ovt_ref, nege_ref, o_ref,
                       *, tq, n_sel, ck):
    i = pl.program_id(1)
    t0 = i * tq
    nq = REP * tq
    tl = t0 + lax.broadcasted_iota(jnp.int32, (1, tq), 1)
    zq = jnp.zeros((HD, nq), BF16)

    def lanes4(a):
        return jnp.concatenate([a] * REP, axis=1)

    def with_ones(vt):
        return jnp.concatenate([vt, jnp.ones((ONES_ROWS, vt.shape[1]), BF16)], axis=0)

    def normalized(acc):
        return acc[0:HD] * (1.0 / acc[HD:HD + 1])

    tl4 = lanes4(tl)
    qg = [jnp.concatenate([qt_ref[0, (g * REP + r) * HD:(g * REP + r + 1) * HD, :] for r in range(REP)], axis=1)
          for g in range(KVH)]
    q_rhs = [jnp.concatenate([q, zq], axis=0) for q in qg]

    ncp = kc_ref.shape[1]
    cend = lax.broadcasted_iota(jnp.int32, (ncp, 1), 0) * CMP_STRIDE + (CMP_BLOCK - 1)
    ok_cmp = cend <= tl4
    s_cmp = [_dot(kc_ref[0, :, g * LANES:(g + 1) * LANES], q_rhs[g]) for g in range(KVH)]
    p_cmp = [_softmax_masked(s, ok_cmp, 0, jnp.exp2) for s in s_cmp]
    o_cmp = [_dot(vct_ref[0, g * HD:(g + 1) * HD, :], p_cmp[g].astype(BF16)) for g in range(KVH)]
    imp = jnp.zeros((LANES, tq), F32)
    for g, p in enumerate(p_cmp):
        psum = p[:, 0:tq] + p[:, tq:2 * tq] + p[:, 2 * tq:3 * tq] + p[:, 3 * tq:4 * tq]
        imp = imp + _dot_01x(ovt_ref[g], psum)

    row = lax.broadcasted_iota(jnp.int32, (LANES, tq), 0)
    jj = row % n_sel
    cur = tl // SEL_BLOCK
    valid = jj * SEL_BLOCK <= tl
    forced = (jj == 0) | (jj == cur) | (jj == cur - 1)
    score = jnp.where(valid, jnp.where(forced, BIG, imp), NEG)
    top = _topk_mask(score, jj, n_sel, range(1, n_sel), 0)
    blocked = jnp.where(top & valid, 0.0, 1.0)

    n_full = t0 // ck
    sel_rhs = [jnp.concatenate([lanes4(jnp.where(row // n_sel == g, blocked, 0.0).astype(BF16)), q_rhs[g]], axis=0)
               for g in range(KVH)]

    def sel_chunk(c, carries, causal):
        k0 = pl.multiple_of(c * ck, ck)
        bias_rows = nege_ref[pl.ds(k0, ck), :]
        if causal:
            cbias = jnp.where(k0 + lax.broadcasted_iota(jnp.int32, (ck, 1), 0) <= tl4, 0.0, NEG)

        def scores(g):
            lhs = jnp.concatenate([bias_rows, kpad_ref[0, pl.ds(k0, ck), g * LANES:(g + 1) * LANES]], axis=1)
            s = _dot(lhs, sel_rhs[g])
            return s + cbias if causal else s

        def softmax(g, s):
            m, acc = carries[g]
            m_new = jnp.maximum(m, jnp.max(s, axis=0, keepdims=True))
            return jnp.exp2(s - m_new).astype(BF16), (m_new, jnp.exp2(m - m_new), acc)

        def values(g, p):
            return _dot(with_ones(vt_ref[0, g * HD:(g + 1) * HD, pl.ds(k0, ck)]), p)

        def finish(g, pv, aux):
            m_new, alpha, acc = aux
            return m_new, alpha * acc + pv

        return tuple(_interleave(KVH, scores, softmax, values, finish))

    init = (jnp.full((1, nq), NEG, F32), jnp.zeros((HD + ONES_ROWS, nq), F32))
    carries = lax.fori_loop(0, n_full, lambda c, cr: sel_chunk(c, cr, False), (init,) * KVH)
    o_sel = [normalized(acc) for _, acc in sel_chunk(n_full, carries, True)]

    kw = WINDOW + tq
    k_start = pl.multiple_of(jnp.maximum(t0 - WINDOW, 0), LANES)
    dpos = tl - (k_start + lax.broadcasted_iota(jnp.int32, (kw, 1), 0))
    wbias = lanes4(jnp.where((dpos >= 0) & (dpos <= WINDOW), 0.0, NEG))

    def win_scores(g):
        return _dot(kpad_ref[0, pl.ds(k_start, kw), (KVH + g) * LANES:(KVH + g + 1) * LANES], q_rhs[g]) + wbias

    def win_softmax(g, s):
        return jnp.exp2(s - jnp.max(s, axis=0, keepdims=True)).astype(BF16), None

    def win_values(g, p):
        return _dot(with_ones(vt_ref[0, GW + g * HD:GW + (g + 1) * HD, pl.ds(k_start, kw)]), p)

    o_win = _interleave(KVH, win_scores, win_softmax, win_values, lambda g, pv, aux: normalized(pv))

    gt = gt_ref[0]
    outs = []
    for g in range(KVH):
        for r in range(REP):
            h = g * REP + r
            sl = slice(r * tq, (r + 1) * tq)
            outs.append(gt[3 * h:3 * h + 1, :] * o_cmp[g][:, sl] + gt[3 * h + 1:3 * h + 2, :] * o_sel[g][:, sl]
                        + gt[3 * h + 2:3 * h + 3, :] * o_win[g][:, sl])
    o_ref[0] = jnp.concatenate(outs, axis=0).T.astype(BF16)


def _nsa_prompt_attention(qt, gt, kc, vct, kpad, vt):
    bsz, nd, t = qt.shape
    tq = LANES
    ck = 4 * LANES
    n_sel = t // SEL_BLOCK
    ncp = t // CH
    assert KVH * n_sel == LANES and ncp == LANES and t % ck == 0
    ovt = np.zeros((KVH, LANES, ncp), np.float32)
    for g in range(KVH):
        ovt[g, g * n_sel:(g + 1) * n_sel, :ncp - 1] = _overlap(ncp - 1, n_sel).T
    nege_t = _block_bias_matrix(n_sel, t).T
    per_b = lambda shape: pl.BlockSpec((1,) + shape, lambda b, i: (b,) + (0,) * len(shape))
    full = lambda shape: pl.BlockSpec(shape, lambda b, i: (0,) * len(shape))
    out = pl.pallas_call(
        functools.partial(_nsa_prompt_kernel, tq=tq, n_sel=n_sel, ck=ck),
        out_shape=jax.ShapeDtypeStruct((bsz, t, nd), BF16),
        grid=(bsz, t // tq),
        in_specs=[pl.BlockSpec((1, nd, tq), lambda b, i: (b, 0, i)),
                  pl.BlockSpec((1, LANES, tq), lambda b, i: (b, 0, i)),
                  per_b((ncp, KPAD)), per_b((GW, ncp)), per_b((t, 2 * KPAD)), per_b((2 * GW, t)),
                  full((KVH, LANES, ncp)), full((t, LANES))],
        out_specs=pl.BlockSpec((1, tq, nd), lambda b, i: (b, i, 0)),
        compiler_params=_cparams(("parallel", "arbitrary")),
        name="nsa_prompt_attention",
    )(qt, gt, kc, vct, kpad.reshape(bsz, t, 2 * KPAD), vt, jnp.asarray(ovt, BF16), jnp.asarray(nege_t, BF16))
    return out.reshape(bsz * t, nd)


def _nsa_sample_kernel(pt_ref, p_ref, *rest, n_pages, past):
    del pt_ref
    cmp_pages = rest[:n_pages]
    sel_pages = rest[n_pages:2 * n_pages]
    win_ref, pe_ref, wab_ref, w2_ref, ov_ref, nege_ref, perm_ref = rest[2 * n_pages:2 * n_pages + 7]
    o_ref, wout_ref, xr_ref = rest[-3:]
    nch = n_pages * (PAGE_SIZE // CH)
    cpp = PAGE_SIZE // CH
    t = past
    keep = win_ref.shape[-1]
    rows_q = 8

    def compress_cache():
        perm = perm_ref[...]
        for p, pg in enumerate(cmp_pages):
            for kv in range(2):
                rows = _dot_nt(perm, pg[0, 0, kv].reshape(GW, PAGE_SIZE).astype(BF16))
                for l in range(N_L):
                    xr_ref[kv, l, p * cpp:(p + 1) * cpp, :] = rows[l * cpp:(l + 1) * cpp, :]
        out = []
        for kv in range(2):
            acc = _compress_accumulate(lambda l, kv: xr_ref[kv, l], pe_ref, wab_ref, kv)
            out.append(_compress_finish(acc, w2_ref[kv], nch).T.astype(BF16))
        return out

    lane = lax.broadcasted_iota(jnp.int32, (rows_q, LANES), 1)
    rowi = lax.broadcasted_iota(jnp.int32, (rows_q, LANES), 0)
    cend = lax.broadcasted_iota(jnp.int32, (1, nch), 1) * CMP_STRIDE + (CMP_BLOCK - 1)
    n_sel = -(-(t + 1) // SEL_BLOCK)
    nj = max(n_sel, SEL_TOPK)
    cur = t // SEL_BLOCK
    valid = (lane < n_sel) & (lane * SEL_BLOCK <= t)
    forced = (lane == 0) | (lane == cur) | (lane == cur - 1)
    shifts = list(range(1, nj)) + list(range(LANES - nj + 1, LANES))
    gates = jnp.broadcast_to(_sigmoid(p_ref[0, :, 1024 + 3 * KVW:NSA_N]), (rows_q, LANES))
    eye = (lax.broadcasted_iota(jnp.int32, (HD, HD), 0) == lax.broadcasted_iota(jnp.int32, (HD, HD), 1))
    last_lane = lax.broadcasted_iota(jnp.int32, (HD, keep), 1) == keep - 1

    G = range(KVH)
    new_row = lambda off, g: p_ref[0, :, off + g * HD:off + (g + 1) * HD]
    q = [jnp.concatenate([p_ref[0, :, (g * REP + r) * HD:(g * REP + r + 1) * HD] for r in range(REP)]
                         + [jnp.zeros((rows_q - REP, HD), F32)], axis=0) * (HD ** -0.5) for g in G]
    qb = [x.astype(BF16) for x in q]

    def softmax_with_new(s, s_new, ok_new):
        s_new = jnp.where(ok_new, s_new, NEG)
        m = jnp.maximum(jnp.max(s, axis=1, keepdims=True), s_new)
        pr = jnp.exp(s - m)
        pr_new = jnp.where(ok_new, jnp.exp(s_new - m), 0.0)
        return pr.astype(BF16), pr_new, 1.0 / (jnp.sum(pr, axis=1, keepdims=True) + pr_new)

    kw = [win_ref[0, 0, 0, g] for g in G]
    vw = [win_ref[0, 0, 1, g] for g in G]
    kw_new = [new_row(1024 + 2 * KVW, g) for g in G]
    vw_new = [new_row(1024 + 2 * KVW + GW, g) for g in G]
    s_win = [_dot(qb[g], kw[g].astype(BF16)) for g in G]
    sm = [softmax_with_new(s_win[g], jnp.sum(q[g] * kw_new[g], axis=1, keepdims=True), jnp.full((rows_q, 1), True))
          for g in G]
    pv = [_dot_nt(sm[g][0], vw[g].astype(BF16)) for g in G]
    o_win = [(pv[g] + sm[g][1] * vw_new[g]) * sm[g][2] for g in G]
    for g in G:
        for kv, old, new in ((0, kw[g], kw_new[g]), (1, vw[g], vw_new[g])):
            col = jnp.sum(jnp.where(eye, jnp.broadcast_to(new, (HD, HD)), 0.0), axis=1, keepdims=True)
            wout_ref[0, 0, kv, g] = jnp.where(last_lane, col, pltpu.roll(old, keep - 1, axis=1))

    kt = [jnp.concatenate([pg[0, 0, 0, g] for pg in sel_pages], axis=1).astype(BF16) for g in G]
    vt = [jnp.concatenate([pg[0, 0, 1, g] for pg in sel_pages], axis=1).astype(BF16) for g in G]

    kct, vct = compress_cache()
    s_cmp = [_dot(qb[g], kct[g * HD:(g + 1) * HD, :]) for g in G]
    p_cmp = [_softmax_masked(s, cend <= t, 1) for s in s_cmp]
    o_cmp = [_dot_nt(p_cmp[g].astype(BF16), vct[g * HD:(g + 1) * HD, :]) for g in G]
    allowed = []
    for g in G:
        psum = jnp.broadcast_to(jnp.sum(p_cmp[g][0:REP], axis=0, keepdims=True), (rows_q, nch))
        imp = _dot_x01(psum, ov_ref[...])
        score = jnp.where(valid, jnp.where(forced, BIG, imp), NEG)
        score = jnp.where(lane < nj, score, -3e38)
        allowed.append(_topk_mask(score, lane, LANES, shifts, 1) & valid)

    nege = nege_ref[...]
    s_sel = [_dot(jnp.concatenate([jnp.where(allowed[g], 0.0, 1.0).astype(BF16), qb[g]], axis=1),
                  jnp.concatenate([nege, kt[g]], axis=0)) for g in G]
    v_new = [new_row(1024 + KVW + GW, g) for g in G]
    sm = [softmax_with_new(s_sel[g], jnp.sum(q[g] * new_row(1024 + KVW, g), axis=1, keepdims=True),
                           jnp.sum(jnp.where((lane == cur) & allowed[g], 1.0, 0.0), axis=1, keepdims=True) > 0.5)
          for g in G]
    pv = [_dot_nt(sm[g][0], vt[g]) for g in G]
    o_sel = [(pv[g] + sm[g][1] * v_new[g]) * sm[g][2] for g in G]

    for g in G:
        o = None
        for br, ob in enumerate((o_cmp[g], o_sel[g], o_win[g])):
            gcol = jnp.sum(jnp.where(lane == (g * REP + rowi) * 3 + br, gates, 0.0), axis=1, keepdims=True)
            o = gcol * ob if o is None else o + gcol * ob
        for r in range(REP):
            h = g * REP + r
            o_ref[0, :, h * HD:(h + 1) * HD] = o[r:r + 1, :].astype(BF16)


def _nsa_sample(proj, cmp_view, sel_view, win_view, layer, win_out_prev, page_table, pe_t, wab, w2bd):
    n_layers = win_view.shape[0]
    bsz, n_pages = page_table.shape
    past = n_pages * PAGE_SIZE
    keep = win_view.shape[-1]
    nch = past // CH
    assert nch == LANES and keep <= WINDOW and past % SEL_BLOCK == 0 and past - keep >= 0
    n_sel = -(-(past + 1) // SEL_BLOCK)
    ov = np.zeros((nch, LANES), np.float32)
    ov[:nch - 1, :n_sel] = _overlap(nch - 1, n_sel)
    full = lambda shape: pl.BlockSpec(shape, lambda b, pt: (0,) * len(shape))
    once = lambda shape: pl.BlockSpec(shape, lambda b, pt: (0,) * len(shape), pipeline_mode=pl.Buffered(1))
    per_b = lambda shape: pl.BlockSpec((1,) + shape, lambda b, pt: (b,) + (0,) * len(shape))
    page_shape = (2, KVH, HD, PAGE_SIZE)
    page = lambda p: pl.BlockSpec((1, 1) + page_shape, lambda b, pt: (layer, pt[b * n_pages + p], 0, 0, 0, 0))
    win_shape = (2, KVH, HD, keep)
    win_spec = pl.BlockSpec((1, 1) + win_shape, lambda b, pt: (layer, b, 0, 0, 0, 0))
    in_specs = ([per_b((1, NSA_N))] + [page(p) for p in range(n_pages)] * 2
                + [win_spec, full(pe_t.shape), once(wab.shape), full(w2bd.shape), full((nch, LANES)),
                   full((LANES, past)), full((PAGE_SIZE, PAGE_SIZE))])
    cpp = PAGE_SIZE // CH
    perm = np.zeros((PAGE_SIZE, PAGE_SIZE), np.float32)
    for l in range(N_L):
        for c in range(cpp):
            perm[l * cpp + c, c * CH + l] = 1.0
    args = [page_table.reshape(-1), proj.reshape(bsz, 1, NSA_N)] + [cmp_view] * n_pages + [sel_view] * n_pages + [
        win_view, pe_t, wab, w2bd, jnp.asarray(ov, BF16), jnp.asarray(_block_bias_matrix(LANES, past), BF16),
        jnp.asarray(perm, BF16)]
    aliases = {}
    if win_out_prev is not None:
        in_specs.append(pl.BlockSpec(memory_space=pl.ANY))
        aliases = {len(args): 1}
        args.append(win_out_prev)
    o, wout = pl.pallas_call(
        functools.partial(_nsa_sample_kernel, n_pages=n_pages, past=past),
        out_shape=(jax.ShapeDtypeStruct((bsz, 1, NSA_HEADS * HD), BF16),
                   jax.ShapeDtypeStruct((n_layers, bsz) + win_shape, F32)),
        grid_spec=pltpu.PrefetchScalarGridSpec(
            num_scalar_prefetch=1, grid=(bsz,), in_specs=in_specs,
            out_specs=(per_b((1, NSA_HEADS * HD)), win_spec),
            scratch_shapes=[pltpu.VMEM((2, N_L, nch, GW), F32)]),
        input_output_aliases=aliases,
        compiler_params=_cparams(("arbitrary",)),
        name="nsa_sample",
    )(*args)
    return o.reshape(bsz, NSA_HEADS * HD), wout


def _prep_nsa(w_in, w_out, pe, w1, w2):
    n_gate = 3 * NSA_HEADS
    w_full = jnp.concatenate([w_in, jnp.zeros((D_MODEL, LANES - n_gate), F32)], axis=1)

    def k_padded(lo):
        k = w_in[:, lo:lo + GW].reshape(D_MODEL, KVH, HD)
        return jnp.concatenate([k, jnp.zeros_like(k)], axis=2).reshape(D_MODEL, KPAD)

    w_rows = jnp.concatenate([w_in[:, 1024:1024 + KVW], k_padded(1024 + KVW), k_padded(1024 + 2 * KVW)], axis=1)
    eye = jnp.eye(KVH, dtype=F32)
    w1r = w1.reshape(2, 2, N_L, HD, HD)
    wab = jnp.einsum('khlde,gf->klgdhfe', w1r, eye).reshape(2, N_L, GW, 2 * GW)
    w2bd = jnp.einsum('kde,gf->kgdfe', w2, eye)
    w2k_pad = jnp.concatenate([w2bd[0], jnp.zeros_like(w2bd[0])], axis=3).reshape(GW, KPAD)
    per = pe.reshape(2, 2, N_L, 1, 1, HD)
    pe_t = jnp.broadcast_to(per, (2, 2, N_L, 8, KVH, HD)).transpose(0, 2, 1, 3, 4, 5).reshape(2, N_L, PE_ROWS, GW)
    b = lambda a: a.astype(BF16)
    return dict(w_sample=b(w_full), w_rows=b(w_rows), w_t=b(w_full.T), wo=b(w_out), pe_t=b(pe_t), wab=b(wab),
                w2bd=b(w2bd.reshape(2, GW, GW)), w2k_pad=b(w2k_pad))


def _nsa_prompt_layer(x, nw, shift, scale, prep, bsz, t):
    rows, kpad, qt, gt, vt, cmp_t, sel_t, win_t = _nsa_proj(x, nw, shift, scale, prep["w_rows"], prep["w_t"], bsz, t)
    kc, vct = _cmp_prompt(rows.reshape(bsz, t // CH, CH * KVW), prep["pe_t"], prep["wab"], prep["w2k_pad"],
                          prep["w2bd"][1])
    return _nsa_prompt_attention(qt, gt, kc, vct, kpad, vt), (cmp_t, sel_t, win_t)


def _prep_ssd(w_in, dt_bias, a_log, d_skip):
    pad = SSD_NP - w_in.shape[1]
    w = jnp.concatenate([w_in, jnp.zeros((D_MODEL, pad), F32)], axis=1).astype(BF16)
    pad_h = lambda v: jnp.concatenate([v, jnp.zeros((LANES - SSD_HEADS,), F32)]).reshape(1, LANES)
    return w, pad_h(dt_bias), pad_h(a_log), jnp.repeat(d_skip, SSD_P).reshape(1, SSD_D_INNER)


def kernel(x_prompt, x_sample, cache_kv_cmp, cache_kv_sel, cache_kv_win, state_ssm, state_conv, page_table, c_prompt, c_sample, ada_w, ada_b, norm_w, mlp_w1, mlp_w2, nsa_w_in, nsa_w_out, nsa_cmp_pe, nsa_cmp_w1, nsa_cmp_w2, ssd_w_in, ssd_conv_w, ssd_conv_b, ssd_dt_bias, ssd_a_log, ssd_d, ssd_norm_w, ssd_w_out, final_norm_w):
    bp, t, d = x_prompt.shape
    bs = x_sample.shape[0]
    xp = x_prompt.reshape(bp * t, d)
    xs = x_sample.reshape(bs, d)
    mods = _adaln(jnp.concatenate([c_prompt, c_sample], axis=0), ada_w, ada_b)
    w1b = mlp_w1.astype(BF16)
    w2b = mlp_w2.astype(BF16)
    fnw = final_norm_w.reshape(1, d)
    tm = 512
    n_nsa, n_pool = cache_kv_cmp.shape[:2]
    cmp_view = jnp.transpose(cache_kv_cmp, (0, 1, 3, 4, 5, 2))
    sel_view = jnp.transpose(cache_kv_sel, (0, 1, 3, 4, 5, 2))
    win_view = jnp.transpose(cache_kv_win, (0, 1, 3, 4, 5, 2))
    win_s = ssm_s = None
    outs = {k: [] for k in ("cmp_p", "cmp_s", "sel_p", "sel_s", "win_p", "win_s", "ssm_p", "ssm_s", "conv_p", "conv_s")}
    kv_rows = lambda a, n: a.reshape(n + (2, KVH, HD))
    kv_cols = lambda a: a.reshape(a.shape[0], 2, KVH, HD, a.shape[-1])

    for i in range(DEPTH):
        jl = i // 2
        mp = [mods[i, :bp, k * d:(k + 1) * d].reshape(bp, 1, d) for k in range(6)]
        ms = [mods[i, bp:, k * d:(k + 1) * d].reshape(1, bs, d) for k in range(6)]
        nw0 = norm_w[i, 0].reshape(1, d)
        nw1 = norm_w[i, 1].reshape(1, d)
        if i % 2 == 0:
            prep = _prep_nsa(nsa_w_in[jl], nsa_w_out[jl], nsa_cmp_pe[jl], nsa_cmp_w1[jl], nsa_cmp_w2[jl])
            wo = prep["wo"]
            ap, (cmp_t, sel_t, win_t) = _nsa_prompt_layer(xp, nw0, mp[0], mp[1], prep, bp, t)
            ps = _mod_matmul(xs, nw0, ms[0], ms[1], prep["w_sample"], tm=bs, tn=896, rows_per_mod=None)
            as_, win_s = _nsa_sample(ps, cmp_view, sel_view, win_view, jl, win_s, page_table, prep["pe_t"],
                                     prep["wab"], prep["w2bd"])
            outs["cmp_p"].append(kv_cols(cmp_t))
            outs["sel_p"].append(kv_cols(sel_t))
            outs["win_p"].append(kv_cols(win_t))
            outs["cmp_s"].append(kv_rows(ps[:, 1024:1024 + KVW], (bs, 1)))
            outs["sel_s"].append(kv_rows(ps[:, 1024 + KVW:1024 + 2 * KVW], (bs, 1)))
        else:
            w, dtb, alog, dsk = _prep_ssd(ssd_w_in[jl], ssd_dt_bias[jl], ssd_a_log[jl], ssd_d[jl])
            wo = ssd_w_out[jl].astype(BF16)
            snw = ssd_norm_w[jl].reshape(1, SSD_D_INNER)
            cw = ssd_conv_w[jl]
            cb = ssd_conv_b[jl].reshape(1, SSD_CONV_DIM)
            pp = _mod_matmul(xp, nw0, mp[0], mp[1], w, tm=tm, tn=896, rows_per_mod=t)
            ps = _mod_matmul(xs, nw0, ms[0], ms[1], w, tm=bs, tn=896, rows_per_mod=None)
            ap, conv_p, ssm_p = _ssd_prompt(pp, bp, t, cw, cb, dtb, alog, dsk, snw)
            as_, conv_s, ssm_s = _ssd_sample(ps, state_conv[jl], state_ssm, jl, ssm_s, cw, cb, dtb, alog, dsk, snw)
            outs["conv_p"].append(conv_p)
            outs["conv_s"].append(conv_s)
            outs["ssm_p"].append(ssm_p.reshape(bp, SSD_HEADS, SSD_P, SSD_N))
        last = i == DEPTH - 1
        xp = _post_mlp(xp, ap, wo, mp[2], nw1, mp[3], mp[4], mp[5], w1b[i], w2b[i], fnw,
                       tm=tm, tf=1024, rows_per_mod=t, final_norm=last)
        xs = _post_mlp(xs, as_, wo, ms[2], nw1, ms[3], ms[4], ms[5], w1b[i], w2b[i], fnw,
                       tm=bs, tf=512, rows_per_mod=None, final_norm=last)

    st = lambda k: jnp.stack(outs[k])
    time_major = lambda a: jnp.transpose(a, (0, 1, 5, 2, 3, 4))
    return (xp.reshape(bp, t, d), xs.reshape(bs, 1, d), time_major(st("cmp_p")), st("cmp_s"),
            time_major(st("sel_p")), st("sel_s"), time_major(st("win_p")), time_major(win_s), st("ssm_p"),
            ssm_s.reshape(state_ssm.shape), st("conv_p"), st("conv_s"))
```

```python
import functools
import math

import numpy as np
import jax
import jax.numpy as jnp
from jax import lax
from jax.experimental import pallas as pl
from jax.experimental.pallas import tpu as pltpu

F32 = jnp.float32
BF16 = jnp.bfloat16

D_MODEL = 1024
DEPTH = 4
D_FF = 4 * D_MODEL
EPS = 1e-6
PAGE_SIZE = 128
NSA_HEADS = 16
HD = 64
KVH = 4
REP = 4
KVW = 2 * KVH * HD
GW = KVH * HD
CMP_BLOCK = 32
CMP_STRIDE = 16
SEL_BLOCK = 64
SEL_TOPK = 8
WINDOW = 512
SSD_D_INNER = 2 * D_MODEL
SSD_P = 64
SSD_HEADS = SSD_D_INNER // SSD_P
SSD_GROUPS = 4
SSD_REP = SSD_HEADS // SSD_GROUPS
SSD_N = 128
SSD_CONV_W = 4
SSD_CHUNK = 256
SSD_BC = 2 * SSD_GROUPS * SSD_N
SSD_CONV_DIM = SSD_D_INNER + SSD_BC
BIG = 1e30
NEG = -1e30
NEG_MXU = -2.0 ** 100
LOG2E = math.log2(math.e)
ONES_ROWS = 16
LANES = 128
NSA_N = 1024 + 3 * KVW + LANES
KPAD = KVH * LANES
SSD_NP = SSD_D_INNER + SSD_CONV_DIM + 2 * LANES
VMEM_LIMIT = 48 * 1024 * 1024


def _cparams(sem):
    return pltpu.CompilerParams(dimension_semantics=sem, vmem_limit_bytes=VMEM_LIMIT)


def _dot(a, b):
    return jnp.dot(a, b, preferred_element_type=F32)


def _dot_nt(a, b):
    return lax.dot_general(a, b, (((1,), (1,)), ((), ())), preferred_element_type=F32)


def _split(x, n):
    parts, r = [], x
    for i in range(n):
        p = r.astype(BF16)
        parts.append(p)
        if i + 1 < n:
            r = r - p.astype(F32)
    return parts


def _dot_x01(x, m01, n=3):
    acc = None
    for p in _split(x, n):
        t = _dot(p, m01)
        acc = t if acc is None else acc + t
    return acc


def _dot_01x(m01, x, n=3):
    acc = None
    for p in _split(x, n):
        t = _dot(m01, p)
        acc = t if acc is None else acc + t
    return acc


def _sigmoid(x):
    return 0.5 + 0.5 * jnp.tanh(0.5 * x)


def _silu(x):
    return x * _sigmoid(x)


def _softplus(x):
    return jnp.maximum(x, 0.0) + jnp.log(1.0 + jnp.exp(-jnp.abs(x)))


def _modulated_norm(x, nw, shift, scale):
    r = lax.rsqrt(jnp.mean(x * x, axis=-1, keepdims=True) + EPS)
    return (x * r) * nw * (1.0 + scale) + shift


def _adaln_kernel(c_ref, w_ref, b_ref, o_ref):
    a = _silu(c_ref[...]).astype(BF16)
    o_ref[0] = _dot(a, w_ref[0].astype(BF16)) + b_ref[0]


def _adaln(c_all, ada_w, ada_b):
    m, d = c_all.shape
    n = ada_w.shape[-1]
    tn = 1536
    return pl.pallas_call(
        _adaln_kernel,
        out_shape=jax.ShapeDtypeStruct((DEPTH, m, n), F32),
        grid=(DEPTH, n // tn),
        in_specs=[pl.BlockSpec((m, d), lambda l, j: (0, 0)),
                  pl.BlockSpec((1, d, tn), lambda l, j: (l, 0, j)),
                  pl.BlockSpec((1, 1, tn), lambda l, j: (l, 0, j))],
        out_specs=pl.BlockSpec((1, m, tn), lambda l, j: (l, 0, j)),
        compiler_params=_cparams(("parallel", "parallel")),
        name="adaln",
    )(c_all, ada_w, ada_b.reshape(DEPTH, 1, n))


def _modmm_kernel(x_ref, nw_ref, sh_ref, sc_ref, w_ref, o_ref, h_ref):
    @pl.when(pl.program_id(1) == 0)
    def _():
        h_ref[...] = _modulated_norm(x_ref[...], nw_ref[...], sh_ref[0], sc_ref[0]).astype(BF16)

    o_ref[...] = _dot(h_ref[...], w_ref[...])


def _mod_rows(tm, rows_per_mod):
    if rows_per_mod is None:
        return lambda shape: pl.BlockSpec((1,) + shape[1:], lambda i, j: (0, 0, 0))
    bpb = rows_per_mod // tm
    return lambda shape: pl.BlockSpec((1,) + shape[1:], lambda i, j: (i // bpb, 0, 0))


def _mod_matmul(x, nw, shift, scale, w, *, tm, tn, rows_per_mod):
    m, d = x.shape
    n = w.shape[1]
    spec = _mod_rows(tm, rows_per_mod)
    w_spec = (pl.BlockSpec((d, tn), lambda i, j: (0, 0), pipeline_mode=pl.Buffered(1)) if tn == n
              else pl.BlockSpec((d, tn), lambda i, j: (0, j)))
    return pl.pallas_call(
        _modmm_kernel,
        out_shape=jax.ShapeDtypeStruct((m, n), F32),
        grid=(m // tm, n // tn),
        in_specs=[pl.BlockSpec((tm, d), lambda i, j: (i, 0)),
                  pl.BlockSpec((1, d), lambda i, j: (0, 0)),
                  spec(shift.shape), spec(scale.shape),
                  w_spec],
        out_specs=pl.BlockSpec((tm, tn), lambda i, j: (i, j)),
        scratch_shapes=[pltpu.VMEM((tm, d), BF16)],
        compiler_params=_cparams(("parallel", "arbitrary")),
        name="mod_matmul",
    )(x, nw, shift, scale, w)


def _post_mlp_kernel(x_ref, a_ref, wo_ref, g1_ref, nw_ref, sh_ref, sc_ref, g2_ref, w1_ref, w2_ref, fnw_ref,
                     o_ref, x1_ref, h_ref, acc_ref, *, final_norm):
    k = pl.program_id(1)

    @pl.when(k == 0)
    def _():
        x1 = x_ref[...] + g1_ref[0] * _dot(a_ref[...], wo_ref[...])
        x1_ref[...] = x1
        h_ref[...] = _modulated_norm(x1, nw_ref[...], sh_ref[0], sc_ref[0]).astype(BF16)
        acc_ref[...] = jnp.zeros_like(acc_ref)

    u = jnp.maximum(_dot(h_ref[...], w1_ref[...]), 0.0)
    acc_ref[...] += _dot((u * u).astype(BF16), w2_ref[...])

    @pl.when(k == pl.num_programs(1) - 1)
    def _():
        y = x1_ref[...] + g2_ref[0] * acc_ref[...]
        if final_norm:
            r = lax.rsqrt(jnp.mean(y * y, axis=-1, keepdims=True) + EPS)
            y = (y * r) * fnw_ref[...]
        o_ref[...] = y


def _post_mlp(x, a, wo, g1, nw, shift, scale, g2, w1, w2, fnw, *, tm, tf, rows_per_mod, final_norm):
    m, d = x.shape
    ka = a.shape[1]
    ff = w1.shape[1]
    spec = _mod_rows(tm, rows_per_mod)
    return pl.pallas_call(
        functools.partial(_post_mlp_kernel, final_norm=final_norm),
        out_shape=jax.ShapeDtypeStruct((m, d), F32),
        grid=(m // tm, ff // tf),
        in_specs=[pl.BlockSpec((tm, d), lambda i, k: (i, 0)),
                  pl.BlockSpec((tm, ka), lambda i, k: (i, 0)),
                  pl.BlockSpec((ka, d), lambda i, k: (0, 0)),
                  spec(g1.shape),
                  pl.BlockSpec((1, d), lambda i, k: (0, 0)),
                  spec(shift.shape), spec(scale.shape), spec(g2.shape),
                  pl.BlockSpec((d, tf), lambda i, k: (0, k)),
                  pl.BlockSpec((tf, d), lambda i, k: (k, 0)),
                  pl.BlockSpec((1, d), lambda i, k: (0, 0))],
        out_specs=pl.BlockSpec((tm, d), lambda i, k: (i, 0)),
        scratch_shapes=[pltpu.VMEM((tm, d), F32), pltpu.VMEM((tm, d), BF16), pltpu.VMEM((tm, d), F32)],
        compiler_params=_cparams(("parallel", "arbitrary")),
        name="post_mlp",
    )(x, a, wo, g1, nw, shift, scale, g2, w1, w2, fnw)


def _head_expand_matrix():
    e = np.zeros((LANES, SSD_D_INNER), np.float32)
    for h in range(SSD_HEADS):
        e[h, h * SSD_P:(h + 1) * SSD_P] = 1.0
    return jnp.asarray(e, BF16)


def _ssd_prompt_kernel(z_ref, x_ref, bc_ref, dt_ref, cwx_ref, cbx_ref, cwb_ref, cbb_ref, dtb_ref, alog_ref,
                       dskip_ref, nw_ref, exp_ref, tri_ref,
                       y_ref, conv_ref, ssm_ref, xs_ref, bs_ref, st_ref):
    c = pl.program_id(1)
    L = SSD_CHUNK
    tail = SSD_CONV_W - 1

    @pl.when(c == 0)
    def _():
        xs_ref[0:8, :] = jnp.zeros((8, SSD_D_INNER), F32)
        bs_ref[0:8, :] = jnp.zeros((8, SSD_BC), F32)
        st_ref[...] = jnp.zeros_like(st_ref)

    xs_ref[8:8 + L, :] = x_ref[...]
    bs_ref[8:8 + L, :] = bc_ref[...]

    def conv(buf, w_ref, b_ref):
        acc = b_ref[...] + buf[8 - tail:8 - tail + L, :] * w_ref[0:1, :]
        for k in range(1, SSD_CONV_W):
            acc = acc + buf[8 - tail + k:8 - tail + k + L, :] * w_ref[k:k + 1, :]
        return _silu(acc)

    x = conv(xs_ref, cwx_ref, cbx_ref)
    bcv = conv(bs_ref, cwb_ref, cbb_ref)
    xs_ref[8 - tail:8, :] = xs_ref[8 + L - tail:8 + L, :]
    bs_ref[8 - tail:8, :] = bs_ref[8 + L - tail:8 + L, :]

    @pl.when(c == pl.num_programs(1) - 1)
    def _():
        conv_ref[0, :, 0:SSD_D_INNER] = x_ref[L - tail:L, :]
        conv_ref[0, :, SSD_D_INNER:SSD_CONV_DIM] = bc_ref[L - tail:L, :]

    lane = lax.broadcasted_iota(jnp.int32, (1, LANES), 1)
    head_ok = lane < SSD_HEADS
    dt = jnp.where(head_ok, _softplus(dt_ref[...] + dtb_ref[...]), 0.0)
    a = jnp.where(head_ok, -jnp.exp(alog_ref[...]) * LOG2E, 0.0)
    acum = _dot_01x(tri_ref[...], dt * a)
    acum_t = acum.T
    a_last = acum[L - 1:L, :]
    expand = exp_ref[...]
    dt_e = _dot_x01(dt, expand)
    eac_e = _dot_x01(jnp.exp2(acum), expand)
    dend_e = _dot_x01(jnp.exp2(a_last - acum), expand)
    cdec_e = eac_e[L - 1:L, :]

    xdt = x * dt_e
    xdt_b = xdt.astype(BF16)
    xdtw_b = (xdt * dend_e).astype(BF16)
    row = lax.broadcasted_iota(jnp.int32, (L, L), 0)
    col = lax.broadcasted_iota(jnp.int32, (L, L), 1)
    causal = row >= col
    lane2 = lax.broadcasted_iota(jnp.int32, (L, 2 * SSD_P), 1)
    first_head = lane2 < SSD_P

    y_parts = []
    for g in range(SSD_GROUPS):
        b_g = bcv[:, g * SSD_N:(g + 1) * SSD_N]
        c_g = bcv[:, (SSD_GROUPS + g) * SSD_N:(SSD_GROUPS + g + 1) * SSD_N]
        b_gb = b_g.astype(BF16)
        c_gb = c_g.astype(BF16)
        cb = _dot_nt(c_gb, b_gb)
        gl = slice(g * SSD_REP * SSD_P, (g + 1) * SSD_REP * SSD_P)
        st_g = st_ref[:, gl]
        y_off = _dot(c_gb, st_g.astype(BF16)) * eac_e[:, gl]
        pair_out = []
        for j in range(SSD_REP // 2):
            pl_ = slice(g * SSD_REP * SSD_P + j * 2 * SSD_P, g * SSD_REP * SSD_P + (j + 1) * 2 * SSD_P)
            xp = xdt_b[:, pl_]
            ys = []
            for hh in range(2):
                h = g * SSD_REP + 2 * j + hh
                seg = acum[:, h:h + 1] - acum_t[h:h + 1, :]
                dec = jnp.exp2(jnp.where(causal, seg, NEG))
                ys.append(_dot((cb * dec).astype(BF16), xp))
            pair_out.append(jnp.where(first_head, ys[0], ys[1]))
        y_g = jnp.concatenate(pair_out, axis=1) + y_off
        y_parts.append(y_g)
        st_ref[:, gl] = st_g * cdec_e[:, gl] + _dot(b_g.T.astype(BF16), xdtw_b[:, gl])

    y = jnp.concatenate(y_parts, axis=1) + dskip_ref[...] * x
    zg = z_ref[...]
    y = y * _silu(zg)
    r = lax.rsqrt(jnp.mean(y * y, axis=-1, keepdims=True) + EPS)
    y_ref[...] = ((y * r) * nw_ref[...]).astype(BF16)

    @pl.when(c == pl.num_programs(1) - 1)
    def _():
        ssm_ref[0] = st_ref[...].T


def _ssd_prompt(proj, bsz, t, cw, cb, dtb, alog, dskip, nw):
    L = SSD_CHUNK
    nc = t // L
    di = SSD_D_INNER
    tri = jnp.asarray(np.tril(np.ones((L, L), np.float32)), BF16)
    full = lambda shape: pl.BlockSpec(shape, lambda b, c: (0,) * len(shape))
    return pl.pallas_call(
        _ssd_prompt_kernel,
        out_shape=(jax.ShapeDtypeStruct((bsz * t, di), BF16),
                   jax.ShapeDtypeStruct((bsz, SSD_CONV_W - 1, SSD_CONV_DIM), F32),
                   jax.ShapeDtypeStruct((bsz, di, SSD_N), F32)),
        grid=(bsz, nc),
        in_specs=[pl.BlockSpec((L, di), lambda b, c: (b * nc + c, 0)),
                  pl.BlockSpec((L, di), lambda b, c: (b * nc + c, 1)),
                  pl.BlockSpec((L, SSD_BC), lambda b, c: (b * nc + c, 2 * di // SSD_BC)),
                  pl.BlockSpec((L, LANES), lambda b, c: (b * nc + c, (2 * di + SSD_BC) // LANES)),
                  full((SSD_CONV_W, di)), full((1, di)), full((SSD_CONV_W, SSD_BC)), full((1, SSD_BC)),
                  full((1, LANES)), full((1, LANES)), full((1, di)), full((1, di)),
                  full((LANES, di)), full((L, L))],
        out_specs=(pl.BlockSpec((L, di), lambda b, c: (b * nc + c, 0)),
                   pl.BlockSpec((1, SSD_CONV_W - 1, SSD_CONV_DIM), lambda b, c: (b, 0, 0)),
                   pl.BlockSpec((1, di, SSD_N), lambda b, c: (b, 0, 0))),
        scratch_shapes=[pltpu.VMEM((8 + L, di), F32), pltpu.VMEM((8 + L, SSD_BC), F32),
                        pltpu.VMEM((SSD_N, di), F32)],
        compiler_params=_cparams(("parallel", "arbitrary")),
        name="ssd_prompt",
    )(proj, proj, proj, proj, cw[:, :di], cb[:, :di], cw[:, di:], cb[:, di:], dtb, alog, dskip, nw,
      _head_expand_matrix(), tri)


def _ssd_sample_kernel(p_ref, cs_ref, st_ref, cw_ref, cb_ref, dtb_ref, alog_ref, dskip_ref, nw_ref, exp_ref, *rest):
    y_ref, conv_ref, ssm_ref = rest[-3:]
    di = SSD_D_INNER
    z = p_ref[0, :, 0:di]
    xbc = p_ref[0, :, di:di + SSD_CONV_DIM]
    dtr = p_ref[0, :, di + SSD_CONV_DIM:di + SSD_CONV_DIM + LANES]
    cs = cs_ref[0]
    acc = cb_ref[...] + xbc * cw_ref[SSD_CONV_W - 1:SSD_CONV_W, :]
    for k in range(SSD_CONV_W - 1):
        acc = acc + cs[k:k + 1, :] * cw_ref[k:k + 1, :]
    conv_ref[0, 0:SSD_CONV_W - 2, :] = cs[1:SSD_CONV_W - 1, :]
    conv_ref[0, SSD_CONV_W - 2:SSD_CONV_W - 1, :] = xbc
    act = _silu(acc)
    x = act[:, 0:di]
    lane = lax.broadcasted_iota(jnp.int32, (1, LANES), 1)
    head_ok = lane < SSD_HEADS
    dt = jnp.where(head_ok, _softplus(dtr + dtb_ref[...]), 0.0)
    a = jnp.where(head_ok, -jnp.exp(alog_ref[...]), 0.0)
    da = jnp.exp(dt * a)
    lhs = jnp.concatenate([dt, da, jnp.zeros((6, LANES), F32)], axis=0)
    ex = _dot_x01(lhs, exp_ref[...])
    dt_e, da_e = ex[0:1, :], ex[1:2, :]
    xdt = x * dt_e

    eye = (lax.broadcasted_iota(jnp.int32, (LANES, LANES), 0)
           == lax.broadcasted_iota(jnp.int32, (LANES, LANES), 1))

    def to_col(rowvec):
        return jnp.sum(jnp.where(eye, jnp.broadcast_to(rowvec, (LANES, LANES)), 0.0), axis=1, keepdims=True)

    def to_row(colvec):
        return jnp.sum(jnp.where(eye, jnp.broadcast_to(colvec, (LANES, LANES)), 0.0), axis=0, keepdims=True)

    y_off = []
    cbs = []
    for g in range(SSD_GROUPS):
        b_g = act[:, di + g * SSD_N:di + (g + 1) * SSD_N]
        c_g = act[:, di + (SSD_GROUPS + g) * SSD_N:di + (SSD_GROUPS + g + 1) * SSD_N]
        cbs.append(jnp.broadcast_to(jnp.sum(b_g * c_g, axis=1, keepdims=True), (1, SSD_REP * SSD_P)))
        for i in range(SSD_REP * SSD_P // LANES):
            lo = g * SSD_REP * SSD_P + i * LANES
            st = st_ref[0, 0, lo:lo + LANES, :]
            xcol = to_col(xdt[:, lo:lo + LANES])
            dcol = to_col(da_e[:, lo:lo + LANES])
            ssm_ref[0, 0, lo:lo + LANES, :] = st * dcol + xcol * b_g
            y_off.append(to_row(jnp.sum(st * c_g, axis=1, keepdims=True)))
    y = xdt * jnp.concatenate(cbs, axis=1) + jnp.concatenate(y_off, axis=1) * da_e + dskip_ref[...] * x
    y = y * _silu(z)
    r = lax.rsqrt(jnp.mean(y * y, axis=-1, keepdims=True) + EPS)
    y_ref[0] = ((y * r) * nw_ref[...]).astype(BF16)


def _ssd_sample(proj, conv_state, ssm_states, layer, ssm_out_prev, cw, cb, dtb, alog, dskip, nw):
    n_layers, bsz = ssm_states.shape[:2]
    di = SSD_D_INNER
    full = lambda shape: pl.BlockSpec(shape, lambda b: (0,) * len(shape))
    per_b = lambda shape: pl.BlockSpec((1,) + shape, lambda b: (b,) + (0,) * len(shape))
    state_spec = pl.BlockSpec((1, 1, di, SSD_N), lambda b: (layer, b, 0, 0))
    in_specs = [per_b((1, SSD_NP)), per_b((SSD_CONV_W - 1, SSD_CONV_DIM)), state_spec,
                full((SSD_CONV_W, SSD_CONV_DIM)), full((1, SSD_CONV_DIM)), full((1, LANES)), full((1, LANES)),
                full((1, di)), full((1, di)), full((LANES, di))]
    args = [proj.reshape(bsz, 1, SSD_NP), conv_state, ssm_states.reshape(n_layers, bsz, di, SSD_N), cw, cb, dtb, alog,
            dskip, nw, _head_expand_matrix()]
    aliases = {}
    if ssm_out_prev is not None:
        in_specs.append(pl.BlockSpec(memory_space=pl.ANY))
        aliases = {len(args): 2}
        args.append(ssm_out_prev)
    y, conv, ssm = pl.pallas_call(
        _ssd_sample_kernel,
        out_shape=(jax.ShapeDtypeStruct((bsz, 1, di), BF16),
                   jax.ShapeDtypeStruct((bsz, SSD_CONV_W - 1, SSD_CONV_DIM), F32),
                   jax.ShapeDtypeStruct((n_layers, bsz, di, SSD_N), F32)),
        grid=(bsz,),
        in_specs=in_specs,
        out_specs=(per_b((1, di)), per_b((SSD_CONV_W - 1, SSD_CONV_DIM)), state_spec),
        input_output_aliases=aliases,
        compiler_params=_cparams(("parallel",)),
        name="ssd_sample",
    )(*args)
    return y.reshape(bsz, di), conv, ssm


CH = CMP_STRIDE
N_L = CMP_BLOCK // 2
PE_ROWS = 16


def _compress_accumulate(lhs_fn, pe_ref, wab_ref, kv):
    acc = None
    for l in range(N_L):
        lhs = jnp.concatenate([lhs_fn(l, kv).astype(BF16), pe_ref[kv, l]], axis=0)
        t = _dot(lhs, wab_ref[kv, l])
        acc = t if acc is None else acc + t
    return acc


def _compress_finish(acc, w2, nch):
    p = acc[0:nch, 0:GW]
    q_next = pltpu.roll(acc[0:nch, GW:2 * GW], nch - 1, axis=0)
    bias = acc[nch:nch + 1, 0:GW] + acc[nch + 8:nch + 9, GW:2 * GW]
    out = _dot(_silu(p + q_next + bias).astype(BF16), w2)
    row = lax.broadcasted_iota(jnp.int32, (nch, 1), 0)
    return jnp.where(row < nch - 1, out, 0.0)


def _softmax_masked(s, ok, axis, exp=jnp.exp):
    s = jnp.where(ok, s, NEG)
    m = jnp.max(s, axis=axis, keepdims=True)
    p = jnp.where(ok, exp(s - m), 0.0)
    d = jnp.sum(p, axis=axis, keepdims=True)
    return p * jnp.where(d > 0.0, 1.0 / d, 0.0)


def _interleave(n, scores, softmax, values, finish):
    out, pending = [], None
    s_next = scores(0)
    for k in range(n):
        s = s_next
        if k + 1 < n:
            s_next = scores(k + 1)
        p, aux = softmax(k, s)
        if pending is not None:
            out.append(finish(*pending))
        pending = (k, values(k, p), aux)
    out.append(finish(*pending))
    return out


def _topk_mask(score, j, width, shifts, axis):
    n = score.shape[axis]
    rank = jnp.zeros(score.shape, F32)
    for k in shifts:
        other = pltpu.roll(score, k, axis=axis)
        lower = j >= k
        if width != n:
            other = jnp.where(lower, other, pltpu.roll(score, n - width + k, axis=axis))
        rank = rank + jnp.where(lower, jnp.where(other >= score, 1.0, 0.0), jnp.where(other > score, 1.0, 0.0))
    return rank < SEL_TOPK


def _overlap(n_cmp, n_sel):
    c_start = np.arange(n_cmp) * CMP_STRIDE
    s_start = np.arange(n_sel) * SEL_BLOCK
    return ((c_start[:, None] < s_start[None, :] + SEL_BLOCK)
            & (c_start[:, None] + CMP_BLOCK > s_start[None, :])).astype(np.float32)


def _block_bias_matrix(width, n_keys):
    l = np.arange(LANES)[:, None] % width
    blk = np.arange(n_keys)[None, :] // SEL_BLOCK
    return np.where(l == blk, NEG_MXU, 0.0).astype(np.float32)


def _nsa_proj_kernel(x_ref, nw_ref, sh_ref, sc_ref, wr_ref, wt_ref, *rest):
    rows_ref, kpad_ref, qt_ref, gt_ref, vt_ref, cmp_ref, sel_ref, win_ref = rest[-8:]
    h = _modulated_norm(x_ref[...], nw_ref[...], sh_ref[0], sc_ref[0]).astype(BF16)
    y = _dot(h, wr_ref[...])
    rows_ref[...] = y[:, 0:KVW].astype(BF16)
    kpad_ref[...] = y[:, KVW:].astype(BF16)
    yt = _dot_nt(wt_ref[...], h)
    qt_ref[0] = (yt[0:1024] * (HD ** -0.5 * LOG2E)).astype(BF16)
    cmp_ref[0, 0] = yt[1024:1024 + KVW]
    sel_ref[0, 0] = yt[1024 + KVW:1024 + 2 * KVW]
    vt_ref[0, 0:GW] = yt[1024 + KVW + GW:1024 + 2 * KVW].astype(BF16)
    vt_ref[0, GW:2 * GW] = yt[1024 + 2 * KVW + GW:1024 + 3 * KVW].astype(BF16)
    gt_ref[0] = _sigmoid(yt[1024 + 3 * KVW:NSA_N])

    @pl.when(pl.program_id(1) == pl.num_programs(1) - 1)
    def _():
        win_ref[0, 0] = yt[1024 + 2 * KVW:1024 + 3 * KVW]


def _nsa_proj(x, nw, shift, scale, wr, wt, bsz, t, layer, n_layers, kv_prev):
    d = x.shape[1]
    tm = min(WINDOW, t)
    nb = t // tm
    row = lambda w: pl.BlockSpec((tm, w), lambda b, i: (b * nb + i, 0))
    col = lambda h: pl.BlockSpec((1, h, tm), lambda b, i: (b, 0, i))
    kv_col = pl.BlockSpec((1, 1, KVW, tm), lambda b, i: (layer, b, 0, i))
    mod = lambda: pl.BlockSpec((1, 1, d), lambda b, i: (b, 0, 0))
    nq = NSA_HEADS * HD
    in_specs = [row(d), pl.BlockSpec((1, d), lambda b, i: (0, 0)), mod(), mod(),
                pl.BlockSpec((d, KVW + 2 * KPAD), lambda b, i: (0, 0)),
                pl.BlockSpec((NSA_N, d), lambda b, i: (0, 0))]
    args = [x, nw, shift, scale, wr, wt]
    aliases = {}
    if kv_prev is not None:
        aliases = {len(args) + k: 5 + k for k in range(3)}
        in_specs += [pl.BlockSpec(memory_space=pl.ANY)] * 3
        args += list(kv_prev)
    return pl.pallas_call(
        _nsa_proj_kernel,
        out_shape=(jax.ShapeDtypeStruct((bsz * t, KVW), BF16),
                   jax.ShapeDtypeStruct((bsz * t, 2 * KPAD), BF16),
                   jax.ShapeDtypeStruct((bsz, nq, t), BF16),
                   jax.ShapeDtypeStruct((bsz, LANES, t), F32),
                   jax.ShapeDtypeStruct((bsz, 2 * GW, t), BF16),
                   jax.ShapeDtypeStruct((n_layers, bsz, KVW, t), F32),
                   jax.ShapeDtypeStruct((n_layers, bsz, KVW, t), F32),
                   jax.ShapeDtypeStruct((n_layers, bsz, KVW, tm), F32)),
        grid=(bsz, nb),
        in_specs=in_specs,
        out_specs=(row(KVW), row(2 * KPAD), col(nq), col(LANES), col(2 * GW), kv_col, kv_col,
                   pl.BlockSpec((1, 1, KVW, tm), lambda b, i: (layer, b, 0, 0))),
        input_output_aliases=aliases,
        compiler_params=_cparams(("parallel", "arbitrary")),
        name="nsa_proj",
    )(*args)


def _cmp_prompt_kernel(rows_ref, perm_ref, pe_ref, wab_ref, w2k_ref, w2v_ref, kc_ref, vc_ref, xr_ref):
    nch = rows_ref.shape[1] // CH
    cpp = PAGE_SIZE // CH
    perm = perm_ref[...]
    for p in range(rows_ref.shape[1] // PAGE_SIZE):
        rows = _dot(perm, rows_ref[0, p * PAGE_SIZE:(p + 1) * PAGE_SIZE, :])
        for l in range(N_L):
            xr_ref[l, p * cpp:(p + 1) * cpp, :] = rows[l * cpp:(l + 1) * cpp, :]
    lhs = lambda l, kv: xr_ref[l, :, kv * GW:(kv + 1) * GW]
    kc_ref[0] = _compress_finish(_compress_accumulate(lhs, pe_ref, wab_ref, 0), w2k_ref[...], nch).astype(BF16)
    vc_ref[0] = _compress_finish(_compress_accumulate(lhs, pe_ref, wab_ref, 1), w2v_ref[...], nch).T.astype(BF16)


def _chunk_perm():
    cpp = PAGE_SIZE // CH
    perm = np.zeros((PAGE_SIZE, PAGE_SIZE), np.float32)
    for l in range(N_L):
        for c in range(cpp):
            perm[l * cpp + c, c * CH + l] = 1.0
    return jnp.asarray(perm, BF16)


def _cmp_prompt(rows, pe_t, wab, w2k_pad, w2v):
    bsz, t, _ = rows.shape
    nch = t // CH
    full = lambda shape: pl.BlockSpec(shape, lambda b: (0,) * len(shape))
    return pl.pallas_call(
        _cmp_prompt_kernel,
        out_shape=(jax.ShapeDtypeStruct((bsz, nch, KPAD), BF16), jax.ShapeDtypeStruct((bsz, GW, nch), BF16)),
        grid=(bsz,),
        in_specs=[pl.BlockSpec((1, t, KVW), lambda b: (b, 0, 0)), full((PAGE_SIZE, PAGE_SIZE)),
                  full(pe_t.shape), full(wab.shape), full(w2k_pad.shape), full(w2v.shape)],
        out_specs=(pl.BlockSpec((1, nch, KPAD), lambda b: (b, 0, 0)), pl.BlockSpec((1, GW, nch), lambda b: (b, 0, 0))),
        scratch_shapes=[pltpu.VMEM((N_L, nch, KVW), F32)],
        compiler_params=_cparams(("parallel",)),
        name="nsa_compress_prompt",
    )(rows, _chunk_perm(), pe_t, wab, w2k_pad, w2v)


def _nsa_prompt_kernel(qt_ref, gt_ref, kc_ref, vct_ref, kpad_ref, vt_ref, ovt_ref, nege_ref, o_ref,
                       *, tq, n_sel, ck):
    i = pl.program_id(1)
    t0 = i * tq
    nq = REP * tq
    tl = t0 + lax.broadcasted_iota(jnp.int32, (1, tq), 1)
    zq = jnp.zeros((HD, nq), BF16)

    def lanes4(a):
        return jnp.concatenate([a] * REP, axis=1)

    def with_ones(vt):
        return jnp.concatenate([vt, jnp.ones((ONES_ROWS, vt.shape[1]), BF16)], axis=0)

    def normalized(acc):
        return acc[0:HD] * (1.0 / acc[HD:HD + 1])

    tl4 = lanes4(tl)
    qg = [jnp.concatenate([qt_ref[0, (g * REP + r) * HD:(g * REP + r + 1) * HD, :] for r in range(REP)], axis=1)
          for g in range(KVH)]
    q_rhs = [jnp.concatenate([q, zq], axis=0) for q in qg]

    ncp = kc_ref.shape[1]
    cend = lax.broadcasted_iota(jnp.int32, (ncp, 1), 0) * CMP_STRIDE + (CMP_BLOCK - 1)
    ok_cmp = cend <= tl4
    s_cmp = [_dot(kc_ref[0, :, g * LANES:(g + 1) * LANES], q_rhs[g]) for g in range(KVH)]
    p_cmp = [_softmax_masked(s, ok_cmp, 0, jnp.exp2) for s in s_cmp]
    o_cmp = [_dot(vct_ref[0, g * HD:(g + 1) * HD, :], p_cmp[g].astype(BF16)) for g in range(KVH)]
    imp = jnp.zeros((LANES, tq), F32)
    for g, p in enumerate(p_cmp):
        psum = p[:, 0:tq] + p[:, tq:2 * tq] + p[:, 2 * tq:3 * tq] + p[:, 3 * tq:4 * tq]
        imp = imp + _dot_01x(ovt_ref[g], psum)

    row = lax.broadcasted_iota(jnp.int32, (LANES, tq), 0)
    jj = row % n_sel
    cur = tl // SEL_BLOCK
    valid = jj * SEL_BLOCK <= tl
    forced = (jj == 0) | (jj == cur) | (jj == cur - 1)
    score = jnp.where(valid, jnp.where(forced, BIG, imp), NEG)
    top = _topk_mask(score, jj, n_sel, range(1, n_sel), 0)
    blocked = jnp.where(top & valid, 0.0, 1.0)

    n_full = t0 // ck
    sel_rhs = [jnp.concatenate([lanes4(jnp.where(row // n_sel == g, blocked, 0.0).astype(BF16)), q_rhs[g]], axis=0)
               for g in range(KVH)]

    def sel_chunk(c, carries, causal):
        k0 = pl.multiple_of(c * ck, ck)
        bias_rows = nege_ref[pl.ds(k0, ck), :]
        if causal:
            cbias = jnp.where(k0 + lax.broadcasted_iota(jnp.int32, (ck, 1), 0) <= tl4, 0.0, NEG)

        def scores(g):
            lhs = jnp.concatenate([bias_rows, kpad_ref[0, pl.ds(k0, ck), g * LANES:(g + 1) * LANES]], axis=1)
            s = _dot(lhs, sel_rhs[g])
            return s + cbias if causal else s

        def softmax(g, s):
            m, acc = carries[g]
            m_new = jnp.maximum(m, jnp.max(s, axis=0, keepdims=True))
            return jnp.exp2(s - m_new).astype(BF16), (m_new, jnp.exp2(m - m_new), acc)

        def values(g, p):
            return _dot(with_ones(vt_ref[0, g * HD:(g + 1) * HD, pl.ds(k0, ck)]), p)

        def finish(g, pv, aux):
            m_new, alpha, acc = aux
            return m_new, alpha * acc + pv

        return tuple(_interleave(KVH, scores, softmax, values, finish))

    init = (jnp.full((1, nq), NEG, F32), jnp.zeros((HD + ONES_ROWS, nq), F32))
    carries = lax.fori_loop(0, n_full, lambda c, cr: sel_chunk(c, cr, False), (init,) * KVH)
    o_sel = [normalized(acc) for _, acc in sel_chunk(n_full, carries, True)]

    kw = WINDOW + tq
    k_start = pl.multiple_of(jnp.maximum(t0 - WINDOW, 0), LANES)
    dpos = tl - (k_start + lax.broadcasted_iota(jnp.int32, (kw, 1), 0))
    wbias = lanes4(jnp.where((dpos >= 0) & (dpos <= WINDOW), 0.0, NEG))

    def win_scores(g):
        return _dot(kpad_ref[0, pl.ds(k_start, kw), (KVH + g) * LANES:(KVH + g + 1) * LANES], q_rhs[g]) + wbias

    def win_softmax(g, s):
        return jnp.exp2(s - jnp.max(s, axis=0, keepdims=True)).astype(BF16), None

    def win_values(g, p):
        return _dot(with_ones(vt_ref[0, GW + g * HD:GW + (g + 1) * HD, pl.ds(k_start, kw)]), p)

    o_win = _interleave(KVH, win_scores, win_softmax, win_values, lambda g, pv, aux: normalized(pv))

    gt = gt_ref[0]
    outs = []
    for g in range(KVH):
        for r in range(REP):
            h = g * REP + r
            sl = slice(r * tq, (r + 1) * tq)
            outs.append(gt[3 * h:3 * h + 1, :] * o_cmp[g][:, sl] + gt[3 * h + 1:3 * h + 2, :] * o_sel[g][:, sl]
                        + gt[3 * h + 2:3 * h + 3, :] * o_win[g][:, sl])
    o_ref[0] = jnp.concatenate(outs, axis=0).T.astype(BF16)


def _nsa_prompt_attention(qt, gt, kc, vct, kpad, vt):
    bsz, nd, t = qt.shape
    tq = LANES
    ck = 4 * LANES
    n_sel = t // SEL_BLOCK
    ncp = t // CH
    assert KVH * n_sel == LANES and ncp == LANES and t % ck == 0
    ovt = np.zeros((KVH, LANES, ncp), np.float32)
    for g in range(KVH):
        ovt[g, g * n_sel:(g + 1) * n_sel, :ncp - 1] = _overlap(ncp - 1, n_sel).T
    nege_t = _block_bias_matrix(n_sel, t).T
    per_b = lambda shape: pl.BlockSpec((1,) + shape, lambda b, i: (b,) + (0,) * len(shape))
    full = lambda shape: pl.BlockSpec(shape, lambda b, i: (0,) * len(shape))
    out = pl.pallas_call(
        functools.partial(_nsa_prompt_kernel, tq=tq, n_sel=n_sel, ck=ck),
        out_shape=jax.ShapeDtypeStruct((bsz, t, nd), BF16),
        grid=(bsz, t // tq),
        in_specs=[pl.BlockSpec((1, nd, tq), lambda b, i: (b, 0, i)),
                  pl.BlockSpec((1, LANES, tq), lambda b, i: (b, 0, i)),
                  per_b((ncp, KPAD)), per_b((GW, ncp)), per_b((t, 2 * KPAD)), per_b((2 * GW, t)),
                  full((KVH, LANES, ncp)), full((t, LANES))],
        out_specs=pl.BlockSpec((1, tq, nd), lambda b, i: (b, i, 0)),
        compiler_params=_cparams(("parallel", "arbitrary")),
        name="nsa_prompt_attention",
    )(qt, gt, kc, vct, kpad.reshape(bsz, t, 2 * KPAD), vt, jnp.asarray(ovt, BF16), jnp.asarray(nege_t, BF16))
    return out.reshape(bsz * t, nd)


def _nsa_sample_kernel(pt_ref, p_ref, *rest, n_pages, past):
    del pt_ref
    cmp_pages = rest[:n_pages]
    sel_pages = rest[n_pages:2 * n_pages]
    win_ref, pe_ref, wab_ref, w2_ref, ov_ref, nege_ref, perm_ref = rest[2 * n_pages:2 * n_pages + 7]
    o_ref, wout_ref, xr_ref = rest[-3:]
    nch = n_pages * (PAGE_SIZE // CH)
    cpp = PAGE_SIZE // CH
    t = past
    keep = win_ref.shape[-1]
    rows_q = 8

    def compress_cache():
        perm = perm_ref[...]
        for p, pg in enumerate(cmp_pages):
            for kv in range(2):
                rows = _dot_nt(perm, pg[0, 0, kv].reshape(GW, PAGE_SIZE).astype(BF16))
                for l in range(N_L):
                    xr_ref[kv, l, p * cpp:(p + 1) * cpp, :] = rows[l * cpp:(l + 1) * cpp, :]
        out = []
        for kv in range(2):
            acc = _compress_accumulate(lambda l, kv: xr_ref[kv, l], pe_ref, wab_ref, kv)
            out.append(_compress_finish(acc, w2_ref[kv], nch).T.astype(BF16))
        return out

    lane = lax.broadcasted_iota(jnp.int32, (rows_q, LANES), 1)
    rowi = lax.broadcasted_iota(jnp.int32, (rows_q, LANES), 0)
    cend = lax.broadcasted_iota(jnp.int32, (1, nch), 1) * CMP_STRIDE + (CMP_BLOCK - 1)
    n_sel = -(-(t + 1) // SEL_BLOCK)
    nj = max(n_sel, SEL_TOPK)
    cur = t // SEL_BLOCK
    valid = (lane < n_sel) & (lane * SEL_BLOCK <= t)
    forced = (lane == 0) | (lane == cur) | (lane == cur - 1)
    shifts = list(range(1, nj)) + list(range(LANES - nj + 1, LANES))
    gates = jnp.broadcast_to(_sigmoid(p_ref[0, :, 1024 + 3 * KVW:NSA_N]), (rows_q, LANES))
    eye = (lax.broadcasted_iota(jnp.int32, (HD, HD), 0) == lax.broadcasted_iota(jnp.int32, (HD, HD), 1))
    last_lane = lax.broadcasted_iota(jnp.int32, (HD, keep), 1) == keep - 1

    G = range(KVH)
    new_row = lambda off, g: p_ref[0, :, off + g * HD:off + (g + 1) * HD]
    q = [jnp.concatenate([p_ref[0, :, (g * REP + r) * HD:(g * REP + r + 1) * HD] for r in range(REP)]
                         + [jnp.zeros((rows_q - REP, HD), F32)], axis=0) * (HD ** -0.5) for g in G]
    qb = [x.astype(BF16) for x in q]

    def softmax_with_new(s, s_new, ok_new):
        s_new = jnp.where(ok_new, s_new, NEG)
        m = jnp.maximum(jnp.max(s, axis=1, keepdims=True), s_new)
        pr = jnp.exp(s - m)
        pr_new = jnp.where(ok_new, jnp.exp(s_new - m), 0.0)
        return pr.astype(BF16), pr_new, 1.0 / (jnp.sum(pr, axis=1, keepdims=True) + pr_new)

    kw = [win_ref[0, 0, 0, g] for g in G]
    vw = [win_ref[0, 0, 1, g] for g in G]
    kw_new = [new_row(1024 + 2 * KVW, g) for g in G]
    vw_new = [new_row(1024 + 2 * KVW + GW, g) for g in G]
    s_win = [_dot(qb[g], kw[g].astype(BF16)) for g in G]
    sm = [softmax_with_new(s_win[g], jnp.sum(q[g] * kw_new[g], axis=1, keepdims=True), jnp.full((rows_q, 1), True))
          for g in G]
    pv = [_dot_nt(sm[g][0], vw[g].astype(BF16)) for g in G]
    o_win = [(pv[g] + sm[g][1] * vw_new[g]) * sm[g][2] for g in G]
    for g in G:
        for kv, old, new in ((0, kw[g], kw_new[g]), (1, vw[g], vw_new[g])):
            col = jnp.sum(jnp.where(eye, jnp.broadcast_to(new, (HD, HD)), 0.0), axis=1, keepdims=True)
            wout_ref[0, 0, kv, g] = jnp.where(last_lane, col, pltpu.roll(old, keep - 1, axis=1))

    kt = [jnp.concatenate([pg[0, 0, 0, g] for pg in sel_pages], axis=1).astype(BF16) for g in G]
    vt = [jnp.concatenate([pg[0, 0, 1, g] for pg in sel_pages], axis=1).astype(BF16) for g in G]

    kct, vct = compress_cache()
    s_cmp = [_dot(qb[g], kct[g * HD:(g + 1) * HD, :]) for g in G]
    p_cmp = [_softmax_masked(s, cend <= t, 1) for s in s_cmp]
    o_cmp = [_dot_nt(p_cmp[g].astype(BF16), vct[g * HD:(g + 1) * HD, :]) for g in G]
    allowed = []
    for g in G:
        psum = jnp.broadcast_to(jnp.sum(p_cmp[g][0:REP], axis=0, keepdims=True), (rows_q, nch))
        imp = _dot_x01(psum, ov_ref[...])
        score = jnp.where(valid, jnp.where(forced, BIG, imp), NEG)
        score = jnp.where(lane < nj, score, -3e38)
        allowed.append(_topk_mask(score, lane, LANES, shifts, 1) & valid)

    nege = nege_ref[...]
    s_sel = [_dot(jnp.concatenate([jnp.where(allowed[g], 0.0, 1.0).astype(BF16), qb[g]], axis=1),
                  jnp.concatenate([nege, kt[g]], axis=0)) for g in G]
    v_new = [new_row(1024 + KVW + GW, g) for g in G]
    sm = [softmax_with_new(s_sel[g], jnp.sum(q[g] * new_row(1024 + KVW, g), axis=1, keepdims=True),
                           jnp.sum(jnp.where((lane == cur) & allowed[g], 1.0, 0.0), axis=1, keepdims=True) > 0.5)
          for g in G]
    pv = [_dot_nt(sm[g][0], vt[g]) for g in G]
    o_sel = [(pv[g] + sm[g][1] * v_new[g]) * sm[g][2] for g in G]

    for g in G:
        o = None
        for br, ob in enumerate((o_cmp[g], o_sel[g], o_win[g])):
            gcol = jnp.sum(jnp.where(lane == (g * REP + rowi) * 3 + br, gates, 0.0), axis=1, keepdims=True)
            o = gcol * ob if o is None else o + gcol * ob
        for r in range(REP):
            h = g * REP + r
            o_ref[0, :, h * HD:(h + 1) * HD] = o[r:r + 1, :].astype(BF16)


def _nsa_sample(proj, cmp_view, sel_view, win_view, layer, win_out_prev, page_table, pe_t, wab, w2bd):
    n_layers = win_view.shape[0]
    bsz, n_pages = page_table.shape
    past = n_pages * PAGE_SIZE
    keep = win_view.shape[-1]
    nch = past // CH
    assert nch == LANES and keep <= WINDOW and past % SEL_BLOCK == 0 and past - keep >= 0
    n_sel = -(-(past + 1) // SEL_BLOCK)
    ov = np.zeros((nch, LANES), np.float32)
    ov[:nch - 1, :n_sel] = _overlap(nch - 1, n_sel)
    full = lambda shape: pl.BlockSpec(shape, lambda b, pt: (0,) * len(shape))
    once = lambda shape: pl.BlockSpec(shape, lambda b, pt: (0,) * len(shape), pipeline_mode=pl.Buffered(1))
    per_b = lambda shape: pl.BlockSpec((1,) + shape, lambda b, pt: (b,) + (0,) * len(shape))
    page_shape = (2, KVH, HD, PAGE_SIZE)
    page = lambda p: pl.BlockSpec((1, 1) + page_shape, lambda b, pt: (layer, pt[b * n_pages + p], 0, 0, 0, 0))
    win_shape = (2, KVH, HD, keep)
    win_spec = pl.BlockSpec((1, 1) + win_shape, lambda b, pt: (layer, b, 0, 0, 0, 0))
    in_specs = ([per_b((1, NSA_N))] + [page(p) for p in range(n_pages)] * 2
                + [win_spec, full(pe_t.shape), once(wab.shape), full(w2bd.shape), full((nch, LANES)),
                   full((LANES, past)), full((PAGE_SIZE, PAGE_SIZE))])
    args = [page_table.reshape(-1), proj.reshape(bsz, 1, NSA_N)] + [cmp_view] * n_pages + [sel_view] * n_pages + [
        win_view, pe_t, wab, w2bd, jnp.asarray(ov, BF16), jnp.asarray(_block_bias_matrix(LANES, past), BF16),
        _chunk_perm()]
    aliases = {}
    if win_out_prev is not None:
        in_specs.append(pl.BlockSpec(memory_space=pl.ANY))
        aliases = {len(args): 1}
        args.append(win_out_prev)
    o, wout = pl.pallas_call(
        functools.partial(_nsa_sample_kernel, n_pages=n_pages, past=past),
        out_shape=(jax.ShapeDtypeStruct((bsz, 1, NSA_HEADS * HD), BF16),
                   jax.ShapeDtypeStruct((n_layers, bsz) + win_shape, F32)),
        grid_spec=pltpu.PrefetchScalarGridSpec(
            num_scalar_prefetch=1, grid=(bsz,), in_specs=in_specs,
            out_specs=(per_b((1, NSA_HEADS * HD)), win_spec),
            scratch_shapes=[pltpu.VMEM((2, N_L, nch, GW), F32)]),
        input_output_aliases=aliases,
        compiler_params=_cparams(("arbitrary",)),
        name="nsa_sample",
    )(*args)
    return o.reshape(bsz, NSA_HEADS * HD), wout


def _prep_nsa(w_in, w_out, pe, w1, w2):
    n_gate = 3 * NSA_HEADS
    w_full = jnp.concatenate([w_in, jnp.zeros((D_MODEL, LANES - n_gate), F32)], axis=1)

    def k_padded(lo):
        k = w_in[:, lo:lo + GW].reshape(D_MODEL, KVH, HD)
        return jnp.concatenate([k, jnp.zeros_like(k)], axis=2).reshape(D_MODEL, KPAD)

    w_rows = jnp.concatenate([w_in[:, 1024:1024 + KVW], k_padded(1024 + KVW), k_padded(1024 + 2 * KVW)], axis=1)
    eye = jnp.eye(KVH, dtype=F32)
    w1r = w1.reshape(2, 2, N_L, HD, HD)
    wab = jnp.einsum('khlde,gf->klgdhfe', w1r, eye).reshape(2, N_L, GW, 2 * GW)
    w2bd = jnp.einsum('kde,gf->kgdfe', w2, eye)
    w2k_pad = jnp.concatenate([w2bd[0], jnp.zeros_like(w2bd[0])], axis=3).reshape(GW, KPAD)
    per = pe.reshape(2, 2, N_L, 1, 1, HD)
    pe_t = jnp.broadcast_to(per, (2, 2, N_L, 8, KVH, HD)).transpose(0, 2, 1, 3, 4, 5).reshape(2, N_L, PE_ROWS, GW)
    b = lambda a: a.astype(BF16)
    return dict(w_sample=b(w_full), w_rows=b(w_rows), w_t=b(w_full.T), wo=b(w_out), pe_t=b(pe_t), wab=b(wab),
                w2bd=b(w2bd.reshape(2, GW, GW)), w2k_pad=b(w2k_pad))


def _nsa_prompt_layer(x, nw, shift, scale, prep, bsz, t, layer, n_layers, kv_prev):
    rows, kpad, qt, gt, vt, *kv_t = _nsa_proj(x, nw, shift, scale, prep["w_rows"], prep["w_t"], bsz, t,
                                              layer, n_layers, kv_prev)
    kc, vct = _cmp_prompt(rows.reshape(bsz, t, KVW), prep["pe_t"], prep["wab"], prep["w2k_pad"],
                          prep["w2bd"][1])
    return _nsa_prompt_attention(qt, gt, kc, vct, kpad, vt), tuple(kv_t)


def _prep_ssd(w_in, dt_bias, a_log, d_skip):
    pad = SSD_NP - w_in.shape[1]
    w = jnp.concatenate([w_in, jnp.zeros((D_MODEL, pad), F32)], axis=1).astype(BF16)
    pad_h = lambda v: jnp.concatenate([v, jnp.zeros((LANES - SSD_HEADS,), F32)]).reshape(1, LANES)
    return w, pad_h(dt_bias), pad_h(a_log), jnp.repeat(d_skip, SSD_P).reshape(1, SSD_D_INNER)


def kernel(x_prompt, x_sample, cache_kv_cmp, cache_kv_sel, cache_kv_win, state_ssm, state_conv, page_table, c_prompt, c_sample, ada_w, ada_b, norm_w, mlp_w1, mlp_w2, nsa_w_in, nsa_w_out, nsa_cmp_pe, nsa_cmp_w1, nsa_cmp_w2, ssd_w_in, ssd_conv_w, ssd_conv_b, ssd_dt_bias, ssd_a_log, ssd_d, ssd_norm_w, ssd_w_out, final_norm_w):
    bp, t, d = x_prompt.shape
    bs = x_sample.shape[0]
    xp = x_prompt.reshape(bp * t, d)
    xs = x_sample.reshape(bs, d)
    mods = _adaln(jnp.concatenate([c_prompt, c_sample], axis=0), ada_w, ada_b)
    w1b = mlp_w1.astype(BF16)
    w2b = mlp_w2.astype(BF16)
    fnw = final_norm_w.reshape(1, d)
    tm = 512
    n_nsa, n_pool = cache_kv_cmp.shape[:2]
    cmp_view = jnp.transpose(cache_kv_cmp, (0, 1, 3, 4, 5, 2))
    sel_view = jnp.transpose(cache_kv_sel, (0, 1, 3, 4, 5, 2))
    win_view = jnp.transpose(cache_kv_win, (0, 1, 3, 4, 5, 2))
    win_s = ssm_s = kv_p = None
    outs = {k: [] for k in ("cmp_s", "sel_s", "ssm_p", "conv_p", "conv_s")}
    kv_rows = lambda a, n: a.reshape(n + (2, KVH, HD))

    for i in range(DEPTH):
        jl = i // 2
        mp = [mods[i, :bp, k * d:(k + 1) * d].reshape(bp, 1, d) for k in range(6)]
        ms = [mods[i, bp:, k * d:(k + 1) * d].reshape(1, bs, d) for k in range(6)]
        nw0 = norm_w[i, 0].reshape(1, d)
        nw1 = norm_w[i, 1].reshape(1, d)
        if i % 2 == 0:
            prep = _prep_nsa(nsa_w_in[jl], nsa_w_out[jl], nsa_cmp_pe[jl], nsa_cmp_w1[jl], nsa_cmp_w2[jl])
            wo = prep["wo"]
            ap, kv_p = _nsa_prompt_layer(xp, nw0, mp[0], mp[1], prep, bp, t, jl, n_nsa, kv_p)
            ps = _mod_matmul(xs, nw0, ms[0], ms[1], prep["w_sample"], tm=bs, tn=896, rows_per_mod=None)
            as_, win_s = _nsa_sample(ps, cmp_view, sel_view, win_view, jl, win_s, page_table, prep["pe_t"],
                                     prep["wab"], prep["w2bd"])
            outs["cmp_s"].append(kv_rows(ps[:, 1024:1024 + KVW], (bs, 1)))
            outs["sel_s"].append(kv_rows(ps[:, 1024 + KVW:1024 + 2 * KVW], (bs, 1)))
        else:
            w, dtb, alog, dsk = _prep_ssd(ssd_w_in[jl], ssd_dt_bias[jl], ssd_a_log[jl], ssd_d[jl])
            wo = ssd_w_out[jl].astype(BF16)
            snw = ssd_norm_w[jl].reshape(1, SSD_D_INNER)
            cw = ssd_conv_w[jl]
            cb = ssd_conv_b[jl].reshape(1, SSD_CONV_DIM)
            pp = _mod_matmul(xp, nw0, mp[0], mp[1], w, tm=tm, tn=SSD_NP, rows_per_mod=t)
            ps = _mod_matmul(xs, nw0, ms[0], ms[1], w, tm=bs, tn=896, rows_per_mod=None)
            ap, conv_p, ssm_p = _ssd_prompt(pp, bp, t, cw, cb, dtb, alog, dsk, snw)
            as_, conv_s, ssm_s = _ssd_sample(ps, state_conv[jl], state_ssm, jl, ssm_s, cw, cb, dtb, alog, dsk, snw)
            outs["conv_p"].append(conv_p)
            outs["conv_s"].append(conv_s)
            outs["ssm_p"].append(ssm_p.reshape(bp, SSD_HEADS, SSD_P, SSD_N))
        last = i == DEPTH - 1
        xp = _post_mlp(xp, ap, wo, mp[2], nw1, mp[3], mp[4], mp[5], w1b[i], w2b[i], fnw,
                       tm=tm, tf=1024, rows_per_mod=t, final_norm=last)
        xs = _post_mlp(xs, as_, wo, ms[2], nw1, ms[3], ms[4], ms[5], w1b[i], w2b[i], fnw,
                       tm=bs, tf=512, rows_per_mod=None, final_norm=last)

    st = lambda k: jnp.stack(outs[k])
    time_major = lambda a: jnp.transpose(a, (0, 1, 5, 2, 3, 4))
    cmp_p, sel_p, win_p = (time_major(a.reshape(a.shape[:2] + (2, KVH, HD, a.shape[-1]))) for a in kv_p)
    return (xp.reshape(bp, t, d), xs.reshape(bs, 1, d), cmp_p, st("cmp_s"), sel_p, st("sel_s"), win_p,
            time_major(win_s), st("ssm_p"), ssm_s.reshape(state_ssm.shape), st("conv_p"), st("conv_s"))
```

```python
import functools
import math

import numpy as np
import jax
import jax.numpy as jnp
from jax import lax
from jax.experimental import pallas as pl
from jax.experimental.pallas import tpu as pltpu

F32 = jnp.float32
BF16 = jnp.bfloat16

D_MODEL = 1024
DEPTH = 4
D_FF = 4 * D_MODEL
EPS = 1e-6
PAGE_SIZE = 128
NSA_HEADS = 16
HD = 64
KVH = 4
REP = 4
KVW = 2 * KVH * HD
GW = KVH * HD
CMP_BLOCK = 32
CMP_STRIDE = 16
SEL_BLOCK = 64
SEL_TOPK = 8
WINDOW = 512
SSD_D_INNER = 2 * D_MODEL
SSD_P = 64
SSD_HEADS = SSD_D_INNER // SSD_P
SSD_GROUPS = 4
SSD_REP = SSD_HEADS // SSD_GROUPS
SSD_N = 128
SSD_CONV_W = 4
SSD_CHUNK = 256
SSD_BC = 2 * SSD_GROUPS * SSD_N
SSD_CONV_DIM = SSD_D_INNER + SSD_BC
BIG = 1e30
NEG = -1e30
NEG_MXU = -2.0 ** 100
LOG2E = math.log2(math.e)
ONES_ROWS = 16
LANES = 128
NSA_N = 1024 + 3 * KVW + LANES
KPAD = KVH * LANES
SSD_NP = SSD_D_INNER + SSD_CONV_DIM + 2 * LANES
VMEM_LIMIT = 48 * 1024 * 1024


def _cparams(sem):
    return pltpu.CompilerParams(dimension_semantics=sem, vmem_limit_bytes=VMEM_LIMIT)


def _dot(a, b):
    return jnp.dot(a, b, preferred_element_type=F32)


def _dot_nt(a, b):
    return lax.dot_general(a, b, (((1,), (1,)), ((), ())), preferred_element_type=F32)


def _split(x, n):
    parts, r = [], x
    for i in range(n):
        p = r.astype(BF16)
        parts.append(p)
        if i + 1 < n:
            r = r - p.astype(F32)
    return parts


def _dot_x01(x, m01, n=3):
    acc = None
    for p in _split(x, n):
        t = _dot(p, m01)
        acc = t if acc is None else acc + t
    return acc


def _dot_01x(m01, x, n=3):
    acc = None
    for p in _split(x, n):
        t = _dot(m01, p)
        acc = t if acc is None else acc + t
    return acc


def _sigmoid(x):
    return 0.5 + 0.5 * jnp.tanh(0.5 * x)


def _silu(x):
    return x * _sigmoid(x)


def _softplus(x):
    return jnp.maximum(x, 0.0) + jnp.log(1.0 + jnp.exp(-jnp.abs(x)))


def _modulated_norm(x, nw, shift, scale):
    r = lax.rsqrt(jnp.mean(x * x, axis=-1, keepdims=True) + EPS)
    return (x * r) * nw * (1.0 + scale) + shift


def _adaln_kernel(c_ref, w_ref, b_ref, o_ref):
    a = _silu(c_ref[...]).astype(BF16)
    o_ref[0] = _dot(a, w_ref[0].astype(BF16)) + b_ref[0]


def _adaln(c_all, ada_w, ada_b):
    m, d = c_all.shape
    n = ada_w.shape[-1]
    tn = 1536
    return pl.pallas_call(
        _adaln_kernel,
        out_shape=jax.ShapeDtypeStruct((DEPTH, m, n), F32),
        grid=(DEPTH, n // tn),
        in_specs=[pl.BlockSpec((m, d), lambda l, j: (0, 0)),
                  pl.BlockSpec((1, d, tn), lambda l, j: (l, 0, j)),
                  pl.BlockSpec((1, 1, tn), lambda l, j: (l, 0, j))],
        out_specs=pl.BlockSpec((1, m, tn), lambda l, j: (l, 0, j)),
        compiler_params=_cparams(("parallel", "parallel")),
        name="adaln",
    )(c_all, ada_w, ada_b.reshape(DEPTH, 1, n))


def _modmm_kernel(x_ref, nw_ref, sh_ref, sc_ref, w_ref, o_ref, h_ref):
    @pl.when(pl.program_id(1) == 0)
    def _():
        h_ref[...] = _modulated_norm(x_ref[...], nw_ref[...], sh_ref[0], sc_ref[0]).astype(BF16)

    o_ref[...] = _dot(h_ref[...], w_ref[...])


def _mod_rows(tm, rows_per_mod):
    if rows_per_mod is None:
        return lambda shape: pl.BlockSpec((1,) + shape[1:], lambda i, j: (0, 0, 0))
    bpb = rows_per_mod // tm
    return lambda shape: pl.BlockSpec((1,) + shape[1:], lambda i, j: (i // bpb, 0, 0))


def _mod_matmul(x, nw, shift, scale, w, *, tm, tn, rows_per_mod):
    m, d = x.shape
    n = w.shape[1]
    spec = _mod_rows(tm, rows_per_mod)
    w_spec = (pl.BlockSpec((d, tn), lambda i, j: (0, 0), pipeline_mode=pl.Buffered(1)) if tn == n
              else pl.BlockSpec((d, tn), lambda i, j: (0, j)))
    return pl.pallas_call(
        _modmm_kernel,
        out_shape=jax.ShapeDtypeStruct((m, n), F32),
        grid=(m // tm, n // tn),
        in_specs=[pl.BlockSpec((tm, d), lambda i, j: (i, 0)),
                  pl.BlockSpec((1, d), lambda i, j: (0, 0)),
                  spec(shift.shape), spec(scale.shape),
                  w_spec],
        out_specs=pl.BlockSpec((tm, tn), lambda i, j: (i, j)),
        scratch_shapes=[pltpu.VMEM((tm, d), BF16)],
        compiler_params=_cparams(("parallel", "arbitrary")),
        name="mod_matmul",
    )(x, nw, shift, scale, w)


def _post_mlp_kernel(x_ref, a_ref, wo_ref, g1_ref, nw_ref, sh_ref, sc_ref, g2_ref, w1_ref, w2_ref, fnw_ref, o_ref,
                     *, final_norm, tf):
    x1 = x_ref[...] + g1_ref[0] * _dot(a_ref[...], wo_ref[...])
    h = _modulated_norm(x1, nw_ref[...], sh_ref[0], sc_ref[0]).astype(BF16)
    acc = None
    for k in range(w1_ref.shape[1] // tf):
        u = jnp.maximum(_dot(h, w1_ref[:, k * tf:(k + 1) * tf]), 0.0)
        part = _dot((u * u).astype(BF16), w2_ref[k * tf:(k + 1) * tf, :])
        acc = part if acc is None else acc + part
    y = x1 + g2_ref[0] * acc
    if final_norm:
        r = lax.rsqrt(jnp.mean(y * y, axis=-1, keepdims=True) + EPS)
        y = (y * r) * fnw_ref[...]
    o_ref[...] = y


def _post_mlp(x, a, wo, g1, nw, shift, scale, g2, w1, w2, fnw, *, tm, tf, rows_per_mod, final_norm):
    m, d = x.shape
    ka = a.shape[1]
    ff = w1.shape[1]
    spec = _mod_rows(tm, rows_per_mod)
    resident = lambda shape: pl.BlockSpec(shape, lambda i, j: (0, 0), pipeline_mode=pl.Buffered(1))
    return pl.pallas_call(
        functools.partial(_post_mlp_kernel, final_norm=final_norm, tf=tf),
        out_shape=jax.ShapeDtypeStruct((m, d), F32),
        grid=(m // tm, 1),
        in_specs=[pl.BlockSpec((tm, d), lambda i, j: (i, 0)),
                  pl.BlockSpec((tm, ka), lambda i, j: (i, 0)),
                  resident((ka, d)),
                  spec(g1.shape),
                  pl.BlockSpec((1, d), lambda i, j: (0, 0)),
                  spec(shift.shape), spec(scale.shape), spec(g2.shape),
                  resident((d, ff)), resident((ff, d)),
                  pl.BlockSpec((1, d), lambda i, j: (0, 0))],
        out_specs=pl.BlockSpec((tm, d), lambda i, j: (i, 0)),
        compiler_params=_cparams(("parallel", "arbitrary")),
        name="post_mlp",
    )(x, a, wo, g1, nw, shift, scale, g2, w1, w2, fnw)


def _head_expand_matrix():
    e = np.zeros((LANES, SSD_D_INNER), np.float32)
    for h in range(SSD_HEADS):
        e[h, h * SSD_P:(h + 1) * SSD_P] = 1.0
    return jnp.asarray(e, BF16)


def _ssd_prompt_kernel(z_ref, x_ref, bc_ref, dt_ref, cwx_ref, cbx_ref, cwb_ref, cbb_ref, dtb_ref, alog_ref,
                       dskip_ref, nw_ref, exp_ref, tri_ref,
                       y_ref, conv_ref, ssm_ref, xs_ref, bs_ref, st_ref):
    c = pl.program_id(1)
    L = SSD_CHUNK
    tail = SSD_CONV_W - 1

    @pl.when(c == 0)
    def _():
        xs_ref[0:8, :] = jnp.zeros((8, SSD_D_INNER), F32)
        bs_ref[0:8, :] = jnp.zeros((8, SSD_BC), F32)
        st_ref[...] = jnp.zeros_like(st_ref)

    xs_ref[8:8 + L, :] = x_ref[...]
    bs_ref[8:8 + L, :] = bc_ref[...]

    def conv(buf, w_ref, b_ref):
        acc = b_ref[...] + buf[8 - tail:8 - tail + L, :] * w_ref[0:1, :]
        for k in range(1, SSD_CONV_W):
            acc = acc + buf[8 - tail + k:8 - tail + k + L, :] * w_ref[k:k + 1, :]
        return _silu(acc)

    x = conv(xs_ref, cwx_ref, cbx_ref)
    bcv = conv(bs_ref, cwb_ref, cbb_ref)
    xs_ref[8 - tail:8, :] = xs_ref[8 + L - tail:8 + L, :]
    bs_ref[8 - tail:8, :] = bs_ref[8 + L - tail:8 + L, :]

    @pl.when(c == pl.num_programs(1) - 1)
    def _():
        conv_ref[0, :, 0:SSD_D_INNER] = x_ref[L - tail:L, :]
        conv_ref[0, :, SSD_D_INNER:SSD_CONV_DIM] = bc_ref[L - tail:L, :]

    lane = lax.broadcasted_iota(jnp.int32, (1, LANES), 1)
    head_ok = lane < SSD_HEADS
    dt = jnp.where(head_ok, _softplus(dt_ref[...] + dtb_ref[...]), 0.0)
    a = jnp.where(head_ok, -jnp.exp(alog_ref[...]) * LOG2E, 0.0)
    acum = _dot_01x(tri_ref[...], dt * a)
    acum_t = acum.T
    a_last = acum[L - 1:L, :]
    expand = exp_ref[...]
    dt_e = _dot_x01(dt, expand)
    eac_e = _dot_x01(jnp.exp2(acum), expand)
    dend_e = _dot_x01(jnp.exp2(a_last - acum), expand)
    cdec_e = eac_e[L - 1:L, :]

    xdt = x * dt_e
    xdt_b = xdt.astype(BF16)
    xdtw_b = (xdt * dend_e).astype(BF16)
    row = lax.broadcasted_iota(jnp.int32, (L, L), 0)
    col = lax.broadcasted_iota(jnp.int32, (L, L), 1)
    causal = row >= col
    lane2 = lax.broadcasted_iota(jnp.int32, (L, 2 * SSD_P), 1)
    first_head = lane2 < SSD_P

    y_parts = []
    for g in range(SSD_GROUPS):
        b_g = bcv[:, g * SSD_N:(g + 1) * SSD_N]
        c_g = bcv[:, (SSD_GROUPS + g) * SSD_N:(SSD_GROUPS + g + 1) * SSD_N]
        b_gb = b_g.astype(BF16)
        c_gb = c_g.astype(BF16)
        cb = _dot_nt(c_gb, b_gb)
        gl = slice(g * SSD_REP * SSD_P, (g + 1) * SSD_REP * SSD_P)
        st_g = st_ref[:, gl]
        y_off = _dot(c_gb, st_g.astype(BF16)) * eac_e[:, gl]
        pair_out = []
        for j in range(SSD_REP // 2):
            pl_ = slice(g * SSD_REP * SSD_P + j * 2 * SSD_P, g * SSD_REP * SSD_P + (j + 1) * 2 * SSD_P)
            xp = xdt_b[:, pl_]
            ys = []
            for hh in range(2):
                h = g * SSD_REP + 2 * j + hh
                seg = acum[:, h:h + 1] - acum_t[h:h + 1, :]
                dec = jnp.exp2(jnp.where(causal, seg, NEG))
                ys.append(_dot((cb * dec).astype(BF16), xp))
            pair_out.append(jnp.where(first_head, ys[0], ys[1]))
        y_g = jnp.concatenate(pair_out, axis=1) + y_off
        y_parts.append(y_g)
        st_ref[:, gl] = st_g * cdec_e[:, gl] + _dot(b_g.T.astype(BF16), xdtw_b[:, gl])

    y = jnp.concatenate(y_parts, axis=1) + dskip_ref[...] * x
    zg = z_ref[...]
    y = y * _silu(zg)
    r = lax.rsqrt(jnp.mean(y * y, axis=-1, keepdims=True) + EPS)
    y_ref[...] = ((y * r) * nw_ref[...]).astype(BF16)

    @pl.when(c == pl.num_programs(1) - 1)
    def _():
        ssm_ref[0] = st_ref[...].T


def _ssd_prompt(proj, bsz, t, cw, cb, dtb, alog, dskip, nw):
    L = SSD_CHUNK
    nc = t // L
    di = SSD_D_INNER
    tri = jnp.asarray(np.tril(np.ones((L, L), np.float32)), BF16)
    full = lambda shape: pl.BlockSpec(shape, lambda b, c: (0,) * len(shape))
    return pl.pallas_call(
        _ssd_prompt_kernel,
        out_shape=(jax.ShapeDtypeStruct((bsz * t, di), BF16),
                   jax.ShapeDtypeStruct((bsz, SSD_CONV_W - 1, SSD_CONV_DIM), F32),
                   jax.ShapeDtypeStruct((bsz, di, SSD_N), F32)),
        grid=(bsz, nc),
        in_specs=[pl.BlockSpec((L, di), lambda b, c: (b * nc + c, 0)),
                  pl.BlockSpec((L, di), lambda b, c: (b * nc + c, 1)),
                  pl.BlockSpec((L, SSD_BC), lambda b, c: (b * nc + c, 2 * di // SSD_BC)),
                  pl.BlockSpec((L, LANES), lambda b, c: (b * nc + c, (2 * di + SSD_BC) // LANES)),
                  full((SSD_CONV_W, di)), full((1, di)), full((SSD_CONV_W, SSD_BC)), full((1, SSD_BC)),
                  full((1, LANES)), full((1, LANES)), full((1, di)), full((1, di)),
                  full((LANES, di)), full((L, L))],
        out_specs=(pl.BlockSpec((L, di), lambda b, c: (b * nc + c, 0)),
                   pl.BlockSpec((1, SSD_CONV_W - 1, SSD_CONV_DIM), lambda b, c: (b, 0, 0)),
                   pl.BlockSpec((1, di, SSD_N), lambda b, c: (b, 0, 0))),
        scratch_shapes=[pltpu.VMEM((8 + L, di), F32), pltpu.VMEM((8 + L, SSD_BC), F32),
                        pltpu.VMEM((SSD_N, di), F32)],
        compiler_params=_cparams(("parallel", "arbitrary")),
        name="ssd_prompt",
    )(proj, proj, proj, proj, cw[:, :di], cb[:, :di], cw[:, di:], cb[:, di:], dtb, alog, dskip, nw,
      _head_expand_matrix(), tri)


def _ssd_sample_kernel(p_ref, cs_ref, st_ref, cw_ref, cb_ref, dtb_ref, alog_ref, dskip_ref, nw_ref, exp_ref, *rest):
    y_ref, conv_ref, ssm_ref = rest[-3:]
    di = SSD_D_INNER
    z = p_ref[0, :, 0:di]
    xbc = p_ref[0, :, di:di + SSD_CONV_DIM]
    dtr = p_ref[0, :, di + SSD_CONV_DIM:di + SSD_CONV_DIM + LANES]
    cs = cs_ref[0]
    acc = cb_ref[...] + xbc * cw_ref[SSD_CONV_W - 1:SSD_CONV_W, :]
    for k in range(SSD_CONV_W - 1):
        acc = acc + cs[k:k + 1, :] * cw_ref[k:k + 1, :]
    conv_ref[0, 0:SSD_CONV_W - 2, :] = cs[1:SSD_CONV_W - 1, :]
    conv_ref[0, SSD_CONV_W - 2:SSD_CONV_W - 1, :] = xbc
    act = _silu(acc)
    x = act[:, 0:di]
    lane = lax.broadcasted_iota(jnp.int32, (1, LANES), 1)
    head_ok = lane < SSD_HEADS
    dt = jnp.where(head_ok, _softplus(dtr + dtb_ref[...]), 0.0)
    a = jnp.where(head_ok, -jnp.exp(alog_ref[...]), 0.0)
    da = jnp.exp(dt * a)
    lhs = jnp.concatenate([dt, da, jnp.zeros((6, LANES), F32)], axis=0)
    ex = _dot_x01(lhs, exp_ref[...])
    dt_e, da_e = ex[0:1, :], ex[1:2, :]
    xdt = x * dt_e

    eye = (lax.broadcasted_iota(jnp.int32, (LANES, LANES), 0)
           == lax.broadcasted_iota(jnp.int32, (LANES, LANES), 1))

    def to_col(rowvec):
        return jnp.sum(jnp.where(eye, jnp.broadcast_to(rowvec, (LANES, LANES)), 0.0), axis=1, keepdims=True)

    def to_row(colvec):
        return jnp.sum(jnp.where(eye, jnp.broadcast_to(colvec, (LANES, LANES)), 0.0), axis=0, keepdims=True)

    y_off = []
    cbs = []
    for g in range(SSD_GROUPS):
        b_g = act[:, di + g * SSD_N:di + (g + 1) * SSD_N]
        c_g = act[:, di + (SSD_GROUPS + g) * SSD_N:di + (SSD_GROUPS + g + 1) * SSD_N]
        cbs.append(jnp.broadcast_to(jnp.sum(b_g * c_g, axis=1, keepdims=True), (1, SSD_REP * SSD_P)))
        for i in range(SSD_REP * SSD_P // LANES):
            lo = g * SSD_REP * SSD_P + i * LANES
            st = st_ref[0, 0, lo:lo + LANES, :]
            xcol = to_col(xdt[:, lo:lo + LANES])
            dcol = to_col(da_e[:, lo:lo + LANES])
            ssm_ref[0, 0, lo:lo + LANES, :] = st * dcol + xcol * b_g
            y_off.append(to_row(jnp.sum(st * c_g, axis=1, keepdims=True)))
    y = xdt * jnp.concatenate(cbs, axis=1) + jnp.concatenate(y_off, axis=1) * da_e + dskip_ref[...] * x
    y = y * _silu(z)
    r = lax.rsqrt(jnp.mean(y * y, axis=-1, keepdims=True) + EPS)
    y_ref[0] = ((y * r) * nw_ref[...]).astype(BF16)


def _ssd_sample(proj, conv_state, ssm_states, layer, ssm_out_prev, cw, cb, dtb, alog, dskip, nw):
    n_layers, bsz = ssm_states.shape[:2]
    di = SSD_D_INNER
    full = lambda shape: pl.BlockSpec(shape, lambda b: (0,) * len(shape))
    per_b = lambda shape: pl.BlockSpec((1,) + shape, lambda b: (b,) + (0,) * len(shape))
    state_spec = pl.BlockSpec((1, 1, di, SSD_N), lambda b: (layer, b, 0, 0))
    in_specs = [per_b((1, SSD_NP)), per_b((SSD_CONV_W - 1, SSD_CONV_DIM)), state_spec,
                full((SSD_CONV_W, SSD_CONV_DIM)), full((1, SSD_CONV_DIM)), full((1, LANES)), full((1, LANES)),
                full((1, di)), full((1, di)), full((LANES, di))]
    args = [proj.reshape(bsz, 1, SSD_NP), conv_state, ssm_states.reshape(n_layers, bsz, di, SSD_N), cw, cb, dtb, alog,
            dskip, nw, _head_expand_matrix()]
    aliases = {}
    if ssm_out_prev is not None:
        in_specs.append(pl.BlockSpec(memory_space=pl.ANY))
        aliases = {len(args): 2}
        args.append(ssm_out_prev)
    y, conv, ssm = pl.pallas_call(
        _ssd_sample_kernel,
        out_shape=(jax.ShapeDtypeStruct((bsz, 1, di), BF16),
                   jax.ShapeDtypeStruct((bsz, SSD_CONV_W - 1, SSD_CONV_DIM), F32),
                   jax.ShapeDtypeStruct((n_layers, bsz, di, SSD_N), F32)),
        grid=(bsz,),
        in_specs=in_specs,
        out_specs=(per_b((1, di)), per_b((SSD_CONV_W - 1, SSD_CONV_DIM)), state_spec),
        input_output_aliases=aliases,
        compiler_params=_cparams(("parallel",)),
        name="ssd_sample",
    )(*args)
    return y.reshape(bsz, di), conv, ssm


CH = CMP_STRIDE
N_L = CMP_BLOCK // 2
PE_ROWS = 16


def _compress_accumulate(lhs_fn, pe_ref, wab_ref, kv):
    acc = None
    for l in range(N_L):
        lhs = jnp.concatenate([lhs_fn(l, kv).astype(BF16), pe_ref[kv, l]], axis=0)
        t = _dot(lhs, wab_ref[kv, l])
        acc = t if acc is None else acc + t
    return acc


def _compress_finish(acc, w2, nch):
    p = acc[0:nch, 0:GW]
    q_next = pltpu.roll(acc[0:nch, GW:2 * GW], nch - 1, axis=0)
    bias = acc[nch:nch + 1, 0:GW] + acc[nch + 8:nch + 9, GW:2 * GW]
    out = _dot(_silu(p + q_next + bias).astype(BF16), w2)
    row = lax.broadcasted_iota(jnp.int32, (nch, 1), 0)
    return jnp.where(row < nch - 1, out, 0.0)


def _softmax_masked(s, ok, axis, exp=jnp.exp):
    s = jnp.where(ok, s, NEG)
    m = jnp.max(s, axis=axis, keepdims=True)
    p = jnp.where(ok, exp(s - m), 0.0)
    d = jnp.sum(p, axis=axis, keepdims=True)
    return p * jnp.where(d > 0.0, 1.0 / d, 0.0)


def _interleave(n, scores, softmax, values, finish):
    out, pending = [], None
    s_next = scores(0)
    for k in range(n):
        s = s_next
        if k + 1 < n:
            s_next = scores(k + 1)
        p, aux = softmax(k, s)
        if pending is not None:
            out.append(finish(*pending))
        pending = (k, values(k, p), aux)
    out.append(finish(*pending))
    return out


def _topk_mask(score, j, width, shifts, axis):
    n = score.shape[axis]
    rank = jnp.zeros(score.shape, F32)
    for k in shifts:
        other = pltpu.roll(score, k, axis=axis)
        lower = j >= k
        if width != n:
            other = jnp.where(lower, other, pltpu.roll(score, n - width + k, axis=axis))
        rank = rank + jnp.where(lower, jnp.where(other >= score, 1.0, 0.0), jnp.where(other > score, 1.0, 0.0))
    return rank < SEL_TOPK


def _overlap(n_cmp, n_sel):
    c_start = np.arange(n_cmp) * CMP_STRIDE
    s_start = np.arange(n_sel) * SEL_BLOCK
    return ((c_start[:, None] < s_start[None, :] + SEL_BLOCK)
            & (c_start[:, None] + CMP_BLOCK > s_start[None, :])).astype(np.float32)


def _block_bias_matrix(width, n_keys):
    l = np.arange(LANES)[:, None] % width
    blk = np.arange(n_keys)[None, :] // SEL_BLOCK
    return np.where(l == blk, NEG_MXU, 0.0).astype(np.float32)


def _nsa_proj_kernel(x_ref, nw_ref, sh_ref, sc_ref, wr_ref, wt_ref, *rest):
    rows_ref, kpad_ref, qt_ref, gt_ref, vt_ref, cmp_ref, sel_ref, win_ref = rest[-8:]
    h = _modulated_norm(x_ref[...], nw_ref[...], sh_ref[0], sc_ref[0]).astype(BF16)
    y = _dot(h, wr_ref[...])
    rows_ref[...] = y[:, 0:KVW].astype(BF16)
    kpad_ref[...] = y[:, KVW:].astype(BF16)
    yt = _dot_nt(wt_ref[...], h)
    qt_ref[0] = (yt[0:1024] * (HD ** -0.5 * LOG2E)).astype(BF16)
    cmp_ref[0, 0] = yt[1024:1024 + KVW]
    sel_ref[0, 0] = yt[1024 + KVW:1024 + 2 * KVW]
    vt_ref[0, 0:GW] = yt[1024 + KVW + GW:1024 + 2 * KVW].astype(BF16)
    vt_ref[0, GW:2 * GW] = yt[1024 + 2 * KVW + GW:1024 + 3 * KVW].astype(BF16)
    gt_ref[0] = _sigmoid(yt[1024 + 3 * KVW:NSA_N])

    @pl.when(pl.program_id(1) == pl.num_programs(1) - 1)
    def _():
        win_ref[0, 0] = yt[1024 + 2 * KVW:1024 + 3 * KVW]


def _nsa_proj(x, nw, shift, scale, wr, wt, bsz, t, layer, n_layers, kv_prev):
    d = x.shape[1]
    tm = min(WINDOW, t)
    nb = t // tm
    row = lambda w: pl.BlockSpec((tm, w), lambda b, i: (b * nb + i, 0))
    col = lambda h: pl.BlockSpec((1, h, tm), lambda b, i: (b, 0, i))
    kv_col = pl.BlockSpec((1, 1, KVW, tm), lambda b, i: (layer, b, 0, i))
    mod = lambda: pl.BlockSpec((1, 1, d), lambda b, i: (b, 0, 0))
    nq = NSA_HEADS * HD
    in_specs = [row(d), pl.BlockSpec((1, d), lambda b, i: (0, 0)), mod(), mod(),
                pl.BlockSpec((d, KVW + 2 * KPAD), lambda b, i: (0, 0)),
                pl.BlockSpec((NSA_N, d), lambda b, i: (0, 0))]
    args = [x, nw, shift, scale, wr, wt]
    aliases = {}
    if kv_prev is not None:
        aliases = {len(args) + k: 5 + k for k in range(3)}
        in_specs += [pl.BlockSpec(memory_space=pl.ANY)] * 3
        args += list(kv_prev)
    return pl.pallas_call(
        _nsa_proj_kernel,
        out_shape=(jax.ShapeDtypeStruct((bsz * t, KVW), BF16),
                   jax.ShapeDtypeStruct((bsz * t, 2 * KPAD), BF16),
                   jax.ShapeDtypeStruct((bsz, nq, t), BF16),
                   jax.ShapeDtypeStruct((bsz, LANES, t), F32),
                   jax.ShapeDtypeStruct((bsz, 2 * GW, t), BF16),
                   jax.ShapeDtypeStruct((n_layers, bsz, KVW, t), F32),
                   jax.ShapeDtypeStruct((n_layers, bsz, KVW, t), F32),
                   jax.ShapeDtypeStruct((n_layers, bsz, KVW, tm), F32)),
        grid=(bsz, nb),
        in_specs=in_specs,
        out_specs=(row(KVW), row(2 * KPAD), col(nq), col(LANES), col(2 * GW), kv_col, kv_col,
                   pl.BlockSpec((1, 1, KVW, tm), lambda b, i: (layer, b, 0, 0))),
        input_output_aliases=aliases,
        compiler_params=_cparams(("parallel", "arbitrary")),
        name="nsa_proj",
    )(*args)


def _cmp_prompt_kernel(rows_ref, perm_ref, pe_ref, wab_ref, w2k_ref, w2v_ref, kc_ref, vc_ref, xr_ref):
    nch = rows_ref.shape[1] // CH
    cpp = PAGE_SIZE // CH
    perm = perm_ref[...]
    for p in range(rows_ref.shape[1] // PAGE_SIZE):
        rows = _dot(perm, rows_ref[0, p * PAGE_SIZE:(p + 1) * PAGE_SIZE, :])
        for l in range(N_L):
            xr_ref[l, p * cpp:(p + 1) * cpp, :] = rows[l * cpp:(l + 1) * cpp, :]
    lhs = lambda l, kv: xr_ref[l, :, kv * GW:(kv + 1) * GW]
    kc_ref[0] = _compress_finish(_compress_accumulate(lhs, pe_ref, wab_ref, 0), w2k_ref[...], nch).astype(BF16)
    vc_ref[0] = _compress_finish(_compress_accumulate(lhs, pe_ref, wab_ref, 1), w2v_ref[...], nch).T.astype(BF16)


def _chunk_perm():
    cpp = PAGE_SIZE // CH
    perm = np.zeros((PAGE_SIZE, PAGE_SIZE), np.float32)
    for l in range(N_L):
        for c in range(cpp):
            perm[l * cpp + c, c * CH + l] = 1.0
    return jnp.asarray(perm, BF16)


def _cmp_prompt(rows, pe_t, wab, w2k_pad, w2v):
    bsz, t, _ = rows.shape
    nch = t // CH
    full = lambda shape: pl.BlockSpec(shape, lambda b: (0,) * len(shape))
    return pl.pallas_call(
        _cmp_prompt_kernel,
        out_shape=(jax.ShapeDtypeStruct((bsz, nch, KPAD), BF16), jax.ShapeDtypeStruct((bsz, GW, nch), BF16)),
        grid=(bsz,),
        in_specs=[pl.BlockSpec((1, t, KVW), lambda b: (b, 0, 0)), full((PAGE_SIZE, PAGE_SIZE)),
                  full(pe_t.shape), full(wab.shape), full(w2k_pad.shape), full(w2v.shape)],
        out_specs=(pl.BlockSpec((1, nch, KPAD), lambda b: (b, 0, 0)), pl.BlockSpec((1, GW, nch), lambda b: (b, 0, 0))),
        scratch_shapes=[pltpu.VMEM((N_L, nch, KVW), F32)],
        compiler_params=_cparams(("parallel",)),
        name="nsa_compress_prompt",
    )(rows, _chunk_perm(), pe_t, wab, w2k_pad, w2v)


def _nsa_prompt_kernel(qt_ref, gt_ref, kc_ref, vct_ref, kpad_ref, vt_ref, ovt_ref, nege_ref, o_ref,
                       *, tq, n_sel, ck):
    i = pl.program_id(1)
    t0 = i * tq
    nq = REP * tq
    tl = t0 + lax.broadcasted_iota(jnp.int32, (1, tq), 1)
    zq = jnp.zeros((HD, nq), BF16)

    def lanes4(a):
        return jnp.concatenate([a] * REP, axis=1)

    def with_ones(vt):
        return jnp.concatenate([vt, jnp.ones((ONES_ROWS, vt.shape[1]), BF16)], axis=0)

    def normalized(acc):
        return acc[0:HD] * (1.0 / acc[HD:HD + 1])

    tl4 = lanes4(tl)
    qg = [jnp.concatenate([qt_ref[0, (g * REP + r) * HD:(g * REP + r + 1) * HD, :] for r in range(REP)], axis=1)
          for g in range(KVH)]
    q_rhs = [jnp.concatenate([q, zq], axis=0) for q in qg]

    ncp = kc_ref.shape[1]
    cend = lax.broadcasted_iota(jnp.int32, (ncp, 1), 0) * CMP_STRIDE + (CMP_BLOCK - 1)
    ok_cmp = cend <= tl4
    s_cmp = [_dot(kc_ref[0, :, g * LANES:(g + 1) * LANES], q_rhs[g]) for g in range(KVH)]
    p_cmp = [_softmax_masked(s, ok_cmp, 0, jnp.exp2) for s in s_cmp]
    o_cmp = [_dot(vct_ref[0, g * HD:(g + 1) * HD, :], p_cmp[g].astype(BF16)) for g in range(KVH)]
    imp = jnp.zeros((LANES, tq), F32)
    for g, p in enumerate(p_cmp):
        psum = p[:, 0:tq] + p[:, tq:2 * tq] + p[:, 2 * tq:3 * tq] + p[:, 3 * tq:4 * tq]
        imp = imp + _dot_01x(ovt_ref[g], psum)

    row = lax.broadcasted_iota(jnp.int32, (LANES, tq), 0)
    jj = row % n_sel
    cur = tl // SEL_BLOCK
    valid = jj * SEL_BLOCK <= tl
    forced = (jj == 0) | (jj == cur) | (jj == cur - 1)
    score = jnp.where(valid, jnp.where(forced, BIG, imp), NEG)
    top = _topk_mask(score, jj, n_sel, range(1, n_sel), 0)
    blocked = jnp.where(top & valid, 0.0, 1.0)

    n_full = t0 // ck
    sel_rhs = [jnp.concatenate([lanes4(jnp.where(row // n_sel == g, blocked, 0.0).astype(BF16)), q_rhs[g]], axis=0)
               for g in range(KVH)]

    def sel_chunk(c, carries, causal):
        k0 = pl.multiple_of(c * ck, ck)
        bias_rows = nege_ref[pl.ds(k0, ck), :]
        if causal:
            cbias = jnp.where(k0 + lax.broadcasted_iota(jnp.int32, (ck, 1), 0) <= tl4, 0.0, NEG)

        def scores(g):
            lhs = jnp.concatenate([bias_rows, kpad_ref[0, pl.ds(k0, ck), g * LANES:(g + 1) * LANES]], axis=1)
            s = _dot(lhs, sel_rhs[g])
            return s + cbias if causal else s

        def softmax(g, s):
            m, acc = carries[g]
            m_new = jnp.maximum(m, jnp.max(s, axis=0, keepdims=True))
            return jnp.exp2(s - m_new).astype(BF16), (m_new, jnp.exp2(m - m_new), acc)

        def values(g, p):
            return _dot(with_ones(vt_ref[0, g * HD:(g + 1) * HD, pl.ds(k0, ck)]), p)

        def finish(g, pv, aux):
            m_new, alpha, acc = aux
            return m_new, alpha * acc + pv

        return tuple(_interleave(KVH, scores, softmax, values, finish))

    init = (jnp.full((1, nq), NEG, F32), jnp.zeros((HD + ONES_ROWS, nq), F32))
    carries = lax.fori_loop(0, n_full, lambda c, cr: sel_chunk(c, cr, False), (init,) * KVH)
    o_sel = [normalized(acc) for _, acc in sel_chunk(n_full, carries, True)]

    kw = WINDOW + tq
    k_start = pl.multiple_of(jnp.maximum(t0 - WINDOW, 0), LANES)
    dpos = tl - (k_start + lax.broadcasted_iota(jnp.int32, (kw, 1), 0))
    wbias = lanes4(jnp.where((dpos >= 0) & (dpos <= WINDOW), 0.0, NEG))

    def win_scores(g):
        return _dot(kpad_ref[0, pl.ds(k_start, kw), (KVH + g) * LANES:(KVH + g + 1) * LANES], q_rhs[g]) + wbias

    def win_softmax(g, s):
        return jnp.exp2(s - jnp.max(s, axis=0, keepdims=True)).astype(BF16), None

    def win_values(g, p):
        return _dot(with_ones(vt_ref[0, GW + g * HD:GW + (g + 1) * HD, pl.ds(k_start, kw)]), p)

    o_win = _interleave(KVH, win_scores, win_softmax, win_values, lambda g, pv, aux: normalized(pv))

    gt = gt_ref[0]
    outs = []
    for g in range(KVH):
        for r in range(REP):
            h = g * REP + r
            sl = slice(r * tq, (r + 1) * tq)
            outs.append(gt[3 * h:3 * h + 1, :] * o_cmp[g][:, sl] + gt[3 * h + 1:3 * h + 2, :] * o_sel[g][:, sl]
                        + gt[3 * h + 2:3 * h + 3, :] * o_win[g][:, sl])
    o_ref[0] = jnp.concatenate(outs, axis=0).T.astype(BF16)


def _nsa_prompt_attention(qt, gt, kc, vct, kpad, vt):
    bsz, nd, t = qt.shape
    tq = LANES
    ck = 4 * LANES
    n_sel = t // SEL_BLOCK
    ncp = t // CH
    assert KVH * n_sel == LANES and ncp == LANES and t % ck == 0
    ovt = np.zeros((KVH, LANES, ncp), np.float32)
    for g in range(KVH):
        ovt[g, g * n_sel:(g + 1) * n_sel, :ncp - 1] = _overlap(ncp - 1, n_sel).T
    nege_t = _block_bias_matrix(n_sel, t).T
    per_b = lambda shape: pl.BlockSpec((1,) + shape, lambda b, i: (b,) + (0,) * len(shape))
    full = lambda shape: pl.BlockSpec(shape, lambda b, i: (0,) * len(shape))
    out = pl.pallas_call(
        functools.partial(_nsa_prompt_kernel, tq=tq, n_sel=n_sel, ck=ck),
        out_shape=jax.ShapeDtypeStruct((bsz, t, nd), BF16),
        grid=(bsz, t // tq),
        in_specs=[pl.BlockSpec((1, nd, tq), lambda b, i: (b, 0, i)),
                  pl.BlockSpec((1, LANES, tq), lambda b, i: (b, 0, i)),
                  per_b((ncp, KPAD)), per_b((GW, ncp)), per_b((t, 2 * KPAD)), per_b((2 * GW, t)),
                  full((KVH, LANES, ncp)), full((t, LANES))],
        out_specs=pl.BlockSpec((1, tq, nd), lambda b, i: (b, i, 0)),
        compiler_params=_cparams(("parallel", "arbitrary")),
        name="nsa_prompt_attention",
    )(qt, gt, kc, vct, kpad.reshape(bsz, t, 2 * KPAD), vt, jnp.asarray(ovt, BF16), jnp.asarray(nege_t, BF16))
    return out.reshape(bsz * t, nd)


def _nsa_sample_kernel(pt_ref, p_ref, *rest, n_pages, past):
    del pt_ref
    cmp_pages = rest[:n_pages]
    sel_pages = rest[n_pages:2 * n_pages]
    win_ref, pe_ref, wab_ref, w2_ref, ov_ref, nege_ref, perm_ref = rest[2 * n_pages:2 * n_pages + 7]
    o_ref, wout_ref, xr_ref = rest[-3:]
    nch = n_pages * (PAGE_SIZE // CH)
    cpp = PAGE_SIZE // CH
    t = past
    keep = win_ref.shape[-1]
    rows_q = 8

    def compress_cache():
        perm = perm_ref[...]
        for p, pg in enumerate(cmp_pages):
            for kv in range(2):
                rows = _dot_nt(perm, pg[0, 0, kv].reshape(GW, PAGE_SIZE).astype(BF16))
                for l in range(N_L):
                    xr_ref[kv, l, p * cpp:(p + 1) * cpp, :] = rows[l * cpp:(l + 1) * cpp, :]
        out = []
        for kv in range(2):
            acc = _compress_accumulate(lambda l, kv: xr_ref[kv, l], pe_ref, wab_ref, kv)
            out.append(_compress_finish(acc, w2_ref[kv], nch).T.astype(BF16))
        return out

    lane = lax.broadcasted_iota(jnp.int32, (rows_q, LANES), 1)
    rowi = lax.broadcasted_iota(jnp.int32, (rows_q, LANES), 0)
    cend = lax.broadcasted_iota(jnp.int32, (1, nch), 1) * CMP_STRIDE + (CMP_BLOCK - 1)
    n_sel = -(-(t + 1) // SEL_BLOCK)
    nj = max(n_sel, SEL_TOPK)
    cur = t // SEL_BLOCK
    valid = (lane < n_sel) & (lane * SEL_BLOCK <= t)
    forced = (lane == 0) | (lane == cur) | (lane == cur - 1)
    shifts = list(range(1, nj)) + list(range(LANES - nj + 1, LANES))
    gates = jnp.broadcast_to(_sigmoid(p_ref[0, :, 1024 + 3 * KVW:NSA_N]), (rows_q, LANES))
    eye = (lax.broadcasted_iota(jnp.int32, (HD, HD), 0) == lax.broadcasted_iota(jnp.int32, (HD, HD), 1))
    last_lane = lax.broadcasted_iota(jnp.int32, (HD, keep), 1) == keep - 1

    G = range(KVH)
    new_row = lambda off, g: p_ref[0, :, off + g * HD:off + (g + 1) * HD]
    q = [jnp.concatenate([p_ref[0, :, (g * REP + r) * HD:(g * REP + r + 1) * HD] for r in range(REP)]
                         + [jnp.zeros((rows_q - REP, HD), F32)], axis=0) * (HD ** -0.5) for g in G]
    qb = [x.astype(BF16) for x in q]

    def softmax_with_new(s, s_new, ok_new):
        s_new = jnp.where(ok_new, s_new, NEG)
        m = jnp.maximum(jnp.max(s, axis=1, keepdims=True), s_new)
        pr = jnp.exp(s - m)
        pr_new = jnp.where(ok_new, jnp.exp(s_new - m), 0.0)
        return pr.astype(BF16), pr_new, 1.0 / (jnp.sum(pr, axis=1, keepdims=True) + pr_new)

    kw = [win_ref[0, 0, 0, g] for g in G]
    vw = [win_ref[0, 0, 1, g] for g in G]
    kw_new = [new_row(1024 + 2 * KVW, g) for g in G]
    vw_new = [new_row(1024 + 2 * KVW + GW, g) for g in G]
    s_win = [_dot(qb[g], kw[g].astype(BF16)) for g in G]
    sm = [softmax_with_new(s_win[g], jnp.sum(q[g] * kw_new[g], axis=1, keepdims=True), jnp.full((rows_q, 1), True))
          for g in G]
    pv = [_dot_nt(sm[g][0], vw[g].astype(BF16)) for g in G]
    o_win = [(pv[g] + sm[g][1] * vw_new[g]) * sm[g][2] for g in G]
    for g in G:
        for kv, old, new in ((0, kw[g], kw_new[g]), (1, vw[g], vw_new[g])):
            col = jnp.sum(jnp.where(eye, jnp.broadcast_to(new, (HD, HD)), 0.0), axis=1, keepdims=True)
            wout_ref[0, 0, kv, g] = jnp.where(last_lane, col, pltpu.roll(old, keep - 1, axis=1))

    kt = [jnp.concatenate([pg[0, 0, 0, g] for pg in sel_pages], axis=1).astype(BF16) for g in G]
    vt = [jnp.concatenate([pg[0, 0, 1, g] for pg in sel_pages], axis=1).astype(BF16) for g in G]

    kct, vct = compress_cache()
    s_cmp = [_dot(qb[g], kct[g * HD:(g + 1) * HD, :]) for g in G]
    p_cmp = [_softmax_masked(s, cend <= t, 1) for s in s_cmp]
    o_cmp = [_dot_nt(p_cmp[g].astype(BF16), vct[g * HD:(g + 1) * HD, :]) for g in G]
    allowed = []
    for g in G:
        psum = jnp.broadcast_to(jnp.sum(p_cmp[g][0:REP], axis=0, keepdims=True), (rows_q, nch))
        imp = _dot_x01(psum, ov_ref[...])
        score = jnp.where(valid, jnp.where(forced, BIG, imp), NEG)
        score = jnp.where(lane < nj, score, -3e38)
        allowed.append(_topk_mask(score, lane, LANES, shifts, 1) & valid)

    nege = nege_ref[...]
    s_sel = [_dot(jnp.concatenate([jnp.where(allowed[g], 0.0, 1.0).astype(BF16), qb[g]], axis=1),
                  jnp.concatenate([nege, kt[g]], axis=0)) for g in G]
    v_new = [new_row(1024 + KVW + GW, g) for g in G]
    sm = [softmax_with_new(s_sel[g], jnp.sum(q[g] * new_row(1024 + KVW, g), axis=1, keepdims=True),
                           jnp.sum(jnp.where((lane == cur) & allowed[g], 1.0, 0.0), axis=1, keepdims=True) > 0.5)
          for g in G]
    pv = [_dot_nt(sm[g][0], vt[g]) for g in G]
    o_sel = [(pv[g] + sm[g][1] * v_new[g]) * sm[g][2] for g in G]

    for g in G:
        o = None
        for br, ob in enumerate((o_cmp[g], o_sel[g], o_win[g])):
            gcol = jnp.sum(jnp.where(lane == (g * REP + rowi) * 3 + br, gates, 0.0), axis=1, keepdims=True)
            o = gcol * ob if o is None else o + gcol * ob
        for r in range(REP):
            h = g * REP + r
            o_ref[0, :, h * HD:(h + 1) * HD] = o[r:r + 1, :].astype(BF16)


def _nsa_sample(proj, cmp_view, sel_view, win_view, layer, win_out_prev, page_table, pe_t, wab, w2bd):
    n_layers = win_view.shape[0]
    bsz, n_pages = page_table.shape
    past = n_pages * PAGE_SIZE
    keep = win_view.shape[-1]
    nch = past // CH
    assert nch == LANES and keep <= WINDOW and past % SEL_BLOCK == 0 and past - keep >= 0
    n_sel = -(-(past + 1) // SEL_BLOCK)
    ov = np.zeros((nch, LANES), np.float32)
    ov[:nch - 1, :n_sel] = _overlap(nch - 1, n_sel)
    full = lambda shape: pl.BlockSpec(shape, lambda b, pt: (0,) * len(shape))
    once = lambda shape: pl.BlockSpec(shape, lambda b, pt: (0,) * len(shape), pipeline_mode=pl.Buffered(1))
    per_b = lambda shape: pl.BlockSpec((1,) + shape, lambda b, pt: (b,) + (0,) * len(shape))
    page_shape = (2, KVH, HD, PAGE_SIZE)
    page = lambda p: pl.BlockSpec((1, 1) + page_shape, lambda b, pt: (layer, pt[b * n_pages + p], 0, 0, 0, 0))
    win_shape = (2, KVH, HD, keep)
    win_spec = pl.BlockSpec((1, 1) + win_shape, lambda b, pt: (layer, b, 0, 0, 0, 0))
    in_specs = ([per_b((1, NSA_N))] + [page(p) for p in range(n_pages)] * 2
                + [win_spec, full(pe_t.shape), once(wab.shape), full(w2bd.shape), full((nch, LANES)),
                   full((LANES, past)), full((PAGE_SIZE, PAGE_SIZE))])
    args = [page_table.reshape(-1), proj.reshape(bsz, 1, NSA_N)] + [cmp_view] * n_pages + [sel_view] * n_pages + [
        win_view, pe_t, wab, w2bd, jnp.asarray(ov, BF16), jnp.asarray(_block_bias_matrix(LANES, past), BF16),
        _chunk_perm()]
    aliases = {}
    if win_out_prev is not None:
        in_specs.append(pl.BlockSpec(memory_space=pl.ANY))
        aliases = {len(args): 1}
        args.append(win_out_prev)
    o, wout = pl.pallas_call(
        functools.partial(_nsa_sample_kernel, n_pages=n_pages, past=past),
        out_shape=(jax.ShapeDtypeStruct((bsz, 1, NSA_HEADS * HD), BF16),
                   jax.ShapeDtypeStruct((n_layers, bsz) + win_shape, F32)),
        grid_spec=pltpu.PrefetchScalarGridSpec(
            num_scalar_prefetch=1, grid=(bsz,), in_specs=in_specs,
            out_specs=(per_b((1, NSA_HEADS * HD)), win_spec),
            scratch_shapes=[pltpu.VMEM((2, N_L, nch, GW), F32)]),
        input_output_aliases=aliases,
        compiler_params=_cparams(("arbitrary",)),
        name="nsa_sample",
    )(*args)
    return o.reshape(bsz, NSA_HEADS * HD), wout


def _prep_nsa(w_in, w_out, pe, w1, w2):
    n_gate = 3 * NSA_HEADS
    w_full = jnp.concatenate([w_in, jnp.zeros((D_MODEL, LANES - n_gate), F32)], axis=1)

    def k_padded(lo):
        k = w_in[:, lo:lo + GW].reshape(D_MODEL, KVH, HD)
        return jnp.concatenate([k, jnp.zeros_like(k)], axis=2).reshape(D_MODEL, KPAD)

    w_rows = jnp.concatenate([w_in[:, 1024:1024 + KVW], k_padded(1024 + KVW), k_padded(1024 + 2 * KVW)], axis=1)
    eye = jnp.eye(KVH, dtype=F32)
    w1r = w1.reshape(2, 2, N_L, HD, HD)
    wab = jnp.einsum('khlde,gf->klgdhfe', w1r, eye).reshape(2, N_L, GW, 2 * GW)
    w2bd = jnp.einsum('kde,gf->kgdfe', w2, eye)
    w2k_pad = jnp.concatenate([w2bd[0], jnp.zeros_like(w2bd[0])], axis=3).reshape(GW, KPAD)
    per = pe.reshape(2, 2, N_L, 1, 1, HD)
    pe_t = jnp.broadcast_to(per, (2, 2, N_L, 8, KVH, HD)).transpose(0, 2, 1, 3, 4, 5).reshape(2, N_L, PE_ROWS, GW)
    b = lambda a: a.astype(BF16)
    return dict(w_sample=b(w_full), w_rows=b(w_rows), w_t=b(w_full.T), wo=b(w_out), pe_t=b(pe_t), wab=b(wab),
                w2bd=b(w2bd.reshape(2, GW, GW)), w2k_pad=b(w2k_pad))


def _nsa_prompt_layer(x, nw, shift, scale, prep, bsz, t, layer, n_layers, kv_prev):
    rows, kpad, qt, gt, vt, *kv_t = _nsa_proj(x, nw, shift, scale, prep["w_rows"], prep["w_t"], bsz, t,
                                              layer, n_layers, kv_prev)
    kc, vct = _cmp_prompt(rows.reshape(bsz, t, KVW), prep["pe_t"], prep["wab"], prep["w2k_pad"],
                          prep["w2bd"][1])
    return _nsa_prompt_attention(qt, gt, kc, vct, kpad, vt), tuple(kv_t)


def _prep_ssd(w_in, dt_bias, a_log, d_skip):
    pad = SSD_NP - w_in.shape[1]
    w = jnp.concatenate([w_in, jnp.zeros((D_MODEL, pad), F32)], axis=1).astype(BF16)
    pad_h = lambda v: jnp.concatenate([v, jnp.zeros((LANES - SSD_HEADS,), F32)]).reshape(1, LANES)
    return w, pad_h(dt_bias), pad_h(a_log), jnp.repeat(d_skip, SSD_P).reshape(1, SSD_D_INNER)


def kernel(x_prompt, x_sample, cache_kv_cmp, cache_kv_sel, cache_kv_win, state_ssm, state_conv, page_table, c_prompt, c_sample, ada_w, ada_b, norm_w, mlp_w1, mlp_w2, nsa_w_in, nsa_w_out, nsa_cmp_pe, nsa_cmp_w1, nsa_cmp_w2, ssd_w_in, ssd_conv_w, ssd_conv_b, ssd_dt_bias, ssd_a_log, ssd_d, ssd_norm_w, ssd_w_out, final_norm_w):
    bp, t, d = x_prompt.shape
    bs = x_sample.shape[0]
    xp = x_prompt.reshape(bp * t, d)
    xs = x_sample.reshape(bs, d)
    mods = _adaln(jnp.concatenate([c_prompt, c_sample], axis=0), ada_w, ada_b)
    w1b = mlp_w1.astype(BF16)
    w2b = mlp_w2.astype(BF16)
    fnw = final_norm_w.reshape(1, d)
    tm = 512
    n_nsa, n_pool = cache_kv_cmp.shape[:2]
    cmp_view = jnp.transpose(cache_kv_cmp, (0, 1, 3, 4, 5, 2))
    sel_view = jnp.transpose(cache_kv_sel, (0, 1, 3, 4, 5, 2))
    win_view = jnp.transpose(cache_kv_win, (0, 1, 3, 4, 5, 2))
    win_s = ssm_s = kv_p = None
    outs = {k: [] for k in ("cmp_s", "sel_s", "ssm_p", "conv_p", "conv_s")}
    kv_rows = lambda a, n: a.reshape(n + (2, KVH, HD))

    for i in range(DEPTH):
        jl = i // 2
        mp = [mods[i, :bp, k * d:(k + 1) * d].reshape(bp, 1, d) for k in range(6)]
        ms = [mods[i, bp:, k * d:(k + 1) * d].reshape(1, bs, d) for k in range(6)]
        nw0 = norm_w[i, 0].reshape(1, d)
        nw1 = norm_w[i, 1].reshape(1, d)
        if i % 2 == 0:
            prep = _prep_nsa(nsa_w_in[jl], nsa_w_out[jl], nsa_cmp_pe[jl], nsa_cmp_w1[jl], nsa_cmp_w2[jl])
            wo = prep["wo"]
            ap, kv_p = _nsa_prompt_layer(xp, nw0, mp[0], mp[1], prep, bp, t, jl, n_nsa, kv_p)
            ps = _mod_matmul(xs, nw0, ms[0], ms[1], prep["w_sample"], tm=bs, tn=896, rows_per_mod=None)
            as_, win_s = _nsa_sample(ps, cmp_view, sel_view, win_view, jl, win_s, page_table, prep["pe_t"],
                                     prep["wab"], prep["w2bd"])
            outs["cmp_s"].append(kv_rows(ps[:, 1024:1024 + KVW], (bs, 1)))
            outs["sel_s"].append(kv_rows(ps[:, 1024 + KVW:1024 + 2 * KVW], (bs, 1)))
        else:
            w, dtb, alog, dsk = _prep_ssd(ssd_w_in[jl], ssd_dt_bias[jl], ssd_a_log[jl], ssd_d[jl])
            wo = ssd_w_out[jl].astype(BF16)
            snw = ssd_norm_w[jl].reshape(1, SSD_D_INNER)
            cw = ssd_conv_w[jl]
            cb = ssd_conv_b[jl].reshape(1, SSD_CONV_DIM)
            pp = _mod_matmul(xp, nw0, mp[0], mp[1], w, tm=tm, tn=SSD_NP, rows_per_mod=t)
            ps = _mod_matmul(xs, nw0, ms[0], ms[1], w, tm=bs, tn=896, rows_per_mod=None)
            ap, conv_p, ssm_p = _ssd_prompt(pp, bp, t, cw, cb, dtb, alog, dsk, snw)
            as_, conv_s, ssm_s = _ssd_sample(ps, state_conv[jl], state_ssm, jl, ssm_s, cw, cb, dtb, alog, dsk, snw)
            outs["conv_p"].append(conv_p)
            outs["conv_s"].append(conv_s)
            outs["ssm_p"].append(ssm_p.reshape(bp, SSD_HEADS, SSD_P, SSD_N))
        last = i == DEPTH - 1
        xp = _post_mlp(xp, ap, wo, mp[2], nw1, mp[3], mp[4], mp[5], w1b[i], w2b[i], fnw,
                       tm=tm, tf=1024, rows_per_mod=t, final_norm=last)
        xs = _post_mlp(xs, as_, wo, ms[2], nw1, ms[3], ms[4], ms[5], w1b[i], w2b[i], fnw,
                       tm=bs, tf=512, rows_per_mod=None, final_norm=last)

    st = lambda k: jnp.stack(outs[k])
    time_major = lambda a: jnp.transpose(a, (0, 1, 5, 2, 3, 4))
    cmp_p, sel_p, win_p = (time_major(a.reshape(a.shape[:2] + (2, KVH, HD, a.shape[-1]))) for a in kv_p)
    return (xp.reshape(bp, t, d), xs.reshape(bs, 1, d), cmp_p, st("cmp_s"), sel_p, st("sel_s"), win_p,
            time_major(win_s), st("ssm_p"), ssm_s.reshape(state_ssm.shape), st("conv_p"), st("conv_s"))
```

```python
import functools
import math

import numpy as np
import jax
import jax.numpy as jnp
from jax import lax
from jax.experimental import pallas as pl
from jax.experimental.pallas import tpu as pltpu

F32 = jnp.float32
BF16 = jnp.bfloat16

D_MODEL = 1024
DEPTH = 4
D_FF = 4 * D_MODEL
EPS = 1e-6
PAGE_SIZE = 128
NSA_HEADS = 16
HD = 64
KVH = 4
REP = 4
KVW = 2 * KVH * HD
GW = KVH * HD
CMP_BLOCK = 32
CMP_STRIDE = 16
SEL_BLOCK = 64
SEL_TOPK = 8
WINDOW = 512
SSD_D_INNER = 2 * D_MODEL
SSD_P = 64
SSD_HEADS = SSD_D_INNER // SSD_P
SSD_GROUPS = 4
SSD_REP = SSD_HEADS // SSD_GROUPS
SSD_N = 128
SSD_CONV_W = 4
SSD_CHUNK = 256
SSD_BC = 2 * SSD_GROUPS * SSD_N
SSD_CONV_DIM = SSD_D_INNER + SSD_BC
BIG = 1e30
NEG = -1e30
NEG_MXU = -2.0 ** 100
LOG2E = math.log2(math.e)
ONES_ROWS = 16
LANES = 128
NSA_N = 1024 + 3 * KVW + LANES
KPAD = KVH * LANES
SSD_NP = SSD_D_INNER + SSD_CONV_DIM + 2 * LANES
VMEM_LIMIT = 48 * 1024 * 1024
ROW_BLOCK = 512
FF_CHUNK = 1024
SAMPLE_COL_BLOCK = 7 * LANES


def _cparams(sem):
    return pltpu.CompilerParams(dimension_semantics=sem, vmem_limit_bytes=VMEM_LIMIT)


def _dot(a, b):
    return jnp.dot(a, b, preferred_element_type=F32)


def _dot_nt(a, b):
    return lax.dot_general(a, b, (((1,), (1,)), ((), ())), preferred_element_type=F32)


def _split(x, n):
    parts, r = [], x
    for i in range(n):
        p = r.astype(BF16)
        parts.append(p)
        if i + 1 < n:
            r = r - p.astype(F32)
    return parts


def _dot_x01(x, m01, n=3):
    acc = None
    for p in _split(x, n):
        t = _dot(p, m01)
        acc = t if acc is None else acc + t
    return acc


def _dot_01x(m01, x, n=3):
    acc = None
    for p in _split(x, n):
        t = _dot(m01, p)
        acc = t if acc is None else acc + t
    return acc


def _sigmoid(x):
    return 0.5 + 0.5 * jnp.tanh(0.5 * x)


def _silu(x):
    return x * _sigmoid(x)


def _softplus(x):
    return jnp.maximum(x, 0.0) + jnp.log(1.0 + jnp.exp(-jnp.abs(x)))


def _modulated_norm(x, nw, shift, scale):
    r = lax.rsqrt(jnp.mean(x * x, axis=-1, keepdims=True) + EPS)
    return (x * r) * nw * (1.0 + scale) + shift


def _adaln_kernel(c_ref, w_ref, b_ref, o_ref):
    a = _silu(c_ref[...]).astype(BF16)
    o_ref[0] = _dot(a, w_ref[0].astype(BF16)) + b_ref[0]


def _adaln(c_all, ada_w, ada_b):
    m, d = c_all.shape
    n = ada_w.shape[-1]
    tn = 1536
    return pl.pallas_call(
        _adaln_kernel,
        out_shape=jax.ShapeDtypeStruct((DEPTH, m, n), F32),
        grid=(DEPTH, n // tn),
        in_specs=[pl.BlockSpec((m, d), lambda l, j: (0, 0)),
                  pl.BlockSpec((1, d, tn), lambda l, j: (l, 0, j)),
                  pl.BlockSpec((1, 1, tn), lambda l, j: (l, 0, j))],
        out_specs=pl.BlockSpec((1, m, tn), lambda l, j: (l, 0, j)),
        compiler_params=_cparams(("parallel", "parallel")),
        name="adaln",
    )(c_all, ada_w, ada_b.reshape(DEPTH, 1, n))


def _modmm_kernel(x_ref, nw_ref, sh_ref, sc_ref, w_ref, o_ref, h_ref):
    @pl.when(pl.program_id(1) == 0)
    def _():
        h_ref[...] = _modulated_norm(x_ref[...], nw_ref[...], sh_ref[0], sc_ref[0]).astype(BF16)

    o_ref[...] = _dot(h_ref[...], w_ref[...])


def _mod_rows(tm, rows_per_mod):
    if rows_per_mod is None:
        return lambda shape: pl.BlockSpec((1,) + shape[1:], lambda i, j: (0, 0, 0))
    bpb = rows_per_mod // tm
    return lambda shape: pl.BlockSpec((1,) + shape[1:], lambda i, j: (i // bpb, 0, 0))


def _mod_matmul(x, nw, shift, scale, w, *, tm, tn, rows_per_mod):
    m, d = x.shape
    n = w.shape[1]
    spec = _mod_rows(tm, rows_per_mod)
    w_spec = (pl.BlockSpec((d, tn), lambda i, j: (0, 0), pipeline_mode=pl.Buffered(1)) if tn == n
              else pl.BlockSpec((d, tn), lambda i, j: (0, j)))
    return pl.pallas_call(
        _modmm_kernel,
        out_shape=jax.ShapeDtypeStruct((m, n), F32),
        grid=(m // tm, n // tn),
        in_specs=[pl.BlockSpec((tm, d), lambda i, j: (i, 0)),
                  pl.BlockSpec((1, d), lambda i, j: (0, 0)),
                  spec(shift.shape), spec(scale.shape),
                  w_spec],
        out_specs=pl.BlockSpec((tm, tn), lambda i, j: (i, j)),
        scratch_shapes=[pltpu.VMEM((tm, d), BF16)],
        compiler_params=_cparams(("parallel", "arbitrary")),
        name="mod_matmul",
    )(x, nw, shift, scale, w)


def _post_mlp_kernel(x_ref, a_ref, wo_ref, g1_ref, nw_ref, sh_ref, sc_ref, g2_ref, w1_ref, w2_ref, fnw_ref, o_ref,
                     *, final_norm, tf):
    x1 = x_ref[...] + g1_ref[0] * _dot(a_ref[...], wo_ref[...])
    h = _modulated_norm(x1, nw_ref[...], sh_ref[0], sc_ref[0]).astype(BF16)
    acc = None
    for k in range(w1_ref.shape[1] // tf):
        u = jnp.maximum(_dot(h, w1_ref[:, k * tf:(k + 1) * tf]), 0.0)
        part = _dot((u * u).astype(BF16), w2_ref[k * tf:(k + 1) * tf, :])
        acc = part if acc is None else acc + part
    y = x1 + g2_ref[0] * acc
    if final_norm:
        r = lax.rsqrt(jnp.mean(y * y, axis=-1, keepdims=True) + EPS)
        y = (y * r) * fnw_ref[...]
    o_ref[...] = y


def _post_mlp(x, a, wo, g1, nw, shift, scale, g2, w1, w2, fnw, *, tm, tf, rows_per_mod, final_norm):
    m, d = x.shape
    ka = a.shape[1]
    ff = w1.shape[1]
    spec = _mod_rows(tm, rows_per_mod)
    resident = lambda shape: pl.BlockSpec(shape, lambda i, j: (0, 0), pipeline_mode=pl.Buffered(1))
    return pl.pallas_call(
        functools.partial(_post_mlp_kernel, final_norm=final_norm, tf=tf),
        out_shape=jax.ShapeDtypeStruct((m, d), F32),
        grid=(m // tm, 1),
        in_specs=[pl.BlockSpec((tm, d), lambda i, j: (i, 0)),
                  pl.BlockSpec((tm, ka), lambda i, j: (i, 0)),
                  resident((ka, d)),
                  spec(g1.shape),
                  pl.BlockSpec((1, d), lambda i, j: (0, 0)),
                  spec(shift.shape), spec(scale.shape), spec(g2.shape),
                  resident((d, ff)), resident((ff, d)),
                  pl.BlockSpec((1, d), lambda i, j: (0, 0))],
        out_specs=pl.BlockSpec((tm, d), lambda i, j: (i, 0)),
        compiler_params=_cparams(("parallel", "arbitrary")),
        name="post_mlp",
    )(x, a, wo, g1, nw, shift, scale, g2, w1, w2, fnw)


def _head_expand_matrix():
    e = np.zeros((LANES, SSD_D_INNER), np.float32)
    for h in range(SSD_HEADS):
        e[h, h * SSD_P:(h + 1) * SSD_P] = 1.0
    return jnp.asarray(e, BF16)


def _ssd_prompt_kernel(z_ref, x_ref, bc_ref, dt_ref, cwx_ref, cbx_ref, cwb_ref, cbb_ref, dtb_ref, alog_ref,
                       dskip_ref, nw_ref, exp_ref, tri_ref,
                       y_ref, conv_ref, ssm_ref, xs_ref, bs_ref, st_ref):
    c = pl.program_id(1)
    L = SSD_CHUNK
    tail = SSD_CONV_W - 1

    @pl.when(c == 0)
    def _():
        xs_ref[0:8, :] = jnp.zeros((8, SSD_D_INNER), F32)
        bs_ref[0:8, :] = jnp.zeros((8, SSD_BC), F32)
        st_ref[...] = jnp.zeros_like(st_ref)

    xs_ref[8:8 + L, :] = x_ref[...]
    bs_ref[8:8 + L, :] = bc_ref[...]

    def conv(buf, w_ref, b_ref):
        acc = b_ref[...] + buf[8 - tail:8 - tail + L, :] * w_ref[0:1, :]
        for k in range(1, SSD_CONV_W):
            acc = acc + buf[8 - tail + k:8 - tail + k + L, :] * w_ref[k:k + 1, :]
        return _silu(acc)

    x = conv(xs_ref, cwx_ref, cbx_ref)
    bcv = conv(bs_ref, cwb_ref, cbb_ref)
    xs_ref[8 - tail:8, :] = xs_ref[8 + L - tail:8 + L, :]
    bs_ref[8 - tail:8, :] = bs_ref[8 + L - tail:8 + L, :]

    @pl.when(c == pl.num_programs(1) - 1)
    def _():
        conv_ref[0, :, 0:SSD_D_INNER] = x_ref[L - tail:L, :]
        conv_ref[0, :, SSD_D_INNER:SSD_CONV_DIM] = bc_ref[L - tail:L, :]

    lane = lax.broadcasted_iota(jnp.int32, (1, LANES), 1)
    head_ok = lane < SSD_HEADS
    dt = jnp.where(head_ok, _softplus(dt_ref[...] + dtb_ref[...]), 0.0)
    a = jnp.where(head_ok, -jnp.exp(alog_ref[...]) * LOG2E, 0.0)
    acum = _dot_01x(tri_ref[...], dt * a)
    acum_t = acum.T
    a_last = acum[L - 1:L, :]
    expand = exp_ref[...]
    dt_e = _dot_x01(dt, expand)
    eac_e = _dot_x01(jnp.exp2(acum), expand)
    dend_e = _dot_x01(jnp.exp2(a_last - acum), expand)
    cdec_e = eac_e[L - 1:L, :]

    xdt = x * dt_e
    xdt_b = xdt.astype(BF16)
    xdtw_b = (xdt * dend_e).astype(BF16)
    row = lax.broadcasted_iota(jnp.int32, (L, L), 0)
    col = lax.broadcasted_iota(jnp.int32, (L, L), 1)
    causal = row >= col
    lane2 = lax.broadcasted_iota(jnp.int32, (L, 2 * SSD_P), 1)
    first_head = lane2 < SSD_P

    y_parts = []
    for g in range(SSD_GROUPS):
        b_g = bcv[:, g * SSD_N:(g + 1) * SSD_N]
        c_g = bcv[:, (SSD_GROUPS + g) * SSD_N:(SSD_GROUPS + g + 1) * SSD_N]
        b_gb = b_g.astype(BF16)
        c_gb = c_g.astype(BF16)
        cb = _dot_nt(c_gb, b_gb)
        gl = slice(g * SSD_REP * SSD_P, (g + 1) * SSD_REP * SSD_P)
        st_g = st_ref[:, gl]
        y_off = _dot(c_gb, st_g.astype(BF16)) * eac_e[:, gl]
        pair_out = []
        for j in range(SSD_REP // 2):
            pl_ = slice(g * SSD_REP * SSD_P + j * 2 * SSD_P, g * SSD_REP * SSD_P + (j + 1) * 2 * SSD_P)
            xp = xdt_b[:, pl_]
            ys = []
            for hh in range(2):
                h = g * SSD_REP + 2 * j + hh
                seg = acum[:, h:h + 1] - acum_t[h:h + 1, :]
                dec = jnp.exp2(jnp.where(causal, seg, NEG))
                ys.append(_dot((cb * dec).astype(BF16), xp))
            pair_out.append(jnp.where(first_head, ys[0], ys[1]))
        y_g = jnp.concatenate(pair_out, axis=1) + y_off
        y_parts.append(y_g)
        st_ref[:, gl] = st_g * cdec_e[:, gl] + _dot(b_g.T.astype(BF16), xdtw_b[:, gl])

    y = jnp.concatenate(y_parts, axis=1) + dskip_ref[...] * x
    zg = z_ref[...]
    y = y * _silu(zg)
    r = lax.rsqrt(jnp.mean(y * y, axis=-1, keepdims=True) + EPS)
    y_ref[...] = ((y * r) * nw_ref[...]).astype(BF16)

    @pl.when(c == pl.num_programs(1) - 1)
    def _():
        ssm_ref[0] = st_ref[...].T


def _ssd_prompt(proj, bsz, t, cw, cb, dtb, alog, dskip, nw):
    L = SSD_CHUNK
    nc = t // L
    di = SSD_D_INNER
    tri = jnp.asarray(np.tril(np.ones((L, L), np.float32)), BF16)
    full = lambda shape: pl.BlockSpec(shape, lambda b, c: (0,) * len(shape))
    return pl.pallas_call(
        _ssd_prompt_kernel,
        out_shape=(jax.ShapeDtypeStruct((bsz * t, di), BF16),
                   jax.ShapeDtypeStruct((bsz, SSD_CONV_W - 1, SSD_CONV_DIM), F32),
                   jax.ShapeDtypeStruct((bsz, di, SSD_N), F32)),
        grid=(bsz, nc),
        in_specs=[pl.BlockSpec((L, di), lambda b, c: (b * nc + c, 0)),
                  pl.BlockSpec((L, di), lambda b, c: (b * nc + c, 1)),
                  pl.BlockSpec((L, SSD_BC), lambda b, c: (b * nc + c, 2 * di // SSD_BC)),
                  pl.BlockSpec((L, LANES), lambda b, c: (b * nc + c, (2 * di + SSD_BC) // LANES)),
                  full((SSD_CONV_W, di)), full((1, di)), full((SSD_CONV_W, SSD_BC)), full((1, SSD_BC)),
                  full((1, LANES)), full((1, LANES)), full((1, di)), full((1, di)),
                  full((LANES, di)), full((L, L))],
        out_specs=(pl.BlockSpec((L, di), lambda b, c: (b * nc + c, 0)),
                   pl.BlockSpec((1, SSD_CONV_W - 1, SSD_CONV_DIM), lambda b, c: (b, 0, 0)),
                   pl.BlockSpec((1, di, SSD_N), lambda b, c: (b, 0, 0))),
        scratch_shapes=[pltpu.VMEM((8 + L, di), F32), pltpu.VMEM((8 + L, SSD_BC), F32),
                        pltpu.VMEM((SSD_N, di), F32)],
        compiler_params=_cparams(("parallel", "arbitrary")),
        name="ssd_prompt",
    )(proj, proj, proj, proj, cw[:, :di], cb[:, :di], cw[:, di:], cb[:, di:], dtb, alog, dskip, nw,
      _head_expand_matrix(), tri)


def _ssd_sample_kernel(p_ref, cs_ref, st_ref, cw_ref, cb_ref, dtb_ref, alog_ref, dskip_ref, nw_ref, exp_ref, *rest):
    y_ref, conv_ref, ssm_ref = rest[-3:]
    di = SSD_D_INNER
    z = p_ref[0, :, 0:di]
    xbc = p_ref[0, :, di:di + SSD_CONV_DIM]
    dtr = p_ref[0, :, di + SSD_CONV_DIM:di + SSD_CONV_DIM + LANES]
    cs = cs_ref[0]
    acc = cb_ref[...] + xbc * cw_ref[SSD_CONV_W - 1:SSD_CONV_W, :]
    for k in range(SSD_CONV_W - 1):
        acc = acc + cs[k:k + 1, :] * cw_ref[k:k + 1, :]
    conv_ref[0, 0:SSD_CONV_W - 2, :] = cs[1:SSD_CONV_W - 1, :]
    conv_ref[0, SSD_CONV_W - 2:SSD_CONV_W - 1, :] = xbc
    act = _silu(acc)
    x = act[:, 0:di]
    lane = lax.broadcasted_iota(jnp.int32, (1, LANES), 1)
    head_ok = lane < SSD_HEADS
    dt = jnp.where(head_ok, _softplus(dtr + dtb_ref[...]), 0.0)
    a = jnp.where(head_ok, -jnp.exp(alog_ref[...]), 0.0)
    da = jnp.exp(dt * a)
    lhs = jnp.concatenate([dt, da, jnp.zeros((6, LANES), F32)], axis=0)
    ex = _dot_x01(lhs, exp_ref[...])
    dt_e, da_e = ex[0:1, :], ex[1:2, :]
    xdt = x * dt_e

    eye = (lax.broadcasted_iota(jnp.int32, (LANES, LANES), 0)
           == lax.broadcasted_iota(jnp.int32, (LANES, LANES), 1))

    def to_col(rowvec):
        return jnp.sum(jnp.where(eye, jnp.broadcast_to(rowvec, (LANES, LANES)), 0.0), axis=1, keepdims=True)

    def to_row(colvec):
        return jnp.sum(jnp.where(eye, jnp.broadcast_to(colvec, (LANES, LANES)), 0.0), axis=0, keepdims=True)

    y_off = []
    cbs = []
    for g in range(SSD_GROUPS):
        b_g = act[:, di + g * SSD_N:di + (g + 1) * SSD_N]
        c_g = act[:, di + (SSD_GROUPS + g) * SSD_N:di + (SSD_GROUPS + g + 1) * SSD_N]
        cbs.append(jnp.broadcast_to(jnp.sum(b_g * c_g, axis=1, keepdims=True), (1, SSD_REP * SSD_P)))
        for i in range(SSD_REP * SSD_P // LANES):
            lo = g * SSD_REP * SSD_P + i * LANES
            st = st_ref[0, 0, lo:lo + LANES, :]
            xcol = to_col(xdt[:, lo:lo + LANES])
            dcol = to_col(da_e[:, lo:lo + LANES])
            ssm_ref[0, 0, lo:lo + LANES, :] = st * dcol + xcol * b_g
            y_off.append(to_row(jnp.sum(st * c_g, axis=1, keepdims=True)))
    y = xdt * jnp.concatenate(cbs, axis=1) + jnp.concatenate(y_off, axis=1) * da_e + dskip_ref[...] * x
    y = y * _silu(z)
    r = lax.rsqrt(jnp.mean(y * y, axis=-1, keepdims=True) + EPS)
    y_ref[0] = ((y * r) * nw_ref[...]).astype(BF16)


def _ssd_sample(proj, conv_state, ssm_states, layer, ssm_out_prev, cw, cb, dtb, alog, dskip, nw):
    n_layers, bsz = ssm_states.shape[:2]
    di = SSD_D_INNER
    full = lambda shape: pl.BlockSpec(shape, lambda b: (0,) * len(shape))
    per_b = lambda shape: pl.BlockSpec((1,) + shape, lambda b: (b,) + (0,) * len(shape))
    state_spec = pl.BlockSpec((1, 1, di, SSD_N), lambda b: (layer, b, 0, 0))
    in_specs = [per_b((1, SSD_NP)), per_b((SSD_CONV_W - 1, SSD_CONV_DIM)), state_spec,
                full((SSD_CONV_W, SSD_CONV_DIM)), full((1, SSD_CONV_DIM)), full((1, LANES)), full((1, LANES)),
                full((1, di)), full((1, di)), full((LANES, di))]
    args = [proj.reshape(bsz, 1, SSD_NP), conv_state, ssm_states.reshape(n_layers, bsz, di, SSD_N), cw, cb, dtb, alog,
            dskip, nw, _head_expand_matrix()]
    aliases = {}
    if ssm_out_prev is not None:
        in_specs.append(pl.BlockSpec(memory_space=pl.ANY))
        aliases = {len(args): 2}
        args.append(ssm_out_prev)
    y, conv, ssm = pl.pallas_call(
        _ssd_sample_kernel,
        out_shape=(jax.ShapeDtypeStruct((bsz, 1, di), BF16),
                   jax.ShapeDtypeStruct((bsz, SSD_CONV_W - 1, SSD_CONV_DIM), F32),
                   jax.ShapeDtypeStruct((n_layers, bsz, di, SSD_N), F32)),
        grid=(bsz,),
        in_specs=in_specs,
        out_specs=(per_b((1, di)), per_b((SSD_CONV_W - 1, SSD_CONV_DIM)), state_spec),
        input_output_aliases=aliases,
        compiler_params=_cparams(("parallel",)),
        name="ssd_sample",
    )(*args)
    return y.reshape(bsz, di), conv, ssm


CH = CMP_STRIDE
N_L = CMP_BLOCK // 2
PE_ROWS = 16


def _compress_accumulate(lhs_fn, pe_ref, wab_ref, kv):
    acc = None
    for l in range(N_L):
        lhs = jnp.concatenate([lhs_fn(l, kv).astype(BF16), pe_ref[kv, l]], axis=0)
        t = _dot(lhs, wab_ref[kv, l])
        acc = t if acc is None else acc + t
    return acc


def _compress_finish(acc, w2, nch):
    p = acc[0:nch, 0:GW]
    q_next = pltpu.roll(acc[0:nch, GW:2 * GW], nch - 1, axis=0)
    bias = acc[nch:nch + 1, 0:GW] + acc[nch + 8:nch + 9, GW:2 * GW]
    out = _dot(_silu(p + q_next + bias).astype(BF16), w2)
    row = lax.broadcasted_iota(jnp.int32, (nch, 1), 0)
    return jnp.where(row < nch - 1, out, 0.0)


def _softmax_masked(s, ok, axis, exp=jnp.exp):
    s = jnp.where(ok, s, NEG)
    m = jnp.max(s, axis=axis, keepdims=True)
    p = jnp.where(ok, exp(s - m), 0.0)
    d = jnp.sum(p, axis=axis, keepdims=True)
    return p * jnp.where(d > 0.0, 1.0 / d, 0.0)


def _interleave(n, scores, softmax, values, finish):
    out, pending = [], None
    s_next = scores(0)
    for k in range(n):
        s = s_next
        if k + 1 < n:
            s_next = scores(k + 1)
        p, aux = softmax(k, s)
        if pending is not None:
            out.append(finish(*pending))
        pending = (k, values(k, p), aux)
    out.append(finish(*pending))
    return out


def _topk_mask(score, j, width, shifts, axis):
    n = score.shape[axis]
    rank = jnp.zeros(score.shape, F32)
    for k in shifts:
        other = pltpu.roll(score, k, axis=axis)
        lower = j >= k
        if width != n:
            other = jnp.where(lower, other, pltpu.roll(score, n - width + k, axis=axis))
        rank = rank + jnp.where(lower, jnp.where(other >= score, 1.0, 0.0), jnp.where(other > score, 1.0, 0.0))
    return rank < SEL_TOPK


def _overlap(n_cmp, n_sel):
    c_start = np.arange(n_cmp) * CMP_STRIDE
    s_start = np.arange(n_sel) * SEL_BLOCK
    return ((c_start[:, None] < s_start[None, :] + SEL_BLOCK)
            & (c_start[:, None] + CMP_BLOCK > s_start[None, :])).astype(np.float32)


def _block_bias_matrix(width, n_keys):
    l = np.arange(LANES)[:, None] % width
    blk = np.arange(n_keys)[None, :] // SEL_BLOCK
    return np.where(l == blk, NEG_MXU, 0.0).astype(np.float32)


def _nsa_proj_kernel(x_ref, nw_ref, sh_ref, sc_ref, wr_ref, wt_ref, *rest):
    rows_ref, kpad_ref, qt_ref, gt_ref, vt_ref, cmp_ref, sel_ref, win_ref = rest[-8:]
    h = _modulated_norm(x_ref[...], nw_ref[...], sh_ref[0], sc_ref[0]).astype(BF16)
    y = _dot(h, wr_ref[...])
    rows_ref[...] = y[:, 0:KVW].astype(BF16)
    kpad_ref[...] = y[:, KVW:].astype(BF16)
    yt = _dot_nt(wt_ref[...], h)
    qt_ref[0] = (yt[0:1024] * (HD ** -0.5 * LOG2E)).astype(BF16)
    cmp_ref[0, 0] = yt[1024:1024 + KVW]
    sel_ref[0, 0] = yt[1024 + KVW:1024 + 2 * KVW]
    vt_ref[0, 0:GW] = yt[1024 + KVW + GW:1024 + 2 * KVW].astype(BF16)
    vt_ref[0, GW:2 * GW] = yt[1024 + 2 * KVW + GW:1024 + 3 * KVW].astype(BF16)
    gt_ref[0] = _sigmoid(yt[1024 + 3 * KVW:NSA_N])

    @pl.when(pl.program_id(1) == pl.num_programs(1) - 1)
    def _():
        win_ref[0, 0] = yt[1024 + 2 * KVW:1024 + 3 * KVW]


def _nsa_proj(x, nw, shift, scale, wr, wt, bsz, t, layer, n_layers, kv_prev):
    d = x.shape[1]
    tm = min(WINDOW, t)
    nb = t // tm
    row = lambda w: pl.BlockSpec((tm, w), lambda b, i: (b * nb + i, 0))
    col = lambda h: pl.BlockSpec((1, h, tm), lambda b, i: (b, 0, i))
    kv_col = pl.BlockSpec((1, 1, KVW, tm), lambda b, i: (layer, b, 0, i))
    mod = lambda: pl.BlockSpec((1, 1, d), lambda b, i: (b, 0, 0))
    nq = NSA_HEADS * HD
    in_specs = [row(d), pl.BlockSpec((1, d), lambda b, i: (0, 0)), mod(), mod(),
                pl.BlockSpec((d, KVW + 2 * KPAD), lambda b, i: (0, 0)),
                pl.BlockSpec((NSA_N, d), lambda b, i: (0, 0))]
    args = [x, nw, shift, scale, wr, wt]
    aliases = {}
    if kv_prev is not None:
        aliases = {len(args) + k: 5 + k for k in range(3)}
        in_specs += [pl.BlockSpec(memory_space=pl.ANY)] * 3
        args += list(kv_prev)
    return pl.pallas_call(
        _nsa_proj_kernel,
        out_shape=(jax.ShapeDtypeStruct((bsz * t, KVW), BF16),
                   jax.ShapeDtypeStruct((bsz * t, 2 * KPAD), BF16),
                   jax.ShapeDtypeStruct((bsz, nq, t), BF16),
                   jax.ShapeDtypeStruct((bsz, LANES, t), F32),
                   jax.ShapeDtypeStruct((bsz, 2 * GW, t), BF16),
                   jax.ShapeDtypeStruct((n_layers, bsz, KVW, t), F32),
                   jax.ShapeDtypeStruct((n_layers, bsz, KVW, t), F32),
                   jax.ShapeDtypeStruct((n_layers, bsz, KVW, tm), F32)),
        grid=(bsz, nb),
        in_specs=in_specs,
        out_specs=(row(KVW), row(2 * KPAD), col(nq), col(LANES), col(2 * GW), kv_col, kv_col,
                   pl.BlockSpec((1, 1, KVW, tm), lambda b, i: (layer, b, 0, 0))),
        input_output_aliases=aliases,
        compiler_params=_cparams(("parallel", "arbitrary")),
        name="nsa_proj",
    )(*args)


def _cmp_prompt_kernel(rows_ref, perm_ref, pe_ref, wab_ref, w2k_ref, w2v_ref, kc_ref, vc_ref, xr_ref):
    nch = rows_ref.shape[1] // CH
    cpp = PAGE_SIZE // CH
    perm = perm_ref[...]
    for p in range(rows_ref.shape[1] // PAGE_SIZE):
        rows = _dot(perm, rows_ref[0, p * PAGE_SIZE:(p + 1) * PAGE_SIZE, :])
        for l in range(N_L):
            xr_ref[l, p * cpp:(p + 1) * cpp, :] = rows[l * cpp:(l + 1) * cpp, :]
    lhs = lambda l, kv: xr_ref[l, :, kv * GW:(kv + 1) * GW]
    kc_ref[0] = _compress_finish(_compress_accumulate(lhs, pe_ref, wab_ref, 0), w2k_ref[...], nch).astype(BF16)
    vc_ref[0] = _compress_finish(_compress_accumulate(lhs, pe_ref, wab_ref, 1), w2v_ref[...], nch).T.astype(BF16)


def _chunk_perm():
    cpp = PAGE_SIZE // CH
    perm = np.zeros((PAGE_SIZE, PAGE_SIZE), np.float32)
    for l in range(N_L):
        for c in range(cpp):
            perm[l * cpp + c, c * CH + l] = 1.0
    return jnp.asarray(perm, BF16)


def _cmp_prompt(rows, pe_t, wab, w2k_pad, w2v):
    bsz, t, _ = rows.shape
    nch = t // CH
    full = lambda shape: pl.BlockSpec(shape, lambda b: (0,) * len(shape))
    return pl.pallas_call(
        _cmp_prompt_kernel,
        out_shape=(jax.ShapeDtypeStruct((bsz, nch, KPAD), BF16), jax.ShapeDtypeStruct((bsz, GW, nch), BF16)),
        grid=(bsz,),
        in_specs=[pl.BlockSpec((1, t, KVW), lambda b: (b, 0, 0)), full((PAGE_SIZE, PAGE_SIZE)),
                  full(pe_t.shape), full(wab.shape), full(w2k_pad.shape), full(w2v.shape)],
        out_specs=(pl.BlockSpec((1, nch, KPAD), lambda b: (b, 0, 0)), pl.BlockSpec((1, GW, nch), lambda b: (b, 0, 0))),
        scratch_shapes=[pltpu.VMEM((N_L, nch, KVW), F32)],
        compiler_params=_cparams(("parallel",)),
        name="nsa_compress_prompt",
    )(rows, _chunk_perm(), pe_t, wab, w2k_pad, w2v)


def _nsa_prompt_kernel(qt_ref, gt_ref, kc_ref, vct_ref, kpad_ref, vt_ref, ovt_ref, nege_ref, o_ref,
                       *, tq, n_sel, ck):
    i = pl.program_id(1)
    t0 = i * tq
    nq = REP * tq
    tl = t0 + lax.broadcasted_iota(jnp.int32, (1, tq), 1)
    zq = jnp.zeros((HD, nq), BF16)

    def lanes4(a):
        return jnp.concatenate([a] * REP, axis=1)

    def with_ones(vt):
        return jnp.concatenate([vt, jnp.ones((ONES_ROWS, vt.shape[1]), BF16)], axis=0)

    def normalized(acc):
        return acc[0:HD] * (1.0 / acc[HD:HD + 1])

    tl4 = lanes4(tl)
    qg = [jnp.concatenate([qt_ref[0, (g * REP + r) * HD:(g * REP + r + 1) * HD, :] for r in range(REP)], axis=1)
          for g in range(KVH)]
    q_rhs = [jnp.concatenate([q, zq], axis=0) for q in qg]

    kw = WINDOW + tq
    k_start = pl.multiple_of(jnp.maximum(t0 - WINDOW, 0), LANES)
    dpos = tl - (k_start + lax.broadcasted_iota(jnp.int32, (kw, 1), 0))
    wbias = lanes4(jnp.where((dpos >= 0) & (dpos <= WINDOW), 0.0, NEG))

    def win_scores(g):
        return _dot(kpad_ref[0, pl.ds(k_start, kw), (KVH + g) * LANES:(KVH + g + 1) * LANES], q_rhs[g]) + wbias

    def win_softmax(g, s):
        return jnp.exp2(s - jnp.max(s, axis=0, keepdims=True)).astype(BF16), None

    def win_values(g, p):
        return _dot(with_ones(vt_ref[0, GW + g * HD:GW + (g + 1) * HD, pl.ds(k_start, kw)]), p)

    o_win = _interleave(KVH, win_scores, win_softmax, win_values, lambda g, pv, aux: normalized(pv))

    ncp = kc_ref.shape[1]
    cend = lax.broadcasted_iota(jnp.int32, (ncp, 1), 0) * CMP_STRIDE + (CMP_BLOCK - 1)
    ok_cmp = cend <= tl4
    s_cmp = [_dot(kc_ref[0, :, g * LANES:(g + 1) * LANES], q_rhs[g]) for g in range(KVH)]
    p_cmp = [_softmax_masked(s, ok_cmp, 0, jnp.exp2) for s in s_cmp]
    o_cmp = [_dot(vct_ref[0, g * HD:(g + 1) * HD, :], p_cmp[g].astype(BF16)) for g in range(KVH)]
    imp = jnp.zeros((LANES, tq), F32)
    for g, p in enumerate(p_cmp):
        psum = p[:, 0:tq] + p[:, tq:2 * tq] + p[:, 2 * tq:3 * tq] + p[:, 3 * tq:4 * tq]
        imp = imp + _dot_01x(ovt_ref[g], psum)

    row = lax.broadcasted_iota(jnp.int32, (LANES, tq), 0)
    jj = row % n_sel
    cur = tl // SEL_BLOCK
    valid = jj * SEL_BLOCK <= tl
    forced = (jj == 0) | (jj == cur) | (jj == cur - 1)
    score = jnp.where(valid, jnp.where(forced, BIG, imp), NEG)
    top = _topk_mask(score, jj, n_sel, range(1, n_sel), 0)
    blocked = jnp.where(top & valid, 0.0, 1.0)

    n_full = t0 // ck
    sel_rhs = [jnp.concatenate([lanes4(jnp.where(row // n_sel == g, blocked, 0.0).astype(BF16)), q_rhs[g]], axis=0)
               for g in range(KVH)]

    def sel_chunk(c, carries, causal):
        k0 = pl.multiple_of(c * ck, ck)
        bias_rows = nege_ref[pl.ds(k0, ck), :]
        if causal:
            cbias = jnp.where(k0 + lax.broadcasted_iota(jnp.int32, (ck, 1), 0) <= tl4, 0.0, NEG)

        def scores(g):
            lhs = jnp.concatenate([bias_rows, kpad_ref[0, pl.ds(k0, ck), g * LANES:(g + 1) * LANES]], axis=1)
            s = _dot(lhs, sel_rhs[g])
            return s + cbias if causal else s

        def softmax(g, s):
            m, acc = carries[g]
            m_new = jnp.maximum(m, jnp.max(s, axis=0, keepdims=True))
            return jnp.exp2(s - m_new).astype(BF16), (m_new, jnp.exp2(m - m_new), acc)

        def values(g, p):
            return _dot(with_ones(vt_ref[0, g * HD:(g + 1) * HD, pl.ds(k0, ck)]), p)

        def finish(g, pv, aux):
            m_new, alpha, acc = aux
            return m_new, alpha * acc + pv

        return tuple(_interleave(KVH, scores, softmax, values, finish))

    init = (jnp.full((1, nq), NEG, F32), jnp.zeros((HD + ONES_ROWS, nq), F32))
    carries = lax.fori_loop(0, n_full, lambda c, cr: sel_chunk(c, cr, False), (init,) * KVH)
    o_sel = [normalized(acc) for _, acc in sel_chunk(n_full, carries, True)]

    gt = gt_ref[0]
    outs = []
    for g in range(KVH):
        for r in range(REP):
            h = g * REP + r
            sl = slice(r * tq, (r + 1) * tq)
            outs.append(gt[3 * h:3 * h + 1, :] * o_cmp[g][:, sl] + gt[3 * h + 1:3 * h + 2, :] * o_sel[g][:, sl]
                        + gt[3 * h + 2:3 * h + 3, :] * o_win[g][:, sl])
    o_ref[0] = jnp.concatenate(outs, axis=0).T.astype(BF16)


def _nsa_prompt_attention(qt, gt, kc, vct, kpad, vt):
    bsz, nd, t = qt.shape
    tq = LANES
    ck = 4 * LANES
    n_sel = t // SEL_BLOCK
    ncp = t // CH
    assert KVH * n_sel == LANES and ncp == LANES and t % ck == 0
    ovt = np.zeros((KVH, LANES, ncp), np.float32)
    for g in range(KVH):
        ovt[g, g * n_sel:(g + 1) * n_sel, :ncp - 1] = _overlap(ncp - 1, n_sel).T
    nege_t = _block_bias_matrix(n_sel, t).T
    per_b = lambda shape: pl.BlockSpec((1,) + shape, lambda b, i: (b,) + (0,) * len(shape))
    full = lambda shape: pl.BlockSpec(shape, lambda b, i: (0,) * len(shape))
    out = pl.pallas_call(
        functools.partial(_nsa_prompt_kernel, tq=tq, n_sel=n_sel, ck=ck),
        out_shape=jax.ShapeDtypeStruct((bsz, t, nd), BF16),
        grid=(bsz, t // tq),
        in_specs=[pl.BlockSpec((1, nd, tq), lambda b, i: (b, 0, i)),
                  pl.BlockSpec((1, LANES, tq), lambda b, i: (b, 0, i)),
                  per_b((ncp, KPAD)), per_b((GW, ncp)), per_b((t, 2 * KPAD)), per_b((2 * GW, t)),
                  full((KVH, LANES, ncp)), full((t, LANES))],
        out_specs=pl.BlockSpec((1, tq, nd), lambda b, i: (b, i, 0)),
        compiler_params=_cparams(("parallel", "arbitrary")),
        name="nsa_prompt_attention",
    )(qt, gt, kc, vct, kpad.reshape(bsz, t, 2 * KPAD), vt, jnp.asarray(ovt, BF16), jnp.asarray(nege_t, BF16))
    return out.reshape(bsz * t, nd)


def _nsa_sample_kernel(pt_ref, p_ref, *rest, n_pages, past):
    del pt_ref
    cmp_pages = rest[:n_pages]
    sel_pages = rest[n_pages:2 * n_pages]
    win_ref, pe_ref, wab_ref, w2_ref, ov_ref, nege_ref, perm_ref = rest[2 * n_pages:2 * n_pages + 7]
    o_ref, wout_ref, xr_ref = rest[-3:]
    nch = n_pages * (PAGE_SIZE // CH)
    cpp = PAGE_SIZE // CH
    t = past
    keep = win_ref.shape[-1]
    rows_q = 8

    def compress_cache():
        perm = perm_ref[...]
        for p, pg in enumerate(cmp_pages):
            for kv in range(2):
                rows = _dot_nt(perm, pg[0, 0, kv].reshape(GW, PAGE_SIZE).astype(BF16))
                for l in range(N_L):
                    xr_ref[kv, l, p * cpp:(p + 1) * cpp, :] = rows[l * cpp:(l + 1) * cpp, :]
        out = []
        for kv in range(2):
            acc = _compress_accumulate(lambda l, kv: xr_ref[kv, l], pe_ref, wab_ref, kv)
            out.append(_compress_finish(acc, w2_ref[kv], nch).T.astype(BF16))
        return out

    lane = lax.broadcasted_iota(jnp.int32, (rows_q, LANES), 1)
    rowi = lax.broadcasted_iota(jnp.int32, (rows_q, LANES), 0)
    cend = lax.broadcasted_iota(jnp.int32, (1, nch), 1) * CMP_STRIDE + (CMP_BLOCK - 1)
    n_sel = -(-(t + 1) // SEL_BLOCK)
    nj = max(n_sel, SEL_TOPK)
    cur = t // SEL_BLOCK
    valid = (lane < n_sel) & (lane * SEL_BLOCK <= t)
    forced = (lane == 0) | (lane == cur) | (lane == cur - 1)
    shifts = list(range(1, nj)) + list(range(LANES - nj + 1, LANES))
    gates = jnp.broadcast_to(_sigmoid(p_ref[0, :, 1024 + 3 * KVW:NSA_N]), (rows_q, LANES))
    eye = (lax.broadcasted_iota(jnp.int32, (HD, HD), 0) == lax.broadcasted_iota(jnp.int32, (HD, HD), 1))
    last_lane = lax.broadcasted_iota(jnp.int32, (HD, keep), 1) == keep - 1

    G = range(KVH)
    new_row = lambda off, g: p_ref[0, :, off + g * HD:off + (g + 1) * HD]
    q = [jnp.concatenate([p_ref[0, :, (g * REP + r) * HD:(g * REP + r + 1) * HD] for r in range(REP)]
                         + [jnp.zeros((rows_q - REP, HD), F32)], axis=0) * (HD ** -0.5) for g in G]
    qb = [x.astype(BF16) for x in q]

    def softmax_with_new(s, s_new, ok_new):
        s_new = jnp.where(ok_new, s_new, NEG)
        m = jnp.maximum(jnp.max(s, axis=1, keepdims=True), s_new)
        pr = jnp.exp(s - m)
        pr_new = jnp.where(ok_new, jnp.exp(s_new - m), 0.0)
        return pr.astype(BF16), pr_new, 1.0 / (jnp.sum(pr, axis=1, keepdims=True) + pr_new)

    kw = [win_ref[0, 0, 0, g] for g in G]
    vw = [win_ref[0, 0, 1, g] for g in G]
    kw_new = [new_row(1024 + 2 * KVW, g) for g in G]
    vw_new = [new_row(1024 + 2 * KVW + GW, g) for g in G]
    s_win = [_dot(qb[g], kw[g].astype(BF16)) for g in G]
    sm = [softmax_with_new(s_win[g], jnp.sum(q[g] * kw_new[g], axis=1, keepdims=True), jnp.full((rows_q, 1), True))
          for g in G]
    pv = [_dot_nt(sm[g][0], vw[g].astype(BF16)) for g in G]
    o_win = [(pv[g] + sm[g][1] * vw_new[g]) * sm[g][2] for g in G]
    for g in G:
        for kv, old, new in ((0, kw[g], kw_new[g]), (1, vw[g], vw_new[g])):
            col = jnp.sum(jnp.where(eye, jnp.broadcast_to(new, (HD, HD)), 0.0), axis=1, keepdims=True)
            wout_ref[0, 0, kv, g] = jnp.where(last_lane, col, pltpu.roll(old, keep - 1, axis=1))

    kt = [jnp.concatenate([pg[0, 0, 0, g] for pg in sel_pages], axis=1).astype(BF16) for g in G]
    vt = [jnp.concatenate([pg[0, 0, 1, g] for pg in sel_pages], axis=1).astype(BF16) for g in G]

    kct, vct = compress_cache()
    s_cmp = [_dot(qb[g], kct[g * HD:(g + 1) * HD, :]) for g in G]
    p_cmp = [_softmax_masked(s, cend <= t, 1) for s in s_cmp]
    o_cmp = [_dot_nt(p_cmp[g].astype(BF16), vct[g * HD:(g + 1) * HD, :]) for g in G]
    allowed = []
    for g in G:
        psum = jnp.broadcast_to(jnp.sum(p_cmp[g][0:REP], axis=0, keepdims=True), (rows_q, nch))
        imp = _dot_x01(psum, ov_ref[...])
        score = jnp.where(valid, jnp.where(forced, BIG, imp), NEG)
        score = jnp.where(lane < nj, score, -3e38)
        allowed.append(_topk_mask(score, lane, LANES, shifts, 1) & valid)

    nege = nege_ref[...]
    s_sel = [_dot(jnp.concatenate([jnp.where(allowed[g], 0.0, 1.0).astype(BF16), qb[g]], axis=1),
                  jnp.concatenate([nege, kt[g]], axis=0)) for g in G]
    v_new = [new_row(1024 + KVW + GW, g) for g in G]
    sm = [softmax_with_new(s_sel[g], jnp.sum(q[g] * new_row(1024 + KVW, g), axis=1, keepdims=True),
                           jnp.sum(jnp.where((lane == cur) & allowed[g], 1.0, 0.0), axis=1, keepdims=True) > 0.5)
          for g in G]
    pv = [_dot_nt(sm[g][0], vt[g]) for g in G]
    o_sel = [(pv[g] + sm[g][1] * v_new[g]) * sm[g][2] for g in G]

    for g in G:
        o = None
        for br, ob in enumerate((o_cmp[g], o_sel[g], o_win[g])):
            gcol = jnp.sum(jnp.where(lane == (g * REP + rowi) * 3 + br, gates, 0.0), axis=1, keepdims=True)
            o = gcol * ob if o is None else o + gcol * ob
        for r in range(REP):
            h = g * REP + r
            o_ref[0, :, h * HD:(h + 1) * HD] = o[r:r + 1, :].astype(BF16)


def _nsa_sample(proj, cmp_view, sel_view, win_view, layer, win_out_prev, page_table, pe_t, wab, w2bd):
    n_layers = win_view.shape[0]
    bsz, n_pages = page_table.shape
    past = n_pages * PAGE_SIZE
    keep = win_view.shape[-1]
    nch = past // CH
    assert nch == LANES and keep <= WINDOW and past % SEL_BLOCK == 0 and past - keep >= 0
    n_sel = -(-(past + 1) // SEL_BLOCK)
    ov = np.zeros((nch, LANES), np.float32)
    ov[:nch - 1, :n_sel] = _overlap(nch - 1, n_sel)
    full = lambda shape: pl.BlockSpec(shape, lambda b, pt: (0,) * len(shape))
    once = lambda shape: pl.BlockSpec(shape, lambda b, pt: (0,) * len(shape), pipeline_mode=pl.Buffered(1))
    per_b = lambda shape: pl.BlockSpec((1,) + shape, lambda b, pt: (b,) + (0,) * len(shape))
    page_shape = (2, KVH, HD, PAGE_SIZE)
    page = lambda p: pl.BlockSpec((1, 1) + page_shape, lambda b, pt: (layer, pt[b * n_pages + p], 0, 0, 0, 0))
    win_shape = (2, KVH, HD, keep)
    win_spec = pl.BlockSpec((1, 1) + win_shape, lambda b, pt: (layer, b, 0, 0, 0, 0))
    in_specs = ([per_b((1, NSA_N))] + [page(p) for p in range(n_pages)] * 2
                + [win_spec, full(pe_t.shape), once(wab.shape), full(w2bd.shape), full((nch, LANES)),
                   full((LANES, past)), full((PAGE_SIZE, PAGE_SIZE))])
    args = [page_table.reshape(-1), proj.reshape(bsz, 1, NSA_N)] + [cmp_view] * n_pages + [sel_view] * n_pages + [
        win_view, pe_t, wab, w2bd, jnp.asarray(ov, BF16), jnp.asarray(_block_bias_matrix(LANES, past), BF16),
        _chunk_perm()]
    aliases = {}
    if win_out_prev is not None:
        in_specs.append(pl.BlockSpec(memory_space=pl.ANY))
        aliases = {len(args): 1}
        args.append(win_out_prev)
    o, wout = pl.pallas_call(
        functools.partial(_nsa_sample_kernel, n_pages=n_pages, past=past),
        out_shape=(jax.ShapeDtypeStruct((bsz, 1, NSA_HEADS * HD), BF16),
                   jax.ShapeDtypeStruct((n_layers, bsz) + win_shape, F32)),
        grid_spec=pltpu.PrefetchScalarGridSpec(
            num_scalar_prefetch=1, grid=(bsz,), in_specs=in_specs,
            out_specs=(per_b((1, NSA_HEADS * HD)), win_spec),
            scratch_shapes=[pltpu.VMEM((2, N_L, nch, GW), F32)]),
        input_output_aliases=aliases,
        compiler_params=_cparams(("arbitrary",)),
        name="nsa_sample",
    )(*args)
    return o.reshape(bsz, NSA_HEADS * HD), wout


def _prep_nsa(w_in, w_out, pe, w1, w2):
    n_gate = 3 * NSA_HEADS
    w_full = jnp.concatenate([w_in, jnp.zeros((D_MODEL, LANES - n_gate), F32)], axis=1)

    def k_padded(lo):
        k = w_in[:, lo:lo + GW].reshape(D_MODEL, KVH, HD)
        return jnp.concatenate([k, jnp.zeros_like(k)], axis=2).reshape(D_MODEL, KPAD)

    w_rows = jnp.concatenate([w_in[:, 1024:1024 + KVW], k_padded(1024 + KVW), k_padded(1024 + 2 * KVW)], axis=1)
    eye = jnp.eye(KVH, dtype=F32)
    w1r = w1.reshape(2, 2, N_L, HD, HD)
    wab = jnp.einsum('khlde,gf->klgdhfe', w1r, eye).reshape(2, N_L, GW, 2 * GW)
    w2bd = jnp.einsum('kde,gf->kgdfe', w2, eye)
    w2k_pad = jnp.concatenate([w2bd[0], jnp.zeros_like(w2bd[0])], axis=3).reshape(GW, KPAD)
    per = pe.reshape(2, 2, N_L, 1, 1, HD)
    pe_t = jnp.broadcast_to(per, (2, 2, N_L, 8, KVH, HD)).transpose(0, 2, 1, 3, 4, 5).reshape(2, N_L, PE_ROWS, GW)
    b = lambda a: a.astype(BF16)
    return dict(w_sample=b(w_full), w_rows=b(w_rows), w_t=b(w_full.T), wo=b(w_out), pe_t=b(pe_t), wab=b(wab),
                w2bd=b(w2bd.reshape(2, GW, GW)), w2k_pad=b(w2k_pad))


def _nsa_prompt_layer(x, nw, shift, scale, prep, bsz, t, layer, n_layers, kv_prev):
    rows, kpad, qt, gt, vt, *kv_t = _nsa_proj(x, nw, shift, scale, prep["w_rows"], prep["w_t"], bsz, t,
                                              layer, n_layers, kv_prev)
    kc, vct = _cmp_prompt(rows.reshape(bsz, t, KVW), prep["pe_t"], prep["wab"], prep["w2k_pad"],
                          prep["w2bd"][1])
    return _nsa_prompt_attention(qt, gt, kc, vct, kpad, vt), tuple(kv_t)


def _prep_ssd(w_in, dt_bias, a_log, d_skip):
    pad = SSD_NP - w_in.shape[1]
    w = jnp.concatenate([w_in, jnp.zeros((D_MODEL, pad), F32)], axis=1).astype(BF16)
    pad_h = lambda v: jnp.concatenate([v, jnp.zeros((LANES - SSD_HEADS,), F32)]).reshape(1, LANES)
    return w, pad_h(dt_bias), pad_h(a_log), jnp.repeat(d_skip, SSD_P).reshape(1, SSD_D_INNER)


def kernel(x_prompt, x_sample, cache_kv_cmp, cache_kv_sel, cache_kv_win, state_ssm, state_conv, page_table, c_prompt, c_sample, ada_w, ada_b, norm_w, mlp_w1, mlp_w2, nsa_w_in, nsa_w_out, nsa_cmp_pe, nsa_cmp_w1, nsa_cmp_w2, ssd_w_in, ssd_conv_w, ssd_conv_b, ssd_dt_bias, ssd_a_log, ssd_d, ssd_norm_w, ssd_w_out, final_norm_w):
    bp, t, d = x_prompt.shape
    bs = x_sample.shape[0]
    xp = x_prompt.reshape(bp * t, d)
    xs = x_sample.reshape(bs, d)
    mods = _adaln(jnp.concatenate([c_prompt, c_sample], axis=0), ada_w, ada_b)
    w1b = mlp_w1.astype(BF16)
    w2b = mlp_w2.astype(BF16)
    fnw = final_norm_w.reshape(1, d)
    tm = ROW_BLOCK
    n_nsa = cache_kv_cmp.shape[0]
    cmp_view = jnp.transpose(cache_kv_cmp, (0, 1, 3, 4, 5, 2))
    sel_view = jnp.transpose(cache_kv_sel, (0, 1, 3, 4, 5, 2))
    win_view = jnp.transpose(cache_kv_win, (0, 1, 3, 4, 5, 2))
    win_s = ssm_s = kv_p = None
    outs = {k: [] for k in ("cmp_s", "sel_s", "ssm_p", "conv_p", "conv_s")}
    kv_rows = lambda a, n: a.reshape(n + (2, KVH, HD))

    for i in range(DEPTH):
        jl = i // 2
        mp = [mods[i, :bp, k * d:(k + 1) * d].reshape(bp, 1, d) for k in range(6)]
        ms = [mods[i, bp:, k * d:(k + 1) * d].reshape(1, bs, d) for k in range(6)]
        nw0 = norm_w[i, 0].reshape(1, d)
        nw1 = norm_w[i, 1].reshape(1, d)
        if i % 2 == 0:
            prep = _prep_nsa(nsa_w_in[jl], nsa_w_out[jl], nsa_cmp_pe[jl], nsa_cmp_w1[jl], nsa_cmp_w2[jl])
            wo = prep["wo"]
            ap, kv_p = _nsa_prompt_layer(xp, nw0, mp[0], mp[1], prep, bp, t, jl, n_nsa, kv_p)
            ps = _mod_matmul(xs, nw0, ms[0], ms[1], prep["w_sample"], tm=bs, tn=SAMPLE_COL_BLOCK, rows_per_mod=None)
            as_, win_s = _nsa_sample(ps, cmp_view, sel_view, win_view, jl, win_s, page_table, prep["pe_t"],
                                     prep["wab"], prep["w2bd"])
            outs["cmp_s"].append(kv_rows(ps[:, 1024:1024 + KVW], (bs, 1)))
            outs["sel_s"].append(kv_rows(ps[:, 1024 + KVW:1024 + 2 * KVW], (bs, 1)))
        else:
            w, dtb, alog, dsk = _prep_ssd(ssd_w_in[jl], ssd_dt_bias[jl], ssd_a_log[jl], ssd_d[jl])
            wo = ssd_w_out[jl].astype(BF16)
            snw = ssd_norm_w[jl].reshape(1, SSD_D_INNER)
            cw = ssd_conv_w[jl]
            cb = ssd_conv_b[jl].reshape(1, SSD_CONV_DIM)
            pp = _mod_matmul(xp, nw0, mp[0], mp[1], w, tm=tm, tn=SSD_NP, rows_per_mod=t)
            ps = _mod_matmul(xs, nw0, ms[0], ms[1], w, tm=bs, tn=SAMPLE_COL_BLOCK, rows_per_mod=None)
            ap, conv_p, ssm_p = _ssd_prompt(pp, bp, t, cw, cb, dtb, alog, dsk, snw)
            as_, conv_s, ssm_s = _ssd_sample(ps, state_conv[jl], state_ssm, jl, ssm_s, cw, cb, dtb, alog, dsk, snw)
            outs["conv_p"].append(conv_p)
            outs["conv_s"].append(conv_s)
            outs["ssm_p"].append(ssm_p.reshape(bp, SSD_HEADS, SSD_P, SSD_N))
        last = i == DEPTH - 1
        xp = _post_mlp(xp, ap, wo, mp[2], nw1, mp[3], mp[4], mp[5], w1b[i], w2b[i], fnw,
                       tm=tm, tf=FF_CHUNK, rows_per_mod=t, final_norm=last)
        xs = _post_mlp(xs, as_, wo, ms[2], nw1, ms[3], ms[4], ms[5], w1b[i], w2b[i], fnw,
                       tm=bs, tf=FF_CHUNK, rows_per_mod=None, final_norm=last)

    st = lambda k: jnp.stack(outs[k])
    time_major = lambda a: jnp.transpose(a, (0, 1, 5, 2, 3, 4))
    cmp_p, sel_p, win_p = (time_major(a.reshape(a.shape[:2] + (2, KVH, HD, a.shape[-1]))) for a in kv_p)
    return (xp.reshape(bp, t, d), xs.reshape(bs, 1, d), cmp_p, st("cmp_s"), sel_p, st("sel_s"), win_p,
            time_major(win_s), st("ssm_p"), ssm_s.reshape(state_ssm.shape), st("conv_p"), st("conv_s"))
```

```python
import functools
import math

import numpy as np
import jax
import jax.numpy as jnp
from jax import lax
from jax.experimental import pallas as pl
from jax.experimental.pallas import tpu as pltpu

F32 = jnp.float32
BF16 = jnp.bfloat16

D_MODEL = 1024
DEPTH = 4
D_FF = 4 * D_MODEL
EPS = 1e-6
PAGE_SIZE = 128
NSA_HEADS = 16
HD = 64
KVH = 4
REP = 4
KVW = 2 * KVH * HD
GW = KVH * HD
CMP_BLOCK = 32
CMP_STRIDE = 16
SEL_BLOCK = 64
SEL_TOPK = 8
WINDOW = 512
SSD_D_INNER = 2 * D_MODEL
SSD_P = 64
SSD_HEADS = SSD_D_INNER // SSD_P
SSD_GROUPS = 4
SSD_REP = SSD_HEADS // SSD_GROUPS
SSD_N = 128
SSD_CONV_W = 4
SSD_CHUNK = 256
SSD_BC = 2 * SSD_GROUPS * SSD_N
SSD_CONV_DIM = SSD_D_INNER + SSD_BC
BIG = 1e30
NEG = -1e30
NEG_MXU = -2.0 ** 100
LOG2E = math.log2(math.e)
ONES_ROWS = 16
LANES = 128
NSA_N = 1024 + 3 * KVW + LANES
KPAD = KVH * LANES
SSD_NP = SSD_D_INNER + SSD_CONV_DIM + 2 * LANES
VMEM_LIMIT = 48 * 1024 * 1024
ROW_BLOCK = 512
FF_CHUNK = 1024
SAMPLE_COL_BLOCK = 7 * LANES


def _cparams(sem):
    return pltpu.CompilerParams(dimension_semantics=sem, vmem_limit_bytes=VMEM_LIMIT)


def _dot(a, b):
    return jnp.dot(a, b, preferred_element_type=F32)


def _dot_nt(a, b):
    return lax.dot_general(a, b, (((1,), (1,)), ((), ())), preferred_element_type=F32)


def _split(x, n):
    parts, r = [], x
    for i in range(n):
        p = r.astype(BF16)
        parts.append(p)
        if i + 1 < n:
            r = r - p.astype(F32)
    return parts


def _dot_x01(x, m01, n=3):
    acc = None
    for p in _split(x, n):
        t = _dot(p, m01)
        acc = t if acc is None else acc + t
    return acc


def _dot_01x(m01, x, n=3):
    acc = None
    for p in _split(x, n):
        t = _dot(m01, p)
        acc = t if acc is None else acc + t
    return acc


def _sigmoid(x):
    return 0.5 + 0.5 * jnp.tanh(0.5 * x)


def _silu(x):
    return x * _sigmoid(x)


def _softplus(x):
    return jnp.maximum(x, 0.0) + jnp.log(1.0 + jnp.exp(-jnp.abs(x)))


def _modulated_norm(x, nw, shift, scale):
    r = lax.rsqrt(jnp.mean(x * x, axis=-1, keepdims=True) + EPS)
    return (x * r) * nw * (1.0 + scale) + shift


def _adaln_kernel(c_ref, w_ref, b_ref, o_ref):
    a = _silu(c_ref[...]).astype(BF16)
    o_ref[0] = _dot(a, w_ref[0].astype(BF16)) + b_ref[0]


def _adaln(c_all, ada_w, ada_b):
    m, d = c_all.shape
    n = ada_w.shape[-1]
    tn = 1536
    return pl.pallas_call(
        _adaln_kernel,
        out_shape=jax.ShapeDtypeStruct((DEPTH, m, n), F32),
        grid=(DEPTH, n // tn),
        in_specs=[pl.BlockSpec((m, d), lambda l, j: (0, 0)),
                  pl.BlockSpec((1, d, tn), lambda l, j: (l, 0, j)),
                  pl.BlockSpec((1, 1, tn), lambda l, j: (l, 0, j))],
        out_specs=pl.BlockSpec((1, m, tn), lambda l, j: (l, 0, j)),
        compiler_params=_cparams(("parallel", "parallel")),
        name="adaln",
    )(c_all, ada_w, ada_b.reshape(DEPTH, 1, n))


def _modmm_kernel(x_ref, nw_ref, sh_ref, sc_ref, w_ref, o_ref, h_ref):
    @pl.when(pl.program_id(1) == 0)
    def _():
        h_ref[...] = _modulated_norm(x_ref[...], nw_ref[...], sh_ref[0], sc_ref[0]).astype(BF16)

    o_ref[...] = _dot(h_ref[...], w_ref[...])


MOD_SHIFT_MIX, MOD_SCALE_MIX, MOD_GATE_MIX, MOD_SHIFT_MLP, MOD_SCALE_MLP, MOD_GATE_MLP = range(6)


def _mod_rows(mod, tm, rows_per_mod):
    block = (1, mod.shape[1], D_MODEL)
    if rows_per_mod is None:
        return lambda k: pl.BlockSpec(block, lambda i, j: (0, 0, k))
    bpb = rows_per_mod // tm
    return lambda k: pl.BlockSpec(block, lambda i, j: (i // bpb, 0, k))


def _mod_matmul(x, nw, mod, w, *, tm, tn, rows_per_mod):
    m, d = x.shape
    n = w.shape[1]
    spec = _mod_rows(mod, tm, rows_per_mod)
    w_spec = (pl.BlockSpec((d, tn), lambda i, j: (0, 0), pipeline_mode=pl.Buffered(1)) if tn == n
              else pl.BlockSpec((d, tn), lambda i, j: (0, j)))
    return pl.pallas_call(
        _modmm_kernel,
        out_shape=jax.ShapeDtypeStruct((m, n), F32),
        grid=(m // tm, n // tn),
        in_specs=[pl.BlockSpec((tm, d), lambda i, j: (i, 0)),
                  pl.BlockSpec((1, d), lambda i, j: (0, 0)),
                  spec(MOD_SHIFT_MIX), spec(MOD_SCALE_MIX),
                  w_spec],
        out_specs=pl.BlockSpec((tm, tn), lambda i, j: (i, j)),
        scratch_shapes=[pltpu.VMEM((tm, d), BF16)],
        compiler_params=_cparams(("parallel", "arbitrary")),
        name="mod_matmul",
    )(x, nw, mod, mod, w)


def _post_mlp_kernel(x_ref, a_ref, wo_ref, g1_ref, nw_ref, sh_ref, sc_ref, g2_ref, w1_ref, w2_ref, fnw_ref, o_ref,
                     *, final_norm, tf):
    x1 = x_ref[...] + g1_ref[0] * _dot(a_ref[...], wo_ref[...])
    h = _modulated_norm(x1, nw_ref[...], sh_ref[0], sc_ref[0]).astype(BF16)
    acc = None
    for k in range(w1_ref.shape[1] // tf):
        u = jnp.maximum(_dot(h, w1_ref[:, k * tf:(k + 1) * tf]), 0.0)
        part = _dot((u * u).astype(BF16), w2_ref[k * tf:(k + 1) * tf, :])
        acc = part if acc is None else acc + part
    y = x1 + g2_ref[0] * acc
    if final_norm:
        r = lax.rsqrt(jnp.mean(y * y, axis=-1, keepdims=True) + EPS)
        y = (y * r) * fnw_ref[...]
    o_ref[...] = y


def _post_mlp(x, a, wo, mod, nw, w1, w2, fnw, *, tm, tf, rows_per_mod, final_norm):
    m, d = x.shape
    ka = a.shape[1]
    ff = w1.shape[1]
    spec = _mod_rows(mod, tm, rows_per_mod)
    resident = lambda shape: pl.BlockSpec(shape, lambda i, j: (0, 0), pipeline_mode=pl.Buffered(1))
    return pl.pallas_call(
        functools.partial(_post_mlp_kernel, final_norm=final_norm, tf=tf),
        out_shape=jax.ShapeDtypeStruct((m, d), F32),
        grid=(m // tm, 1),
        in_specs=[pl.BlockSpec((tm, d), lambda i, j: (i, 0)),
                  pl.BlockSpec((tm, ka), lambda i, j: (i, 0)),
                  resident((ka, d)),
                  spec(MOD_GATE_MIX),
                  pl.BlockSpec((1, d), lambda i, j: (0, 0)),
                  spec(MOD_SHIFT_MLP), spec(MOD_SCALE_MLP), spec(MOD_GATE_MLP),
                  resident((d, ff)), resident((ff, d)),
                  pl.BlockSpec((1, d), lambda i, j: (0, 0))],
        out_specs=pl.BlockSpec((tm, d), lambda i, j: (i, 0)),
        compiler_params=_cparams(("parallel", "arbitrary")),
        name="post_mlp",
    )(x, a, wo, mod, nw, mod, mod, mod, w1, w2, fnw)


def _head_expand_matrix():
    e = np.zeros((LANES, SSD_D_INNER), np.float32)
    for h in range(SSD_HEADS):
        e[h, h * SSD_P:(h + 1) * SSD_P] = 1.0
    return jnp.asarray(e, BF16)


def _ssd_prompt_kernel(z_ref, x_ref, bc_ref, dt_ref, cwx_ref, cbx_ref, cwb_ref, cbb_ref, dtb_ref, alog_ref,
                       dskip_ref, nw_ref, exp_ref, tri_ref,
                       y_ref, conv_ref, ssm_ref, xs_ref, bs_ref, st_ref):
    c = pl.program_id(1)
    L = SSD_CHUNK
    tail = SSD_CONV_W - 1

    @pl.when(c == 0)
    def _():
        xs_ref[0:8, :] = jnp.zeros((8, SSD_D_INNER), F32)
        bs_ref[0:8, :] = jnp.zeros((8, SSD_BC), F32)
        st_ref[...] = jnp.zeros_like(st_ref)

    xs_ref[8:8 + L, :] = x_ref[...]
    bs_ref[8:8 + L, :] = bc_ref[...]

    def conv(buf, w_ref, b_ref):
        acc = b_ref[...] + buf[8 - tail:8 - tail + L, :] * w_ref[0:1, :]
        for k in range(1, SSD_CONV_W):
            acc = acc + buf[8 - tail + k:8 - tail + k + L, :] * w_ref[k:k + 1, :]
        return _silu(acc)

    x = conv(xs_ref, cwx_ref, cbx_ref)
    bcv = conv(bs_ref, cwb_ref, cbb_ref)
    xs_ref[8 - tail:8, :] = xs_ref[8 + L - tail:8 + L, :]
    bs_ref[8 - tail:8, :] = bs_ref[8 + L - tail:8 + L, :]

    @pl.when(c == pl.num_programs(1) - 1)
    def _():
        conv_ref[0, :, 0:SSD_D_INNER] = x_ref[L - tail:L, :]
        conv_ref[0, :, SSD_D_INNER:SSD_CONV_DIM] = bc_ref[L - tail:L, :]

    lane = lax.broadcasted_iota(jnp.int32, (1, LANES), 1)
    head_ok = lane < SSD_HEADS
    dt = jnp.where(head_ok, _softplus(dt_ref[...] + dtb_ref[...]), 0.0)
    a = jnp.where(head_ok, -jnp.exp(alog_ref[...]) * LOG2E, 0.0)
    acum = _dot_01x(tri_ref[...], dt * a)
    acum_t = acum.T
    a_last = acum[L - 1:L, :]
    expand = exp_ref[...]
    dt_e = _dot_x01(dt, expand)
    eac_e = _dot_x01(jnp.exp2(acum), expand)
    dend_e = _dot_x01(jnp.exp2(a_last - acum), expand)
    cdec_e = eac_e[L - 1:L, :]

    xdt = x * dt_e
    xdt_b = xdt.astype(BF16)
    xdtw_b = (xdt * dend_e).astype(BF16)
    row = lax.broadcasted_iota(jnp.int32, (L, L), 0)
    col = lax.broadcasted_iota(jnp.int32, (L, L), 1)
    causal = row >= col
    lane2 = lax.broadcasted_iota(jnp.int32, (L, 2 * SSD_P), 1)
    first_head = lane2 < SSD_P

    y_parts = []
    for g in range(SSD_GROUPS):
        b_g = bcv[:, g * SSD_N:(g + 1) * SSD_N]
        c_g = bcv[:, (SSD_GROUPS + g) * SSD_N:(SSD_GROUPS + g + 1) * SSD_N]
        b_gb = b_g.astype(BF16)
        c_gb = c_g.astype(BF16)
        cb = _dot_nt(c_gb, b_gb)
        gl = slice(g * SSD_REP * SSD_P, (g + 1) * SSD_REP * SSD_P)
        st_g = st_ref[:, gl]
        y_off = _dot(c_gb, st_g.astype(BF16)) * eac_e[:, gl]
        pair_out = []
        for j in range(SSD_REP // 2):
            pl_ = slice(g * SSD_REP * SSD_P + j * 2 * SSD_P, g * SSD_REP * SSD_P + (j + 1) * 2 * SSD_P)
            xp = xdt_b[:, pl_]
            ys = []
            for hh in range(2):
                h = g * SSD_REP + 2 * j + hh
                seg = acum[:, h:h + 1] - acum_t[h:h + 1, :]
                dec = jnp.exp2(jnp.where(causal, seg, NEG))
                ys.append(_dot((cb * dec).astype(BF16), xp))
            pair_out.append(jnp.where(first_head, ys[0], ys[1]))
        y_g = jnp.concatenate(pair_out, axis=1) + y_off
        y_parts.append(y_g)
        st_ref[:, gl] = st_g * cdec_e[:, gl] + _dot(b_g.T.astype(BF16), xdtw_b[:, gl])

    y = jnp.concatenate(y_parts, axis=1) + dskip_ref[...] * x
    zg = z_ref[...]
    y = y * _silu(zg)
    r = lax.rsqrt(jnp.mean(y * y, axis=-1, keepdims=True) + EPS)
    y_ref[...] = ((y * r) * nw_ref[...]).astype(BF16)

    @pl.when(c == pl.num_programs(1) - 1)
    def _():
        ssm_ref[0] = st_ref[...].T


def _ssd_prompt(proj, bsz, t, cw, cb, dtb, alog, dskip, nw):
    L = SSD_CHUNK
    nc = t // L
    di = SSD_D_INNER
    tri = jnp.asarray(np.tril(np.ones((L, L), np.float32)), BF16)
    full = lambda shape: pl.BlockSpec(shape, lambda b, c: (0,) * len(shape))
    return pl.pallas_call(
        _ssd_prompt_kernel,
        out_shape=(jax.ShapeDtypeStruct((bsz * t, di), BF16),
                   jax.ShapeDtypeStruct((bsz, SSD_CONV_W - 1, SSD_CONV_DIM), F32),
                   jax.ShapeDtypeStruct((bsz, di, SSD_N), F32)),
        grid=(bsz, nc),
        in_specs=[pl.BlockSpec((L, di), lambda b, c: (b * nc + c, 0)),
                  pl.BlockSpec((L, di), lambda b, c: (b * nc + c, 1)),
                  pl.BlockSpec((L, SSD_BC), lambda b, c: (b * nc + c, 2 * di // SSD_BC)),
                  pl.BlockSpec((L, LANES), lambda b, c: (b * nc + c, (2 * di + SSD_BC) // LANES)),
                  full((SSD_CONV_W, di)), full((1, di)), full((SSD_CONV_W, SSD_BC)), full((1, SSD_BC)),
                  full((1, LANES)), full((1, LANES)), full((1, di)), full((1, di)),
                  full((LANES, di)), full((L, L))],
        out_specs=(pl.BlockSpec((L, di), lambda b, c: (b * nc + c, 0)),
                   pl.BlockSpec((1, SSD_CONV_W - 1, SSD_CONV_DIM), lambda b, c: (b, 0, 0)),
                   pl.BlockSpec((1, di, SSD_N), lambda b, c: (b, 0, 0))),
        scratch_shapes=[pltpu.VMEM((8 + L, di), F32), pltpu.VMEM((8 + L, SSD_BC), F32),
                        pltpu.VMEM((SSD_N, di), F32)],
        compiler_params=_cparams(("parallel", "arbitrary")),
        name="ssd_prompt",
    )(proj, proj, proj, proj, cw[:, :di], cb[:, :di], cw[:, di:], cb[:, di:], dtb, alog, dskip, nw,
      _head_expand_matrix(), tri)


def _ssd_sample_kernel(p_ref, cs_ref, st_ref, cw_ref, cb_ref, dtb_ref, alog_ref, dskip_ref, nw_ref, exp_ref, *rest):
    y_ref, conv_ref, ssm_ref = rest[-3:]
    di = SSD_D_INNER
    z = p_ref[0, :, 0:di]
    xbc = p_ref[0, :, di:di + SSD_CONV_DIM]
    dtr = p_ref[0, :, di + SSD_CONV_DIM:di + SSD_CONV_DIM + LANES]
    cs = cs_ref[0]
    acc = cb_ref[...] + xbc * cw_ref[SSD_CONV_W - 1:SSD_CONV_W, :]
    for k in range(SSD_CONV_W - 1):
        acc = acc + cs[k:k + 1, :] * cw_ref[k:k + 1, :]
    conv_ref[0, 0:SSD_CONV_W - 2, :] = cs[1:SSD_CONV_W - 1, :]
    conv_ref[0, SSD_CONV_W - 2:SSD_CONV_W - 1, :] = xbc
    act = _silu(acc)
    x = act[:, 0:di]
    lane = lax.broadcasted_iota(jnp.int32, (1, LANES), 1)
    head_ok = lane < SSD_HEADS
    dt = jnp.where(head_ok, _softplus(dtr + dtb_ref[...]), 0.0)
    a = jnp.where(head_ok, -jnp.exp(alog_ref[...]), 0.0)
    da = jnp.exp(dt * a)
    lhs = jnp.concatenate([dt, da, jnp.zeros((6, LANES), F32)], axis=0)
    ex = _dot_x01(lhs, exp_ref[...])
    dt_e, da_e = ex[0:1, :], ex[1:2, :]
    xdt = x * dt_e

    eye = (lax.broadcasted_iota(jnp.int32, (LANES, LANES), 0)
           == lax.broadcasted_iota(jnp.int32, (LANES, LANES), 1))

    def to_col(rowvec):
        return jnp.sum(jnp.where(eye, jnp.broadcast_to(rowvec, (LANES, LANES)), 0.0), axis=1, keepdims=True)

    def to_row(colvec):
        return jnp.sum(jnp.where(eye, jnp.broadcast_to(colvec, (LANES, LANES)), 0.0), axis=0, keepdims=True)

    y_off = []
    cbs = []
    for g in range(SSD_GROUPS):
        b_g = act[:, di + g * SSD_N:di + (g + 1) * SSD_N]
        c_g = act[:, di + (SSD_GROUPS + g) * SSD_N:di + (SSD_GROUPS + g + 1) * SSD_N]
        cbs.append(jnp.broadcast_to(jnp.sum(b_g * c_g, axis=1, keepdims=True), (1, SSD_REP * SSD_P)))
        for i in range(SSD_REP * SSD_P // LANES):
            lo = g * SSD_REP * SSD_P + i * LANES
            st = st_ref[0, 0, lo:lo + LANES, :]
            xcol = to_col(xdt[:, lo:lo + LANES])
            dcol = to_col(da_e[:, lo:lo + LANES])
            ssm_ref[0, 0, lo:lo + LANES, :] = st * dcol + xcol * b_g
            y_off.append(to_row(jnp.sum(st * c_g, axis=1, keepdims=True)))
    y = xdt * jnp.concatenate(cbs, axis=1) + jnp.concatenate(y_off, axis=1) * da_e + dskip_ref[...] * x
    y = y * _silu(z)
    r = lax.rsqrt(jnp.mean(y * y, axis=-1, keepdims=True) + EPS)
    y_ref[0] = ((y * r) * nw_ref[...]).astype(BF16)


def _ssd_sample(proj, conv_state, ssm_states, layer, ssm_out_prev, cw, cb, dtb, alog, dskip, nw):
    n_layers, bsz = ssm_states.shape[:2]
    di = SSD_D_INNER
    full = lambda shape: pl.BlockSpec(shape, lambda b: (0,) * len(shape))
    per_b = lambda shape: pl.BlockSpec((1,) + shape, lambda b: (b,) + (0,) * len(shape))
    state_spec = pl.BlockSpec((1, 1, di, SSD_N), lambda b: (layer, b, 0, 0))
    in_specs = [per_b((1, SSD_NP)), per_b((SSD_CONV_W - 1, SSD_CONV_DIM)), state_spec,
                full((SSD_CONV_W, SSD_CONV_DIM)), full((1, SSD_CONV_DIM)), full((1, LANES)), full((1, LANES)),
                full((1, di)), full((1, di)), full((LANES, di))]
    args = [proj.reshape(bsz, 1, SSD_NP), conv_state, ssm_states.reshape(n_layers, bsz, di, SSD_N), cw, cb, dtb, alog,
            dskip, nw, _head_expand_matrix()]
    aliases = {}
    if ssm_out_prev is not None:
        in_specs.append(pl.BlockSpec(memory_space=pl.ANY))
        aliases = {len(args): 2}
        args.append(ssm_out_prev)
    y, conv, ssm = pl.pallas_call(
        _ssd_sample_kernel,
        out_shape=(jax.ShapeDtypeStruct((bsz, 1, di), BF16),
                   jax.ShapeDtypeStruct((bsz, SSD_CONV_W - 1, SSD_CONV_DIM), F32),
                   jax.ShapeDtypeStruct((n_layers, bsz, di, SSD_N), F32)),
        grid=(bsz,),
        in_specs=in_specs,
        out_specs=(per_b((1, di)), per_b((SSD_CONV_W - 1, SSD_CONV_DIM)), state_spec),
        input_output_aliases=aliases,
        compiler_params=_cparams(("parallel",)),
        name="ssd_sample",
    )(*args)
    return y.reshape(bsz, di), conv, ssm


CH = CMP_STRIDE
N_L = CMP_BLOCK // 2
PE_ROWS = 16


def _compress_accumulate(lhs_fn, pe_ref, wab_ref, kv):
    acc = None
    for l in range(N_L):
        lhs = jnp.concatenate([lhs_fn(l, kv).astype(BF16), pe_ref[kv, l]], axis=0)
        t = _dot(lhs, wab_ref[kv, l])
        acc = t if acc is None else acc + t
    return acc


def _compress_finish(acc, w2, nch):
    p = acc[0:nch, 0:GW]
    q_next = pltpu.roll(acc[0:nch, GW:2 * GW], nch - 1, axis=0)
    bias = acc[nch:nch + 1, 0:GW] + acc[nch + 8:nch + 9, GW:2 * GW]
    out = _dot(_silu(p + q_next + bias).astype(BF16), w2)
    row = lax.broadcasted_iota(jnp.int32, (nch, 1), 0)
    return jnp.where(row < nch - 1, out, 0.0)


def _softmax_masked(s, ok, axis, exp=jnp.exp):
    s = jnp.where(ok, s, NEG)
    m = jnp.max(s, axis=axis, keepdims=True)
    p = jnp.where(ok, exp(s - m), 0.0)
    d = jnp.sum(p, axis=axis, keepdims=True)
    return p * jnp.where(d > 0.0, 1.0 / d, 0.0)


def _interleave(n, scores, softmax, values, finish):
    out, pending = [], None
    s_next = scores(0)
    for k in range(n):
        s = s_next
        if k + 1 < n:
            s_next = scores(k + 1)
        p, aux = softmax(k, s)
        if pending is not None:
            out.append(finish(*pending))
        pending = (k, values(k, p), aux)
    out.append(finish(*pending))
    return out


def _topk_mask(score, j, width, shifts, axis):
    n = score.shape[axis]
    rank = jnp.zeros(score.shape, F32)
    for k in shifts:
        other = pltpu.roll(score, k, axis=axis)
        lower = j >= k
        if width != n:
            other = jnp.where(lower, other, pltpu.roll(score, n - width + k, axis=axis))
        rank = rank + jnp.where(lower, jnp.where(other >= score, 1.0, 0.0), jnp.where(other > score, 1.0, 0.0))
    return rank < SEL_TOPK


def _overlap(n_cmp, n_sel):
    c_start = np.arange(n_cmp) * CMP_STRIDE
    s_start = np.arange(n_sel) * SEL_BLOCK
    return ((c_start[:, None] < s_start[None, :] + SEL_BLOCK)
            & (c_start[:, None] + CMP_BLOCK > s_start[None, :])).astype(np.float32)


def _block_bias_matrix(width, n_keys):
    l = np.arange(LANES)[:, None] % width
    blk = np.arange(n_keys)[None, :] // SEL_BLOCK
    return np.where(l == blk, NEG_MXU, 0.0).astype(np.float32)


def _nsa_proj_kernel(x_ref, nw_ref, sh_ref, sc_ref, wr_ref, wt_ref, *rest):
    rows_ref, kpad_ref, qt_ref, gt_ref, vt_ref, cmp_ref, sel_ref, win_ref = rest[-8:]
    h = _modulated_norm(x_ref[...], nw_ref[...], sh_ref[0], sc_ref[0]).astype(BF16)
    y = _dot(h, wr_ref[...])
    rows_ref[...] = y[:, 0:KVW].astype(BF16)
    kpad_ref[...] = y[:, KVW:].astype(BF16)
    yt = _dot_nt(wt_ref[...], h)
    qt_ref[0] = (yt[0:1024] * (HD ** -0.5 * LOG2E)).astype(BF16)
    cmp_ref[0, 0] = yt[1024:1024 + KVW]
    sel_ref[0, 0] = yt[1024 + KVW:1024 + 2 * KVW]
    vt_ref[0, 0:GW] = yt[1024 + KVW + GW:1024 + 2 * KVW].astype(BF16)
    vt_ref[0, GW:2 * GW] = yt[1024 + 2 * KVW + GW:1024 + 3 * KVW].astype(BF16)
    gt_ref[0] = _sigmoid(yt[1024 + 3 * KVW:NSA_N])

    @pl.when(pl.program_id(1) == pl.num_programs(1) - 1)
    def _():
        win_ref[0, 0] = yt[1024 + 2 * KVW:1024 + 3 * KVW]


def _nsa_proj(x, nw, mod, wr, wt, bsz, t, layer, n_layers, kv_prev):
    d = x.shape[1]
    tm = min(WINDOW, t)
    nb = t // tm
    row = lambda w: pl.BlockSpec((tm, w), lambda b, i: (b * nb + i, 0))
    col = lambda h: pl.BlockSpec((1, h, tm), lambda b, i: (b, 0, i))
    kv_col = pl.BlockSpec((1, 1, KVW, tm), lambda b, i: (layer, b, 0, i))
    mod_col = lambda k: pl.BlockSpec((1, 1, d), lambda b, i: (b, 0, k))
    nq = NSA_HEADS * HD
    in_specs = [row(d), pl.BlockSpec((1, d), lambda b, i: (0, 0)), mod_col(MOD_SHIFT_MIX), mod_col(MOD_SCALE_MIX),
                pl.BlockSpec((d, KVW + 2 * KPAD), lambda b, i: (0, 0)),
                pl.BlockSpec((NSA_N, d), lambda b, i: (0, 0))]
    args = [x, nw, mod, mod, wr, wt]
    aliases = {}
    if kv_prev is not None:
        aliases = {len(args) + k: 5 + k for k in range(3)}
        in_specs += [pl.BlockSpec(memory_space=pl.ANY)] * 3
        args += list(kv_prev)
    return pl.pallas_call(
        _nsa_proj_kernel,
        out_shape=(jax.ShapeDtypeStruct((bsz * t, KVW), BF16),
                   jax.ShapeDtypeStruct((bsz * t, 2 * KPAD), BF16),
                   jax.ShapeDtypeStruct((bsz, nq, t), BF16),
                   jax.ShapeDtypeStruct((bsz, LANES, t), F32),
                   jax.ShapeDtypeStruct((bsz, 2 * GW, t), BF16),
                   jax.ShapeDtypeStruct((n_layers, bsz, KVW, t), F32),
                   jax.ShapeDtypeStruct((n_layers, bsz, KVW, t), F32),
                   jax.ShapeDtypeStruct((n_layers, bsz, KVW, tm), F32)),
        grid=(bsz, nb),
        in_specs=in_specs,
        out_specs=(row(KVW), row(2 * KPAD), col(nq), col(LANES), col(2 * GW), kv_col, kv_col,
                   pl.BlockSpec((1, 1, KVW, tm), lambda b, i: (layer, b, 0, 0))),
        input_output_aliases=aliases,
        compiler_params=_cparams(("parallel", "arbitrary")),
        name="nsa_proj",
    )(*args)


def _cmp_prompt_kernel(rows_ref, perm_ref, pe_ref, wab_ref, w2k_ref, w2v_ref, kc_ref, vc_ref, xr_ref):
    nch = rows_ref.shape[1] // CH
    cpp = PAGE_SIZE // CH
    perm = perm_ref[...]
    for p in range(rows_ref.shape[1] // PAGE_SIZE):
        rows = _dot(perm, rows_ref[0, p * PAGE_SIZE:(p + 1) * PAGE_SIZE, :])
        for l in range(N_L):
            xr_ref[l, p * cpp:(p + 1) * cpp, :] = rows[l * cpp:(l + 1) * cpp, :]
    lhs = lambda l, kv: xr_ref[l, :, kv * GW:(kv + 1) * GW]
    kc_ref[0] = _compress_finish(_compress_accumulate(lhs, pe_ref, wab_ref, 0), w2k_ref[...], nch).astype(BF16)
    vc_ref[0] = _compress_finish(_compress_accumulate(lhs, pe_ref, wab_ref, 1), w2v_ref[...], nch).T.astype(BF16)


def _chunk_perm():
    cpp = PAGE_SIZE // CH
    perm = np.zeros((PAGE_SIZE, PAGE_SIZE), np.float32)
    for l in range(N_L):
        for c in range(cpp):
            perm[l * cpp + c, c * CH + l] = 1.0
    return jnp.asarray(perm, BF16)


def _cmp_prompt(rows, pe_t, wab, w2k_pad, w2v):
    bsz, t, _ = rows.shape
    nch = t // CH
    full = lambda shape: pl.BlockSpec(shape, lambda b: (0,) * len(shape))
    return pl.pallas_call(
        _cmp_prompt_kernel,
        out_shape=(jax.ShapeDtypeStruct((bsz, nch, KPAD), BF16), jax.ShapeDtypeStruct((bsz, GW, nch), BF16)),
        grid=(bsz,),
        in_specs=[pl.BlockSpec((1, t, KVW), lambda b: (b, 0, 0)), full((PAGE_SIZE, PAGE_SIZE)),
                  full(pe_t.shape), full(wab.shape), full(w2k_pad.shape), full(w2v.shape)],
        out_specs=(pl.BlockSpec((1, nch, KPAD), lambda b: (b, 0, 0)), pl.BlockSpec((1, GW, nch), lambda b: (b, 0, 0))),
        scratch_shapes=[pltpu.VMEM((N_L, nch, KVW), F32)],
        compiler_params=_cparams(("parallel",)),
        name="nsa_compress_prompt",
    )(rows, _chunk_perm(), pe_t, wab, w2k_pad, w2v)


def _nsa_prompt_kernel(qt_ref, gt_ref, kc_ref, vct_ref, kpad_ref, vt_ref, ovt_ref, nege_ref, o_ref,
                       *, tq, n_sel, ck):
    i = pl.program_id(1)
    t0 = i * tq
    nq = REP * tq
    tl = t0 + lax.broadcasted_iota(jnp.int32, (1, tq), 1)
    zq = jnp.zeros((HD, nq), BF16)

    def lanes4(a):
        return jnp.concatenate([a] * REP, axis=1)

    def with_ones(vt):
        return jnp.concatenate([vt, jnp.ones((ONES_ROWS, vt.shape[1]), BF16)], axis=0)

    def normalized(acc):
        return acc[0:HD] * (1.0 / acc[HD:HD + 1])

    tl4 = lanes4(tl)
    qg = [jnp.concatenate([qt_ref[0, (g * REP + r) * HD:(g * REP + r + 1) * HD, :] for r in range(REP)], axis=1)
          for g in range(KVH)]
    q_rhs = [jnp.concatenate([q, zq], axis=0) for q in qg]

    kw = WINDOW + tq
    k_start = pl.multiple_of(jnp.maximum(t0 - WINDOW, 0), LANES)
    dpos = tl - (k_start + lax.broadcasted_iota(jnp.int32, (kw, 1), 0))
    wbias = lanes4(jnp.where((dpos >= 0) & (dpos <= WINDOW), 0.0, NEG))

    def win_scores(g):
        return _dot(kpad_ref[0, pl.ds(k_start, kw), (KVH + g) * LANES:(KVH + g + 1) * LANES], q_rhs[g]) + wbias

    def win_softmax(g, s):
        return jnp.exp2(s - jnp.max(s, axis=0, keepdims=True)).astype(BF16), None

    def win_values(g, p):
        return _dot(with_ones(vt_ref[0, GW + g * HD:GW + (g + 1) * HD, pl.ds(k_start, kw)]), p)

    o_win = _interleave(KVH, win_scores, win_softmax, win_values, lambda g, pv, aux: normalized(pv))

    ncp = kc_ref.shape[1]
    cend = lax.broadcasted_iota(jnp.int32, (ncp, 1), 0) * CMP_STRIDE + (CMP_BLOCK - 1)
    ok_cmp = cend <= tl4
    s_cmp = [_dot(kc_ref[0, :, g * LANES:(g + 1) * LANES], q_rhs[g]) for g in range(KVH)]
    p_cmp = [_softmax_masked(s, ok_cmp, 0, jnp.exp2) for s in s_cmp]
    o_cmp = [_dot(vct_ref[0, g * HD:(g + 1) * HD, :], p_cmp[g].astype(BF16)) for g in range(KVH)]
    imp = jnp.zeros((LANES, tq), F32)
    for g, p in enumerate(p_cmp):
        psum = p[:, 0:tq] + p[:, tq:2 * tq] + p[:, 2 * tq:3 * tq] + p[:, 3 * tq:4 * tq]
        imp = imp + _dot_01x(ovt_ref[g], psum)

    row = lax.broadcasted_iota(jnp.int32, (LANES, tq), 0)
    jj = row % n_sel
    cur = tl // SEL_BLOCK
    valid = jj * SEL_BLOCK <= tl
    forced = (jj == 0) | (jj == cur) | (jj == cur - 1)
    score = jnp.where(valid, jnp.where(forced, BIG, imp), NEG)
    top = _topk_mask(score, jj, n_sel, range(1, n_sel), 0)
    blocked = jnp.where(top & valid, 0.0, 1.0)

    n_full = t0 // ck
    sel_rhs = [jnp.concatenate([lanes4(jnp.where(row // n_sel == g, blocked, 0.0).astype(BF16)), q_rhs[g]], axis=0)
               for g in range(KVH)]

    def sel_chunk(c, carries, causal):
        k0 = pl.multiple_of(c * ck, ck)
        bias_rows = nege_ref[pl.ds(k0, ck), :]
        if causal:
            cbias = jnp.where(k0 + lax.broadcasted_iota(jnp.int32, (ck, 1), 0) <= tl4, 0.0, NEG)

        def scores(g):
            lhs = jnp.concatenate([bias_rows, kpad_ref[0, pl.ds(k0, ck), g * LANES:(g + 1) * LANES]], axis=1)
            s = _dot(lhs, sel_rhs[g])
            return s + cbias if causal else s

        def softmax(g, s):
            m, acc = carries[g]
            m_new = jnp.maximum(m, jnp.max(s, axis=0, keepdims=True))
            return jnp.exp2(s - m_new).astype(BF16), (m_new, jnp.exp2(m - m_new), acc)

        def values(g, p):
            return _dot(with_ones(vt_ref[0, g * HD:(g + 1) * HD, pl.ds(k0, ck)]), p)

        def finish(g, pv, aux):
            m_new, alpha, acc = aux
            return m_new, alpha * acc + pv

        return tuple(_interleave(KVH, scores, softmax, values, finish))

    init = (jnp.full((1, nq), NEG, F32), jnp.zeros((HD + ONES_ROWS, nq), F32))
    carries = lax.fori_loop(0, n_full, lambda c, cr: sel_chunk(c, cr, False), (init,) * KVH)
    o_sel = [normalized(acc) for _, acc in sel_chunk(n_full, carries, True)]

    gt = gt_ref[0]
    outs = []
    for g in range(KVH):
        for r in range(REP):
            h = g * REP + r
            sl = slice(r * tq, (r + 1) * tq)
            outs.append(gt[3 * h:3 * h + 1, :] * o_cmp[g][:, sl] + gt[3 * h + 1:3 * h + 2, :] * o_sel[g][:, sl]
                        + gt[3 * h + 2:3 * h + 3, :] * o_win[g][:, sl])
    o_ref[0] = jnp.concatenate(outs, axis=0).T.astype(BF16)


def _nsa_prompt_attention(qt, gt, kc, vct, kpad, vt):
    bsz, nd, t = qt.shape
    tq = LANES
    ck = 4 * LANES
    n_sel = t // SEL_BLOCK
    ncp = t // CH
    assert KVH * n_sel == LANES and ncp == LANES and t % ck == 0
    ovt = np.zeros((KVH, LANES, ncp), np.float32)
    for g in range(KVH):
        ovt[g, g * n_sel:(g + 1) * n_sel, :ncp - 1] = _overlap(ncp - 1, n_sel).T
    nege_t = _block_bias_matrix(n_sel, t).T
    per_b = lambda shape: pl.BlockSpec((1,) + shape, lambda b, i: (b,) + (0,) * len(shape))
    full = lambda shape: pl.BlockSpec(shape, lambda b, i: (0,) * len(shape))
    out = pl.pallas_call(
        functools.partial(_nsa_prompt_kernel, tq=tq, n_sel=n_sel, ck=ck),
        out_shape=jax.ShapeDtypeStruct((bsz, t, nd), BF16),
        grid=(bsz, t // tq),
        in_specs=[pl.BlockSpec((1, nd, tq), lambda b, i: (b, 0, i)),
                  pl.BlockSpec((1, LANES, tq), lambda b, i: (b, 0, i)),
                  per_b((ncp, KPAD)), per_b((GW, ncp)), per_b((t, 2 * KPAD)), per_b((2 * GW, t)),
                  full((KVH, LANES, ncp)), full((t, LANES))],
        out_specs=pl.BlockSpec((1, tq, nd), lambda b, i: (b, i, 0)),
        compiler_params=_cparams(("parallel", "arbitrary")),
        name="nsa_prompt_attention",
    )(qt, gt, kc, vct, kpad.reshape(bsz, t, 2 * KPAD), vt, jnp.asarray(ovt, BF16), jnp.asarray(nege_t, BF16))
    return out.reshape(bsz * t, nd)


def _nsa_sample_kernel(pt_ref, p_ref, *rest, n_pages, past):
    del pt_ref
    cmp_pages = rest[:n_pages]
    sel_pages = rest[n_pages:2 * n_pages]
    win_ref, pe_ref, wab_ref, w2_ref, ov_ref, nege_ref, perm_ref = rest[2 * n_pages:2 * n_pages + 7]
    o_ref, wout_ref, xr_ref = rest[-3:]
    nch = n_pages * (PAGE_SIZE // CH)
    cpp = PAGE_SIZE // CH
    t = past
    keep = win_ref.shape[-1]
    rows_q = 8

    def compress_cache():
        perm = perm_ref[...]
        for p, pg in enumerate(cmp_pages):
            for kv in range(2):
                rows = _dot_nt(perm, pg[0, 0, kv].reshape(GW, PAGE_SIZE).astype(BF16))
                for l in range(N_L):
                    xr_ref[kv, l, p * cpp:(p + 1) * cpp, :] = rows[l * cpp:(l + 1) * cpp, :]
        out = []
        for kv in range(2):
            acc = _compress_accumulate(lambda l, kv: xr_ref[kv, l], pe_ref, wab_ref, kv)
            out.append(_compress_finish(acc, w2_ref[kv], nch).T.astype(BF16))
        return out

    lane = lax.broadcasted_iota(jnp.int32, (rows_q, LANES), 1)
    rowi = lax.broadcasted_iota(jnp.int32, (rows_q, LANES), 0)
    cend = lax.broadcasted_iota(jnp.int32, (1, nch), 1) * CMP_STRIDE + (CMP_BLOCK - 1)
    n_sel = -(-(t + 1) // SEL_BLOCK)
    nj = max(n_sel, SEL_TOPK)
    cur = t // SEL_BLOCK
    valid = (lane < n_sel) & (lane * SEL_BLOCK <= t)
    forced = (lane == 0) | (lane == cur) | (lane == cur - 1)
    shifts = list(range(1, nj)) + list(range(LANES - nj + 1, LANES))
    gates = jnp.broadcast_to(_sigmoid(p_ref[0, :, 1024 + 3 * KVW:NSA_N]), (rows_q, LANES))
    eye = (lax.broadcasted_iota(jnp.int32, (HD, HD), 0) == lax.broadcasted_iota(jnp.int32, (HD, HD), 1))
    last_lane = lax.broadcasted_iota(jnp.int32, (HD, keep), 1) == keep - 1

    G = range(KVH)
    new_row = lambda off, g: p_ref[0, :, off + g * HD:off + (g + 1) * HD]
    q = [jnp.concatenate([p_ref[0, :, (g * REP + r) * HD:(g * REP + r + 1) * HD] for r in range(REP)]
                         + [jnp.zeros((rows_q - REP, HD), F32)], axis=0) * (HD ** -0.5) for g in G]
    qb = [x.astype(BF16) for x in q]

    def softmax_with_new(s, s_new, ok_new):
        s_new = jnp.where(ok_new, s_new, NEG)
        m = jnp.maximum(jnp.max(s, axis=1, keepdims=True), s_new)
        pr = jnp.exp(s - m)
        pr_new = jnp.where(ok_new, jnp.exp(s_new - m), 0.0)
        return pr.astype(BF16), pr_new, 1.0 / (jnp.sum(pr, axis=1, keepdims=True) + pr_new)

    kw = [win_ref[0, 0, 0, g] for g in G]
    vw = [win_ref[0, 0, 1, g] for g in G]
    kw_new = [new_row(1024 + 2 * KVW, g) for g in G]
    vw_new = [new_row(1024 + 2 * KVW + GW, g) for g in G]
    s_win = [_dot(qb[g], kw[g].astype(BF16)) for g in G]
    sm = [softmax_with_new(s_win[g], jnp.sum(q[g] * kw_new[g], axis=1, keepdims=True), jnp.full((rows_q, 1), True))
          for g in G]
    pv = [_dot_nt(sm[g][0], vw[g].astype(BF16)) for g in G]
    o_win = [(pv[g] + sm[g][1] * vw_new[g]) * sm[g][2] for g in G]
    for g in G:
        for kv, old, new in ((0, kw[g], kw_new[g]), (1, vw[g], vw_new[g])):
            col = jnp.sum(jnp.where(eye, jnp.broadcast_to(new, (HD, HD)), 0.0), axis=1, keepdims=True)
            wout_ref[0, 0, kv, g] = jnp.where(last_lane, col, pltpu.roll(old, keep - 1, axis=1))

    kt = [jnp.concatenate([pg[0, 0, 0, g] for pg in sel_pages], axis=1).astype(BF16) for g in G]
    vt = [jnp.concatenate([pg[0, 0, 1, g] for pg in sel_pages], axis=1).astype(BF16) for g in G]

    kct, vct = compress_cache()
    s_cmp = [_dot(qb[g], kct[g * HD:(g + 1) * HD, :]) for g in G]
    p_cmp = [_softmax_masked(s, cend <= t, 1) for s in s_cmp]
    o_cmp = [_dot_nt(p_cmp[g].astype(BF16), vct[g * HD:(g + 1) * HD, :]) for g in G]
    allowed = []
    for g in G:
        psum = jnp.broadcast_to(jnp.sum(p_cmp[g][0:REP], axis=0, keepdims=True), (rows_q, nch))
        imp = _dot_x01(psum, ov_ref[...])
        score = jnp.where(valid, jnp.where(forced, BIG, imp), NEG)
        score = jnp.where(lane < nj, score, -3e38)
        allowed.append(_topk_mask(score, lane, LANES, shifts, 1) & valid)

    nege = nege_ref[...]
    s_sel = [_dot(jnp.concatenate([jnp.where(allowed[g], 0.0, 1.0).astype(BF16), qb[g]], axis=1),
                  jnp.concatenate([nege, kt[g]], axis=0)) for g in G]
    v_new = [new_row(1024 + KVW + GW, g) for g in G]
    sm = [softmax_with_new(s_sel[g], jnp.sum(q[g] * new_row(1024 + KVW, g), axis=1, keepdims=True),
                           jnp.sum(jnp.where((lane == cur) & allowed[g], 1.0, 0.0), axis=1, keepdims=True) > 0.5)
          for g in G]
    pv = [_dot_nt(sm[g][0], vt[g]) for g in G]
    o_sel = [(pv[g] + sm[g][1] * v_new[g]) * sm[g][2] for g in G]

    for g in G:
        o = None
        for br, ob in enumerate((o_cmp[g], o_sel[g], o_win[g])):
            gcol = jnp.sum(jnp.where(lane == (g * REP + rowi) * 3 + br, gates, 0.0), axis=1, keepdims=True)
            o = gcol * ob if o is None else o + gcol * ob
        for r in range(REP):
            h = g * REP + r
            o_ref[0, :, h * HD:(h + 1) * HD] = o[r:r + 1, :].astype(BF16)


def _nsa_sample(proj, cmp_view, sel_view, win_view, layer, win_out_prev, page_table, pe_t, wab, w2bd):
    n_layers = win_view.shape[0]
    bsz, n_pages = page_table.shape
    past = n_pages * PAGE_SIZE
    keep = win_view.shape[-1]
    nch = past // CH
    assert nch == LANES and keep <= WINDOW and past % SEL_BLOCK == 0 and past - keep >= 0
    n_sel = -(-(past + 1) // SEL_BLOCK)
    ov = np.zeros((nch, LANES), np.float32)
    ov[:nch - 1, :n_sel] = _overlap(nch - 1, n_sel)
    full = lambda shape: pl.BlockSpec(shape, lambda b, pt: (0,) * len(shape))
    once = lambda shape: pl.BlockSpec(shape, lambda b, pt: (0,) * len(shape), pipeline_mode=pl.Buffered(1))
    per_b = lambda shape: pl.BlockSpec((1,) + shape, lambda b, pt: (b,) + (0,) * len(shape))
    page_shape = (2, KVH, HD, PAGE_SIZE)
    page = lambda p: pl.BlockSpec((1, 1) + page_shape, lambda b, pt: (layer, pt[b * n_pages + p], 0, 0, 0, 0))
    win_shape = (2, KVH, HD, keep)
    win_spec = pl.BlockSpec((1, 1) + win_shape, lambda b, pt: (layer, b, 0, 0, 0, 0))
    in_specs = ([per_b((1, NSA_N))] + [page(p) for p in range(n_pages)] * 2
                + [win_spec, full(pe_t.shape), once(wab.shape), full(w2bd.shape), full((nch, LANES)),
                   full((LANES, past)), full((PAGE_SIZE, PAGE_SIZE))])
    args = [page_table.reshape(-1), proj.reshape(bsz, 1, NSA_N)] + [cmp_view] * n_pages + [sel_view] * n_pages + [
        win_view, pe_t, wab, w2bd, jnp.asarray(ov, BF16), jnp.asarray(_block_bias_matrix(LANES, past), BF16),
        _chunk_perm()]
    aliases = {}
    if win_out_prev is not None:
        in_specs.append(pl.BlockSpec(memory_space=pl.ANY))
        aliases = {len(args): 1}
        args.append(win_out_prev)
    o, wout = pl.pallas_call(
        functools.partial(_nsa_sample_kernel, n_pages=n_pages, past=past),
        out_shape=(jax.ShapeDtypeStruct((bsz, 1, NSA_HEADS * HD), BF16),
                   jax.ShapeDtypeStruct((n_layers, bsz) + win_shape, F32)),
        grid_spec=pltpu.PrefetchScalarGridSpec(
            num_scalar_prefetch=1, grid=(bsz,), in_specs=in_specs,
            out_specs=(per_b((1, NSA_HEADS * HD)), win_spec),
            scratch_shapes=[pltpu.VMEM((2, N_L, nch, GW), F32)]),
        input_output_aliases=aliases,
        compiler_params=_cparams(("arbitrary",)),
        name="nsa_sample",
    )(*args)
    return o.reshape(bsz, NSA_HEADS * HD), wout


def _prep_nsa(w_in, w_out, pe, w1, w2):
    n_gate = 3 * NSA_HEADS
    w_full = jnp.concatenate([w_in, jnp.zeros((D_MODEL, LANES - n_gate), F32)], axis=1)

    def k_padded(lo):
        k = w_in[:, lo:lo + GW].reshape(D_MODEL, KVH, HD)
        return jnp.concatenate([k, jnp.zeros_like(k)], axis=2).reshape(D_MODEL, KPAD)

    w_rows = jnp.concatenate([w_in[:, 1024:1024 + KVW], k_padded(1024 + KVW), k_padded(1024 + 2 * KVW)], axis=1)
    eye = jnp.eye(KVH, dtype=F32)
    w1r = w1.reshape(2, 2, N_L, HD, HD)
    wab = jnp.einsum('khlde,gf->klgdhfe', w1r, eye).reshape(2, N_L, GW, 2 * GW)
    w2bd = jnp.einsum('kde,gf->kgdfe', w2, eye)
    w2k_pad = jnp.concatenate([w2bd[0], jnp.zeros_like(w2bd[0])], axis=3).reshape(GW, KPAD)
    per = pe.reshape(2, 2, N_L, 1, 1, HD)
    pe_t = jnp.broadcast_to(per, (2, 2, N_L, 8, KVH, HD)).transpose(0, 2, 1, 3, 4, 5).reshape(2, N_L, PE_ROWS, GW)
    b = lambda a: a.astype(BF16)
    return dict(w_sample=b(w_full), w_rows=b(w_rows), w_t=b(w_full.T), wo=b(w_out), pe_t=b(pe_t), wab=b(wab),
                w2bd=b(w2bd.reshape(2, GW, GW)), w2k_pad=b(w2k_pad))


def _nsa_prompt_layer(x, nw, mod, prep, bsz, t, layer, n_layers, kv_prev):
    rows, kpad, qt, gt, vt, *kv_t = _nsa_proj(x, nw, mod, prep["w_rows"], prep["w_t"], bsz, t, layer, n_layers,
                                              kv_prev)
    kc, vct = _cmp_prompt(rows.reshape(bsz, t, KVW), prep["pe_t"], prep["wab"], prep["w2k_pad"],
                          prep["w2bd"][1])
    return _nsa_prompt_attention(qt, gt, kc, vct, kpad, vt), tuple(kv_t)


def _prep_ssd(w_in, dt_bias, a_log, d_skip):
    pad = SSD_NP - w_in.shape[1]
    w = jnp.concatenate([w_in, jnp.zeros((D_MODEL, pad), F32)], axis=1).astype(BF16)
    pad_h = lambda v: jnp.concatenate([v, jnp.zeros((LANES - SSD_HEADS,), F32)]).reshape(1, LANES)
    return w, pad_h(dt_bias), pad_h(a_log), jnp.repeat(d_skip, SSD_P).reshape(1, SSD_D_INNER)


def kernel(x_prompt, x_sample, cache_kv_cmp, cache_kv_sel, cache_kv_win, state_ssm, state_conv, page_table, c_prompt, c_sample, ada_w, ada_b, norm_w, mlp_w1, mlp_w2, nsa_w_in, nsa_w_out, nsa_cmp_pe, nsa_cmp_w1, nsa_cmp_w2, ssd_w_in, ssd_conv_w, ssd_conv_b, ssd_dt_bias, ssd_a_log, ssd_d, ssd_norm_w, ssd_w_out, final_norm_w):
    bp, t, d = x_prompt.shape
    bs = x_sample.shape[0]
    xp = x_prompt.reshape(bp * t, d)
    xs = x_sample.reshape(bs, d)
    mods = _adaln(jnp.concatenate([c_prompt, c_sample], axis=0), ada_w, ada_b)
    w1b = mlp_w1.astype(BF16)
    w2b = mlp_w2.astype(BF16)
    fnw = final_norm_w.reshape(1, d)
    tm = ROW_BLOCK
    n_nsa = cache_kv_cmp.shape[0]
    cmp_view = jnp.transpose(cache_kv_cmp, (0, 1, 3, 4, 5, 2))
    sel_view = jnp.transpose(cache_kv_sel, (0, 1, 3, 4, 5, 2))
    win_view = jnp.transpose(cache_kv_win, (0, 1, 3, 4, 5, 2))
    win_s = ssm_s = kv_p = None
    outs = {k: [] for k in ("cmp_s", "sel_s", "ssm_p", "conv_p", "conv_s")}
    kv_rows = lambda a, n: a.reshape(n + (2, KVH, HD))

    for i in range(DEPTH):
        jl = i // 2
        mp = mods[i, :bp].reshape(bp, 1, 6 * d)
        ms = mods[i, bp:].reshape(1, bs, 6 * d)
        nw0 = norm_w[i, 0].reshape(1, d)
        nw1 = norm_w[i, 1].reshape(1, d)
        if i % 2 == 0:
            prep = _prep_nsa(nsa_w_in[jl], nsa_w_out[jl], nsa_cmp_pe[jl], nsa_cmp_w1[jl], nsa_cmp_w2[jl])
            wo = prep["wo"]
            ap, kv_p = _nsa_prompt_layer(xp, nw0, mp, prep, bp, t, jl, n_nsa, kv_p)
            ps = _mod_matmul(xs, nw0, ms, prep["w_sample"], tm=bs, tn=SAMPLE_COL_BLOCK, rows_per_mod=None)
            as_, win_s = _nsa_sample(ps, cmp_view, sel_view, win_view, jl, win_s, page_table, prep["pe_t"],
                                     prep["wab"], prep["w2bd"])
            outs["cmp_s"].append(kv_rows(ps[:, 1024:1024 + KVW], (bs, 1)))
            outs["sel_s"].append(kv_rows(ps[:, 1024 + KVW:1024 + 2 * KVW], (bs, 1)))
        else:
            w, dtb, alog, dsk = _prep_ssd(ssd_w_in[jl], ssd_dt_bias[jl], ssd_a_log[jl], ssd_d[jl])
            wo = ssd_w_out[jl].astype(BF16)
            snw = ssd_norm_w[jl].reshape(1, SSD_D_INNER)
            cw = ssd_conv_w[jl]
            cb = ssd_conv_b[jl].reshape(1, SSD_CONV_DIM)
            pp = _mod_matmul(xp, nw0, mp, w, tm=tm, tn=SSD_NP, rows_per_mod=t)
            ps = _mod_matmul(xs, nw0, ms, w, tm=bs, tn=SAMPLE_COL_BLOCK, rows_per_mod=None)
            ap, conv_p, ssm_p = _ssd_prompt(pp, bp, t, cw, cb, dtb, alog, dsk, snw)
            as_, conv_s, ssm_s = _ssd_sample(ps, state_conv[jl], state_ssm, jl, ssm_s, cw, cb, dtb, alog, dsk, snw)
            outs["conv_p"].append(conv_p)
            outs["conv_s"].append(conv_s)
            outs["ssm_p"].append(ssm_p.reshape(bp, SSD_HEADS, SSD_P, SSD_N))
        last = i == DEPTH - 1
        xp = _post_mlp(xp, ap, wo, mp, nw1, w1b[i], w2b[i], fnw, tm=tm, tf=FF_CHUNK, rows_per_mod=t, final_norm=last)
        xs = _post_mlp(xs, as_, wo, ms, nw1, w1b[i], w2b[i], fnw, tm=bs, tf=FF_CHUNK, rows_per_mod=None,
                       final_norm=last)

    st = lambda k: jnp.stack(outs[k])
    time_major = lambda a: jnp.transpose(a, (0, 1, 5, 2, 3, 4))
    cmp_p, sel_p, win_p = (time_major(a.reshape(a.shape[:2] + (2, KVH, HD, a.shape[-1]))) for a in kv_p)
    return (xp.reshape(bp, t, d), xs.reshape(bs, 1, d), cmp_p, st("cmp_s"), sel_p, st("sel_s"), win_p,
            time_major(win_s), st("ssm_p"), ssm_s.reshape(state_ssm.shape), st("conv_p"), st("conv_s"))
```
